```python
import math
import jax
import jax.numpy as jnp
from jax import lax
import numpy as np

D_MODEL = 2048
BATCH = 2
SEQ = 4096
DEPTH = 2
DEC_BATCH = 32
DEC_SEQ = 8
PAST_LEN = 8192
PAGE_SIZE = 128

N_EVEN = (DEPTH + 1) // 2
N_ODD = DEPTH // 2
EPS = 1e-6
N_MOD = 9
D_FF = 5632
CONV_W = 4
CHUNK = 64
ROPE_THETA = 10000.0

GLA_HEADS = 4
GLA_DK = 128
GLA_DV = 256
GLA_RANK = 16
GLA_TAU = 16.0
GLA_QK = GLA_HEADS * GLA_DK
GLA_VW = GLA_HEADS * GLA_DV

SSD_HEADS = 32
SSD_P = 64
SSD_STATE = 128
SSD_GROUPS = 4
SSD_HPG = SSD_HEADS // SSD_GROUPS
SSD_INNER = SSD_HEADS * SSD_P
SSD_CONV_DIM = SSD_INNER + 2 * SSD_GROUPS * SSD_STATE

GDN_QK_HEADS = 8
GDN_V_HEADS = 16
GDN_DK = 128
GDN_DV = 128
GDN_QK = GDN_QK_HEADS * GDN_DK
GDN_VW = GDN_V_HEADS * GDN_DV
GDN_CONV_DIM = 2 * GDN_QK + GDN_VW

DSA_GROUPS = ((128, 1), (512, 4), (2048, 16))
DSA_HEADS = 8
DSA_HD = 128
DSA_BLOCK = 128
DSA_W = len(DSA_GROUPS) * DSA_HEADS * DSA_HD

EVEN_SIZES = (GLA_QK, GLA_QK, GLA_VW, GLA_VW, GLA_RANK, SSD_INNER, SSD_CONV_DIM, SSD_HEADS)
EVEN_IN = sum(EVEN_SIZES)
EVEN_OUT = GLA_VW + SSD_INNER
ODD_SIZES = (GDN_QK, GDN_QK, GDN_VW, GDN_VW, GDN_V_HEADS, GDN_V_HEADS, DSA_W, DSA_W, DSA_W)
ODD_IN = sum(ODD_SIZES)
ODD_OUT = GDN_VW + DSA_HEADS * DSA_HD
STATE_KEYS = ('gla', 'ssd', 'ssd_conv', 'gdn', 'gdn_conv', 'dsa0', 'dsa1', 'dsa2')

kernel_name = 'hybrid_gla_ssd_gdn_dilated_swa_step'


def _split(a, sizes):
    return jnp.split(a, np.cumsum(sizes)[:-1].tolist(), axis=-1)


def rmsnorm(x, gain):
    xf = x.astype(jnp.float32)
    y = xf * lax.rsqrt(jnp.mean(xf * xf, axis=-1, keepdims=True) + EPS)
    return (y * gain.astype(jnp.float32)).astype(x.dtype)


def l2norm(x):
    return x * lax.rsqrt(jnp.sum(x * x, axis=-1, keepdims=True) + EPS)


def modulate(x, gain, shift, scale):
    return rmsnorm(x, gain) * (1 + scale) + shift


def swiglu(h, w_in, w_out):
    a, b = jnp.split(h @ w_in, 2, axis=-1)
    return (jax.nn.silu(a) * b) @ w_out


def rope(x, pos):
    half = x.shape[-1] // 2
    inv = ROPE_THETA ** (-jnp.arange(half, dtype=jnp.float32) / half)
    ang = pos[:, None] * inv[None, :]
    shape = (pos.shape[0],) + (1,) * (x.ndim - 3) + (half,)
    cos = jnp.cos(ang).reshape(shape)
    sin = jnp.sin(ang).reshape(shape)
    x1, x2 = x[..., :half], x[..., half:]
    return jnp.concatenate([x1 * cos - x2 * sin, x2 * cos + x1 * sin], axis=-1)


def causal_conv(u, buf, w):
    full = jnp.concatenate([buf.astype(u.dtype), u], axis=1)
    out = lax.conv_general_dilated(full, w.astype(u.dtype)[:, None, :], window_strides=(1,), padding='VALID',
                                   dimension_numbers=('NWC', 'WIO', 'NWC'), feature_group_count=u.shape[-1])
    return out, full[:, full.shape[1] - (CONV_W - 1):]


def _chunks(a, C):
    B, T = a.shape[:2]
    return jnp.moveaxis(a.reshape((B, T // C, C) + a.shape[2:]), 1, 0)


def _unchunk(a):
    a = jnp.moveaxis(a, 0, 1)
    return a.reshape((a.shape[0], a.shape[1] * a.shape[2]) + a.shape[3:])


def gla_scan(q, k, v, logf, s0):
    C = math.gcd(q.shape[1], CHUNK)
    incl = jnp.tril(jnp.ones((C, C), dtype=bool))[None, :, :, None, None]

    def step(S, inp):
        qi, ki, vi, gi = inp
        b = jnp.cumsum(gi, axis=1)
        decay = jnp.exp(jnp.where(incl, b[:, :, None] - b[:, None], -jnp.inf))
        att = jnp.sum(qi[:, :, None] * ki[:, None] * decay, axis=-1)
        o = jnp.einsum('btsh,bshv->bthv', att, vi) + jnp.einsum('bthk,bhkv->bthv', qi * jnp.exp(b), S)
        last = b[:, -1]
        S = S * jnp.exp(last)[..., None] + jnp.einsum('bshk,bshv->bhkv', ki * jnp.exp(last[:, None] - b), vi)
        return S, o

    S, o = lax.scan(step, s0, tuple(_chunks(a, C) for a in (q, k, v, logf)))
    return _unchunk(o), S


def ssd_scan(x, dt, A, Bm, Cm, s0):
    C = math.gcd(x.shape[1], CHUNK)
    incl = jnp.tril(jnp.ones((C, C), dtype=bool))[None, :, :, None, None]

    def step(S, inp):
        xi, ai, bi, ci = inp
        cum = jnp.cumsum(ai, axis=1)
        decay = jnp.exp(jnp.where(incl, cum[:, :, None] - cum[:, None], -jnp.inf))
        att = jnp.einsum('btgn,bsgn->btsg', ci, bi)[..., None] * decay
        y = (jnp.einsum('btsgh,bsghp->btghp', att, xi)
             + jnp.einsum('btgn,bghpn->btghp', ci, S) * jnp.exp(cum)[..., None])
        last = cum[:, -1]
        S = (S * jnp.exp(last)[..., None, None]
             + jnp.einsum('bsgn,bsghp->bghpn', bi, xi * jnp.exp(last[:, None] - cum)[..., None]))
        return S, y

    S, y = lax.scan(step, s0, tuple(_chunks(a, C) for a in (x * dt[..., None], dt * A, Bm, Cm)))
    return _unchunk(y), S


def gdn_scan(q, k, v, beta, g, s0):
    V = v.shape[-1]
    C = math.gcd(q.shape[1], CHUNK)
    strict = jnp.tril(jnp.ones((C, C), dtype=bool), -1)
    incl = jnp.tril(jnp.ones((C, C), dtype=bool))

    def step(S, inp):
        qi, ki, vi, bi, gi = inp
        cum = jnp.cumsum(gi, axis=1)
        seg = jnp.moveaxis(cum[:, :, None] - cum[:, None], -1, 1)
        a_mat = (jnp.moveaxis(bi, -1, 1)[..., None] * jnp.einsum('bthk,bshk->bhts', ki, ki)
                 * jnp.exp(jnp.where(strict, seg, -jnp.inf)))
        rhs = jnp.concatenate([vi * bi[..., None], ki * (bi * jnp.exp(cum))[..., None]], axis=-1)
        sol = lax.linalg.triangular_solve(a_mat, jnp.moveaxis(rhs, 1, 2), left_side=True, lower=True,
                                          unit_diagonal=True)
        delta = sol[..., :V] - jnp.einsum('bhtk,bhkv->bhtv', sol[..., V:], S)
        qk = jnp.einsum('bthk,bshk->bhts', qi, ki) * jnp.exp(jnp.where(incl, seg, -jnp.inf))
        o = (jnp.einsum('bthk,bhkv->bthv', qi * jnp.exp(cum)[..., None], S)
             + jnp.einsum('bhts,bhsv->bthv', qk, delta))
        last = cum[:, -1]
        S = (S * jnp.exp(last)[..., None, None]
             + jnp.einsum('bshk,bhsv->bhkv', ki * jnp.exp(last[:, None] - cum)[..., None], delta))
        return S, o

    S, o = lax.scan(step, s0, tuple(_chunks(a, C) for a in (q, k, v, beta, g)))
    return _unchunk(o), S


def dsa_prompt(q, k, v, window, dil):
    B, T, H, E = q.shape
    J = window // dil
    Ls = T // dil
    nb = -(-Ls // DSA_BLOCK)
    Lp = nb * DSA_BLOCK
    qs = jnp.pad(q.reshape(B, Ls, dil, H, E), ((0, 0), (0, Lp - Ls), (0, 0), (0, 0), (0, 0)))
    qs = qs.reshape(B, nb, DSA_BLOCK, dil, H, E)

    def key_blocks(a):
        ap = jnp.pad(a.reshape(B, Ls, dil, H, E), ((0, 0), (DSA_BLOCK, Lp - Ls), (0, 0), (0, 0), (0, 0)))
        prev = ap[:, :Lp].reshape(B, nb, DSA_BLOCK, dil, H, E)
        cur = ap[:, DSA_BLOCK:].reshape(B, nb, DSA_BLOCK, dil, H, E)
        return jnp.concatenate([prev, cur], axis=2)

    kb, vb = key_blocks(k), key_blocks(v)
    s = jnp.einsum('bnirhe,bnmrhe->bnrhim', qs, kb) * E ** -0.5
    i_idx = jnp.arange(DSA_BLOCK)[:, None]
    m_idx = jnp.arange(2 * DSA_BLOCK)[None, :]
    dist = i_idx + DSA_BLOCK - m_idx
    u_key = jnp.arange(nb)[:, None, None] * DSA_BLOCK - DSA_BLOCK + m_idx[None]
    valid = (dist >= 0) & (dist <= J) & (u_key >= 0)
    s = jnp.where(valid[None, :, None, None], s, -jnp.inf)
    lse = jax.nn.logsumexp(s, axis=-1)
    o = jnp.einsum('bnrhim,bnmrhe->bnirhe', jnp.exp(s - lse[..., None]), vb)
    o = o.reshape(B, Lp, dil, H, E)[:, :Ls].reshape(B, T, H, E)
    lse = lse.transpose(0, 1, 4, 2, 3).reshape(B, Lp, dil, H)[:, :Ls].reshape(B, T, H)
    return o, lse


def dsa_sample(q, k, v, buf, window, dil):
    B, S, H, E = q.shape
    L = buf.shape[1]
    J = window // dil
    k_all = jnp.concatenate([buf[:, :, 0].astype(k.dtype), k], axis=1)
    v_all = jnp.concatenate([buf[:, :, 1].astype(v.dtype), v], axis=1)
    idx = L + jnp.arange(S)[:, None] - dil * jnp.arange(J + 1)[None, :]
    valid = idx >= 0
    idx = jnp.maximum(idx, 0)
    kg = jnp.take(k_all, idx, axis=1)
    vg = jnp.take(v_all, idx, axis=1)
    s = jnp.einsum('bshe,bsjhe->bhsj', q, kg) * E ** -0.5
    s = jnp.where(valid[None, None], s, -jnp.inf)
    lse = jax.nn.logsumexp(s, axis=-1)
    o = jnp.einsum('bhsj,bsjhe->bshe', jnp.exp(s - lse[..., None]), vg)
    return o, jnp.moveaxis(lse, 1, 2)


def mixer_even(h, gla_s0, ssd_s0, ssd_conv0, w_in, gla_w_gate2, gla_b_gate, gla_norm, ssd_conv_w,
               ssd_conv_b, ssd_dt_bias, ssd_A_log, ssd_D, ssd_norm, w_out):
    B, T, _ = h.shape
    f32 = jnp.float32
    q, k, v, r, glr, z, xbc, dt = _split((h @ w_in).astype(f32), EVEN_SIZES)
    q = q.reshape(B, T, GLA_HEADS, GLA_DK) * GLA_DK ** -0.5
    k = k.reshape(B, T, GLA_HEADS, GLA_DK)
    v = v.reshape(B, T, GLA_HEADS, GLA_DV)
    logf = jax.nn.log_sigmoid(glr @ gla_w_gate2.astype(f32) + gla_b_gate) / GLA_TAU
    o, gla_s = gla_scan(q, k, v, logf.reshape(B, T, GLA_HEADS, GLA_DK), gla_s0.astype(f32))
    o_gla = rmsnorm(o, gla_norm).reshape(B, T, GLA_VW) * jax.nn.silu(r)
    xbc, conv_s = causal_conv(xbc, ssd_conv0, ssd_conv_w)
    xbc = jax.nn.silu(xbc + ssd_conv_b)
    xs, Bm, Cm = _split(xbc, (SSD_INNER, SSD_GROUPS * SSD_STATE, SSD_GROUPS * SSD_STATE))
    xs = xs.reshape(B, T, SSD_GROUPS, SSD_HPG, SSD_P)
    Bm = Bm.reshape(B, T, SSD_GROUPS, SSD_STATE)
    Cm = Cm.reshape(B, T, SSD_GROUPS, SSD_STATE)
    dt = jax.nn.softplus(dt + ssd_dt_bias).reshape(B, T, SSD_GROUPS, SSD_HPG)
    A = -jnp.exp(ssd_A_log.astype(f32)).reshape(SSD_GROUPS, SSD_HPG)
    s0 = ssd_s0.astype(f32).reshape(B, SSD_GROUPS, SSD_HPG, SSD_P, SSD_STATE)
    y, ssd_s = ssd_scan(xs, dt, A, Bm, Cm, s0)
    y = y + xs * ssd_D.reshape(SSD_GROUPS, SSD_HPG, 1)
    y = rmsnorm(y.reshape(B, T, SSD_INNER) * jax.nn.silu(z), ssd_norm)
    out = jnp.concatenate([o_gla, y], axis=-1).astype(h.dtype) @ w_out
    return out, gla_s, ssd_s.reshape(B, SSD_HEADS, SSD_P, SSD_STATE), conv_s


def mixer_odd(h, pos, gdn_s0, gdn_conv0, dsa_bufs, w_in, gdn_conv_w, gdn_A_log, gdn_dt_bias, gdn_norm, w_out):
    B, T, _ = h.shape
    f32 = jnp.float32
    q, k, v, z, b, a, dq, dk, dv = _split((h @ w_in).astype(f32), ODD_SIZES)
    qkv, conv_s = causal_conv(jnp.concatenate([q, k, v], axis=-1), gdn_conv0, gdn_conv_w)
    q, k, v = _split(jax.nn.silu(qkv), (GDN_QK, GDN_QK, GDN_VW))
    rep = GDN_V_HEADS // GDN_QK_HEADS
    q = jnp.repeat(l2norm(q.reshape(B, T, GDN_QK_HEADS, GDN_DK)), rep, axis=2) * GDN_DK ** -0.5
    k = jnp.repeat(l2norm(k.reshape(B, T, GDN_QK_HEADS, GDN_DK)), rep, axis=2)
    v = v.reshape(B, T, GDN_V_HEADS, GDN_DV)
    beta = jax.nn.sigmoid(b)
    g = -jnp.exp(gdn_A_log.astype(f32)) * jax.nn.softplus(a + gdn_dt_bias)
    o, gdn_s = gdn_scan(q, k, v, beta, g, gdn_s0.astype(f32))
    o_gdn = (rmsnorm(o, gdn_norm) * jax.nn.silu(z.reshape(B, T, GDN_V_HEADS, GDN_DV))).reshape(B, T, GDN_VW)
    shp = (B, T, len(DSA_GROUPS), DSA_HEADS, DSA_HD)
    dq = rope(dq.reshape(shp), pos)
    dk = rope(dk.reshape(shp), pos)
    dv = dv.reshape(shp)
    outs, lses, rows = [], [], []
    for gi, (window, dil) in enumerate(DSA_GROUPS):
        qg, kg, vg = dq[:, :, gi], dk[:, :, gi], dv[:, :, gi]
        if dsa_bufs is None:
            o_g, lse_g = dsa_prompt(qg, kg, vg, window, dil)
            keep = min(window, T)
            rows.append(jnp.stack([kg[:, T - keep:], vg[:, T - keep:]], axis=2))
        else:
            o_g, lse_g = dsa_sample(qg, kg, vg, dsa_bufs[gi], window, dil)
            rows.append(jnp.stack([kg, vg], axis=2))
        outs.append(o_g)
        lses.append(lse_g)
    wts = jax.nn.softmax(jnp.stack(lses), axis=0)
    o_dsa = jnp.sum(wts[..., None] * jnp.stack(outs), axis=0).reshape(B, T, DSA_HEADS * DSA_HD)
    out = jnp.concatenate([o_gdn, o_dsa], axis=-1).astype(h.dtype) @ w_out
    return out, gdn_s, conv_s, rows


def run_trunk(x, c, pos0, gla0, ssd0, ssd_conv0, gdn0, gdn_conv0, dsa0, W):
    T = x.shape[1]
    pos = pos0 + jnp.arange(T, dtype=jnp.float32)
    new = {name: [] for name in STATE_KEYS}
    cond = jax.nn.silu(c)
    for l in range(DEPTH):
        mod = (cond @ W['w_ada'][l] + W['b_ada'][l])[:, None, :]
        sh1, sc1, g1, shm, scm, gm, sh2, sc2, g2 = jnp.split(mod, N_MOD, axis=-1)
        h = modulate(x, W['ln_ffn1'][l], sh1, sc1)
        x = x + 0.5 * g1 * swiglu(h, W['ffn1_w_in'][l], W['ffn1_w_out'][l])
        h = modulate(x, W['ln_mix'][l], shm, scm)
        i = l // 2
        if l % 2 == 0:
            m, s_gla, s_ssd, s_conv = mixer_even(
                h, gla0[i], ssd0[i], ssd_conv0[i], W['even_w_in'][i], W['gla_w_gate2'][i], W['gla_b_gate'][i],
                W['gla_norm'][i], W['ssd_conv_w'][i], W['ssd_conv_b'][i], W['ssd_dt_bias'][i], W['ssd_A_log'][i],
                W['ssd_D'][i], W['ssd_norm'][i], W['even_w_out'][i])
            new['gla'].append(s_gla)
            new['ssd'].append(s_ssd)
            new['ssd_conv'].append(s_conv)
        else:
            bufs = None if dsa0 is None else tuple(buf[i] for buf in dsa0)
            m, s_gdn, s_conv, rows = mixer_odd(
                h, pos, gdn0[i], gdn_conv0[i], bufs, W['odd_w_in'][i], W['gdn_conv_w'][i], W['gdn_A_log'][i],
                W['gdn_dt_bias'][i], W['gdn_norm'][i], W['odd_w_out'][i])
            new['gdn'].append(s_gdn)
            new['gdn_conv'].append(s_conv)
            for gi in range(len(DSA_GROUPS)):
                new['dsa%d' % gi].append(rows[gi])
        x = x + gm * m
        h = modulate(x, W['ln_ffn2'][l], sh2, sc2)
        x = x + 0.5 * g2 * swiglu(h, W['ffn2_w_in'][l], W['ffn2_w_out'][l])
    y = rmsnorm(x, W['final_norm'])
    return y, tuple(jnp.stack(new[name]).astype(x.dtype) for name in STATE_KEYS)


def setup_inputs(seed: int = 0) -> dict:
    key = jax.random.key(seed)
    ks = iter(jax.random.split(key, 64))
    f32 = jnp.float32
    D = D_MODEL

    def nrm(shape, scale=1.0):
        return jax.random.normal(next(ks), shape, f32) * scale

    def gain(shape):
        return 1.0 + nrm(shape, 0.02)

    def unif(shape, lo, hi):
        return jax.random.uniform(next(ks), shape, f32, lo, hi)

    def dt_bias(shape):
        dt = jnp.exp(unif(shape, math.log(1e-3), math.log(1e-1)))
        return dt + jnp.log(-jnp.expm1(-dt))

    n_dsa = [min(w, PAST_LEN) for w, _ in DSA_GROUPS]
    return {
        'x_prompt': nrm((BATCH, SEQ, D)),
        'x_sample': nrm((DEC_BATCH, DEC_SEQ, D)),
        'c_prompt': nrm((BATCH, D)),
        'c_sample': nrm((DEC_BATCH, D)),
        'state_gla': nrm((N_EVEN, DEC_BATCH, GLA_HEADS, GLA_DK, GLA_DV), 0.1),
        'state_ssd': nrm((N_EVEN, DEC_BATCH, SSD_HEADS, SSD_P, SSD_STATE), 0.1),
        'state_ssd_conv': nrm((N_EVEN, DEC_BATCH, CONV_W - 1, SSD_CONV_DIM)),
        'state_gdn': nrm((N_ODD, DEC_BATCH, GDN_V_HEADS, GDN_DK, GDN_DV), 0.1),
        'state_gdn_conv': nrm((N_ODD, DEC_BATCH, CONV_W - 1, GDN_CONV_DIM)),
        'cache_dsa_w128': nrm((N_ODD, DEC_BATCH, n_dsa[0], 2, DSA_HEADS, DSA_HD)),
        'cache_dsa_w512': nrm((N_ODD, DEC_BATCH, n_dsa[1], 2, DSA_HEADS, DSA_HD)),
        'cache_dsa_w2048': nrm((N_ODD, DEC_BATCH, n_dsa[2], 2, DSA_HEADS, DSA_HD)),
        'ln_ffn1': gain((DEPTH, D)),
        'ln_mix': gain((DEPTH, D)),
        'ln_ffn2': gain((DEPTH, D)),
        'w_ada': nrm((DEPTH, D, N_MOD * D), 0.5 * D ** -0.5),
        'b_ada': nrm((DEPTH, N_MOD * D), 0.01),
        'ffn1_w_in': nrm((DEPTH, D, 2 * D_FF), D ** -0.5),
        'ffn1_w_out': nrm((DEPTH, D_FF, D), D_FF ** -0.5),
        'ffn2_w_in': nrm((DEPTH, D, 2 * D_FF), D ** -0.5),
        'ffn2_w_out': nrm((DEPTH, D_FF, D), D_FF ** -0.5),
        'even_w_in': nrm((N_EVEN, D, EVEN_IN), D ** -0.5),
        'gla_w_gate2': nrm((N_EVEN, GLA_RANK, GLA_QK), GLA_RANK ** -0.5),
        'gla_b_gate': nrm((N_EVEN, GLA_QK), 0.1),
        'gla_norm': gain((N_EVEN, GLA_DV)),
        'ssd_conv_w': nrm((N_EVEN, CONV_W, SSD_CONV_DIM), CONV_W ** -0.5),
        'ssd_conv_b': nrm((N_EVEN, SSD_CONV_DIM), 0.02),
        'ssd_dt_bias': dt_bias((N_EVEN, SSD_HEADS)),
        'ssd_A_log': jnp.log(unif((N_EVEN, SSD_HEADS), 1.0, 16.0)),
        'ssd_D': gain((N_EVEN, SSD_HEADS)),
        'ssd_norm': gain((N_EVEN, SSD_INNER)),
        'even_w_out': nrm((N_EVEN, EVEN_OUT, D), EVEN_OUT ** -0.5),
        'odd_w_in': nrm((N_ODD, D, ODD_IN), D ** -0.5),
        'gdn_conv_w': nrm((N_ODD, CONV_W, GDN_CONV_DIM), CONV_W ** -0.5),
        'gdn_A_log': jnp.log(unif((N_ODD, GDN_V_HEADS), 1.0, 16.0)),
        'gdn_dt_bias': dt_bias((N_ODD, GDN_V_HEADS)),
        'gdn_norm': gain((N_ODD, GDN_DV)),
        'odd_w_out': nrm((N_ODD, ODD_OUT, D), ODD_OUT ** -0.5),
        'final_norm': gain((D,)),
    }


def reference(x_prompt, x_sample, c_prompt, c_sample, state_gla, state_ssd, state_ssd_conv, state_gdn,
              state_gdn_conv, cache_dsa_w128, cache_dsa_w512, cache_dsa_w2048, ln_ffn1, ln_mix, ln_ffn2,
              w_ada, b_ada, ffn1_w_in, ffn1_w_out, ffn2_w_in, ffn2_w_out, even_w_in, gla_w_gate2, gla_b_gate,
              gla_norm, ssd_conv_w, ssd_conv_b, ssd_dt_bias, ssd_A_log, ssd_D, ssd_norm, even_w_out, odd_w_in,
              gdn_conv_w, gdn_A_log, gdn_dt_bias, gdn_norm, odd_w_out, final_norm):
    W = dict(ln_ffn1=ln_ffn1, ln_mix=ln_mix, ln_ffn2=ln_ffn2, w_ada=w_ada, b_ada=b_ada,
             ffn1_w_in=ffn1_w_in, ffn1_w_out=ffn1_w_out, ffn2_w_in=ffn2_w_in, ffn2_w_out=ffn2_w_out,
             even_w_in=even_w_in, gla_w_gate2=gla_w_gate2, gla_b_gate=gla_b_gate, gla_norm=gla_norm,
             ssd_conv_w=ssd_conv_w, ssd_conv_b=ssd_conv_b, ssd_dt_bias=ssd_dt_bias, ssd_A_log=ssd_A_log,
             ssd_D=ssd_D, ssd_norm=ssd_norm, even_w_out=even_w_out, odd_w_in=odd_w_in, gdn_conv_w=gdn_conv_w,
             gdn_A_log=gdn_A_log, gdn_dt_bias=gdn_dt_bias, gdn_norm=gdn_norm, odd_w_out=odd_w_out,
             final_norm=final_norm)
    B = x_prompt.shape[0]
    f32 = jnp.float32
    y_prompt, (gla_p, ssd_p, ssd_conv_p, gdn_p, gdn_conv_p, dsa0_p, dsa1_p, dsa2_p) = run_trunk(
        x_prompt, c_prompt, 0,
        jnp.zeros((N_EVEN, B, GLA_HEADS, GLA_DK, GLA_DV), f32),
        jnp.zeros((N_EVEN, B, SSD_HEADS, SSD_P, SSD_STATE), f32),
        jnp.zeros((N_EVEN, B, CONV_W - 1, SSD_CONV_DIM), f32),
        jnp.zeros((N_ODD, B, GDN_V_HEADS, GDN_DK, GDN_DV), f32),
        jnp.zeros((N_ODD, B, CONV_W - 1, GDN_CONV_DIM), f32),
        None, W)
    y_sample, (gla_s, ssd_s, ssd_conv_s, gdn_s, gdn_conv_s, dsa0_s, dsa1_s, dsa2_s) = run_trunk(
        x_sample, c_sample, PAST_LEN, state_gla, state_ssd, state_ssd_conv, state_gdn, state_gdn_conv,
        (cache_dsa_w128, cache_dsa_w512, cache_dsa_w2048), W)
    return (y_prompt, y_sample, gla_p, gla_s, ssd_p, ssd_s, ssd_conv_p, ssd_conv_s, gdn_p, gdn_s,
            gdn_conv_p, gdn_conv_s, dsa0_p, dsa0_s, dsa1_p, dsa1_s, dsa2_p, dsa2_s)
```

```python
import functools
import math

import numpy as np
import jax
import jax.numpy as jnp
from jax import lax
from jax.experimental import pallas as pl
from jax.experimental.pallas import tpu as pltpu

F32 = jnp.float32
BF16 = jnp.bfloat16

D_MODEL = 2048
D_FF = 5632
N_MOD = 9
EPS = 1e-6
CONV_W = 4
ROPE_THETA = 10000.0
PAST_LEN = 8192

GLA_HEADS, GLA_DK, GLA_DV, GLA_RANK, GLA_TAU = 4, 128, 256, 16, 16.0
GLA_QK, GLA_VW = GLA_HEADS * GLA_DK, GLA_HEADS * GLA_DV
SSD_HEADS, SSD_P, SSD_STATE, SSD_GROUPS = 32, 64, 128, 4
SSD_HPG = SSD_HEADS // SSD_GROUPS
SSD_INNER = SSD_HEADS * SSD_P
SSD_BC = SSD_GROUPS * SSD_STATE
SSD_CONV_DIM = SSD_INNER + 2 * SSD_BC
GDN_QK_HEADS, GDN_V_HEADS, GDN_DK, GDN_DV = 8, 16, 128, 128
GDN_QK, GDN_VW = GDN_QK_HEADS * GDN_DK, GDN_V_HEADS * GDN_DV
GDN_CONV_DIM = 2 * GDN_QK + GDN_VW
DSA_GROUPS = ((128, 1), (512, 4), (2048, 16))
DSA_HEADS, DSA_HD, DSA_BLOCK = 8, 128, 128
DSA_GW = DSA_HEADS * DSA_HD
DSA_W = len(DSA_GROUPS) * DSA_GW

LANES = 128
SUBLANES = 8
VMEM_LIMIT_BYTES = 56 * 1024 * 1024

ROW_TILE = 256
CHUNK = 64
GDN_HB = 4
NEG_BIG = -1e30

EV_Z, EV_X, EV_V, EV_R, EV_Q, EV_K, EV_B, EV_C = 0, 2048, 4096, 5120, 6144, 6656, 7168, 7680
EV_MAIN = 8192
OD_V, OD_Z, OD_Q, OD_K, OD_DQ, OD_DK, OD_DV = 0, 2048, 4096, 5120, 6144, 9216, 12288
OD_MAIN = 15360


def _cparams(*sem):
    return pltpu.CompilerParams(dimension_semantics=sem, vmem_limit_bytes=VMEM_LIMIT_BYTES)


def _dot(a, b):
    return jnp.dot(a.astype(BF16), b.astype(BF16), preferred_element_type=F32)


def _dot_nt(a, b):
    return lax.dot_general(a.astype(BF16), b.astype(BF16), (((1,), (1,)), ((), ())), preferred_element_type=F32)


def _dot_tn(a, b):
    return lax.dot_general(a.astype(BF16), b.astype(BF16), (((0,), (0,)), ((), ())), preferred_element_type=F32)


def _split2(x):
    hi = x.astype(BF16)
    lo = (x - hi.astype(F32)).astype(BF16)
    return hi, lo


def _dot_exact_rhs(x, e):
    hi, lo = _split2(x)
    eb = e.astype(BF16)
    return (jnp.dot(hi, eb, preferred_element_type=F32) + jnp.dot(lo, eb, preferred_element_type=F32))


def _cumsum_rows(x):
    n = x.shape[0]
    r = lax.broadcasted_iota(jnp.int32, (n, n), 0)
    c = lax.broadcasted_iota(jnp.int32, (n, n), 1)
    tri = jnp.where(r >= c, 1.0, 0.0).astype(BF16)
    hi = x.astype(BF16)
    r1 = x - hi.astype(F32)
    mid = r1.astype(BF16)
    lo = (r1 - mid.astype(F32)).astype(BF16)
    return (jnp.dot(tri, hi, preferred_element_type=F32) + jnp.dot(tri, mid, preferred_element_type=F32)
            + jnp.dot(tri, lo, preferred_element_type=F32))


def _rows_of(x, n_rows):
    r = lax.broadcasted_iota(jnp.int32, (n_rows, x.shape[1]), 0)
    c = lax.broadcasted_iota(jnp.int32, (n_rows, x.shape[1]), 1)
    sel = jnp.where(r == c, 1.0, 0.0).astype(BF16)
    hi = x.astype(BF16)
    r1 = x - hi.astype(F32)
    mid = r1.astype(BF16)
    lo = (r1 - mid.astype(F32)).astype(BF16)
    nt = (((1,), (1,)), ((), ()))
    return (lax.dot_general(sel, hi, nt, preferred_element_type=F32)
            + lax.dot_general(sel, mid, nt, preferred_element_type=F32)
            + lax.dot_general(sel, lo, nt, preferred_element_type=F32))


def _pad_rows(x, n):
    if x.shape[0] == n:
        return x
    return jnp.concatenate([x, jnp.zeros((n - x.shape[0],) + x.shape[1:], x.dtype)], axis=0)


def _silu(x):
    return x * jax.nn.sigmoid(x)


def _tri_masks(n):
    r = lax.broadcasted_iota(jnp.int32, (n, n), 0)
    c = lax.broadcasted_iota(jnp.int32, (n, n), 1)
    return r >= c, r > c, r == c


def _ada_body(c_ref, w_ref, b_ref, o_ref):
    cond = _silu(c_ref[...])
    o_ref[...] = _dot(cond, w_ref[...]) + b_ref[...]


def ada_mod(c_all, w_ada, b_ada):
    depth, d, n = w_ada.shape
    rows = c_all.shape[0]
    tn = 1024
    return pl.pallas_call(
        _ada_body,
        grid=(depth, n // tn),
        in_specs=[pl.BlockSpec((rows, d), lambda l, j: (0, 0)),
                  pl.BlockSpec((None, d, tn), lambda l, j: (l, 0, j)),
                  pl.BlockSpec((None, 1, tn), lambda l, j: (l, 0, j))],
        out_specs=pl.BlockSpec((None, rows, tn), lambda l, j: (l, 0, j)),
        out_shape=jax.ShapeDtypeStruct((depth, rows, n), F32),
        compiler_params=_cparams("arbitrary", "arbitrary"),
        name="ada_mod",
    )(c_all, w_ada, b_ada.reshape(depth, 1, n))


def _rows_body(*refs, has_m, coef, gate_idx, shift_idx, scale_idx, out_x, out_h, out_y, n_prompt_tiles, ts):
    it = iter(refs)
    x_ref = next(it)
    m_ref = next(it) if has_m else None
    gp_ref, gs_ref = (next(it), next(it)) if has_m else (None, None)
    mp_ref, ms_ref = (next(it), next(it)) if out_h else (None, None)
    gain_ref = next(it)
    xo_ref = next(it) if out_x else None
    h_ref = next(it) if out_h else None
    y_ref = next(it) if out_y else None
    i = pl.program_id(0)
    gain = gain_ref[...]

    def compute(shape3, gate, mod):
        x = x_ref[...]
        if shape3 is not None:
            x = x.reshape(shape3)
        if has_m:
            m = m_ref[...]
            if shape3 is not None:
                m = m.reshape(shape3)
            x = x + coef * gate(gate_idx) * m
        if out_x:
            xo_ref[...] = x.reshape(xo_ref.shape)
        var = jnp.mean(x * x, axis=-1, keepdims=True)
        y = x * lax.rsqrt(var + EPS) * gain
        if out_y:
            y_ref[...] = y.reshape(y_ref.shape)
        if out_h:
            h = y * (1.0 + mod(scale_idx)) + mod(shift_idx)
            h_ref[...] = h.reshape(h_ref.shape).astype(h_ref.dtype)

    @pl.when(i < n_prompt_tiles)
    def _():
        compute(None, lambda k: gp_ref[0, k:k + 1, :], lambda k: mp_ref[0, k:k + 1, :])

    @pl.when(i >= n_prompt_tiles)
    def _():
        rt, d = x_ref.shape
        compute((rt // ts, ts, d), lambda k: gs_ref[:, k:k + 1, :], lambda k: ms_ref[:, k:k + 1, :])


def rows_update(x, m, mod4, gain, geom, *, gate_layer=0, mod_layer=0, coef=0.0, gate_idx=0, shift_idx=0,
                scale_idx=0, out_x=False, out_h=False, out_y=False):
    bp, tp, bs, ts = geom
    mrows, d = x.shape
    n_prompt_tiles = bp * tp // ROW_TILE
    tiles_per_seq = tp // ROW_TILE
    grid = (mrows // ROW_TILE,)
    has_m = m is not None
    row_spec = pl.BlockSpec((ROW_TILE, d), lambda i: (i, 0))

    def mod_specs(layer):
        return [pl.BlockSpec((None, 1, N_MOD, d),
                             lambda i: (layer, bs + jnp.minimum(i // tiles_per_seq, bp - 1), 0, 0)),
                pl.BlockSpec((None, bs, N_MOD, d), lambda i: (layer, 0, 0, 0))]

    in_specs = [row_spec]
    args = [x]
    if has_m:
        in_specs += [row_spec] + mod_specs(gate_layer)
        args += [m, mod4, mod4]
    if out_h:
        in_specs += mod_specs(mod_layer)
        args += [mod4, mod4]
    in_specs.append(pl.BlockSpec((1, d), lambda i: (0, 0)))
    args.append(gain.reshape(1, d))
    out_specs, out_shape = [], []
    if out_x:
        out_specs.append(row_spec)
        out_shape.append(jax.ShapeDtypeStruct((mrows, d), F32))
    if out_h:
        out_specs.append(row_spec)
        out_shape.append(jax.ShapeDtypeStruct((mrows, d), BF16))
    if out_y:
        out_specs.append(row_spec)
        out_shape.append(jax.ShapeDtypeStruct((mrows, d), F32))
    body = functools.partial(_rows_body, has_m=has_m, coef=coef, gate_idx=gate_idx, shift_idx=shift_idx,
                             scale_idx=scale_idx, out_x=out_x, out_h=out_h, out_y=out_y,
                             n_prompt_tiles=n_prompt_tiles, ts=ts)
    return pl.pallas_call(
        body, grid=grid, in_specs=in_specs, out_specs=out_specs, out_shape=out_shape,
        compiler_params=_cparams("arbitrary"), name="rows_update",
    )(*args)


def _mm_body(a_ref, w_ref, o_ref, wb_ref):
    @pl.when(pl.program_id(1) == 0)
    def _():
        wb_ref[...] = w_ref[...].astype(BF16)

    o_ref[...] = jnp.dot(a_ref[...], wb_ref[...], preferred_element_type=F32).astype(o_ref.dtype)


def _mm_swiglu_body(a_ref, wa_ref, wg_ref, o_ref, wb_ref):
    tn = o_ref.shape[1]

    @pl.when(pl.program_id(1) == 0)
    def _():
        wb_ref[:, :tn] = wa_ref[...].astype(BF16)
        wb_ref[:, tn:] = wg_ref[...].astype(BF16)

    u = jnp.dot(a_ref[...], wb_ref[...], preferred_element_type=F32)
    o_ref[...] = (_silu(u[:, :tn]) * u[:, tn:]).astype(o_ref.dtype)


def _pick(n, cands):
    for c in cands:
        if n % c == 0:
            return c
    raise ValueError(f"no tile for {n} in {cands}")


def matmul(a, w, *, layer=None, tm_cands=(768, 512, 256), tn=512, out_dtype=F32, name="matmul"):
    mrows, k = a.shape
    n = w.shape[-1]
    tm = _pick(mrows, tm_cands)
    if w.ndim == 3:
        w_spec = pl.BlockSpec((None, k, tn), lambda j, i: (layer, 0, j))
    else:
        w_spec = pl.BlockSpec((k, tn), lambda j, i: (0, j))
    return pl.pallas_call(
        _mm_body,
        grid=(n // tn, mrows // tm),
        in_specs=[pl.BlockSpec((tm, k), lambda j, i: (i, 0)), w_spec],
        out_specs=pl.BlockSpec((tm, tn), lambda j, i: (i, j)),
        out_shape=jax.ShapeDtypeStruct((mrows, n), out_dtype),
        scratch_shapes=[pltpu.VMEM((k, tn), BF16)],
        compiler_params=_cparams("arbitrary", "arbitrary"),
        name=name,
    )(a, w)


def matmul_swiglu(a, w_in, layer, *, tn=512, tm_cands=(768, 512, 256)):
    mrows, k = a.shape
    f = w_in.shape[-1] // 2
    tm = _pick(mrows, tm_cands)
    nt = f // tn
    return pl.pallas_call(
        _mm_swiglu_body,
        grid=(nt, mrows // tm),
        in_specs=[pl.BlockSpec((tm, k), lambda j, i: (i, 0)),
                  pl.BlockSpec((None, k, tn), lambda j, i: (layer, 0, j)),
                  pl.BlockSpec((None, k, tn), lambda j, i: (layer, 0, j + nt))],
        out_specs=pl.BlockSpec((tm, tn), lambda j, i: (i, j)),
        out_shape=jax.ShapeDtypeStruct((mrows, f), BF16),
        scratch_shapes=[pltpu.VMEM((k, 2 * tn), BF16)],
        compiler_params=_cparams("arbitrary", "arbitrary"),
        name="ffn_in_swiglu",
    )(a, w_in, w_in)


def _conv_chunk(ext_ref, raw, w_ref, cin):
    ext_ref[SUBLANES:SUBLANES + CHUNK, :] = _pad_rows(raw, CHUNK)
    acc = None
    for j in range(CONV_W):
        off = SUBLANES - (CONV_W - 1) + j
        term = ext_ref[off:off + CHUNK, :] * w_ref[j:j + 1, :]
        acc = term if acc is None else acc + term
    return acc


def _conv_init(ext_ref, c0_ref, has_state):
    ext_ref[0:SUBLANES, :] = jnp.zeros((SUBLANES, ext_ref.shape[1]), F32)
    if has_state:
        ext_ref[SUBLANES - (CONV_W - 1):SUBLANES, :] = c0_ref[0]


def _conv_tail(ext_ref, cin):
    tail = ext_ref[cin + SUBLANES - (CONV_W - 1):cin + SUBLANES, :]
    return tail


def _conv_advance(ext_ref):
    ext_ref[0:SUBLANES, :] = ext_ref[CHUNK:CHUNK + SUBLANES, :]


def _gla_body(q_ref, k_ref, v_ref, r_ref, g_ref, wg_ref, bg_ref, nrm_ref, s0_ref, o_ref, so_ref, st_ref,
              *, cin, has_state):
    c = pl.program_id(2)

    @pl.when(c == 0)
    def _():
        if has_state:
            st_ref[...] = s0_ref[0, 0].T
        else:
            st_ref[...] = jnp.zeros(st_ref.shape, F32)

    q = _pad_rows(q_ref[...], CHUNK) * (GLA_DK ** -0.5)
    k = _pad_rows(k_ref[...], CHUNK)
    v = _pad_rows(v_ref[...], CHUNK)
    glr = _pad_rows(g_ref[...], CHUNK)
    x = _dot(glr, wg_ref[...]) + bg_ref[...]
    logf = jax.nn.log_sigmoid(x) * (1.0 / GLA_TAU)
    if cin < CHUNK:
        rows = lax.broadcasted_iota(jnp.int32, logf.shape, 0)
        logf = jnp.where(rows < cin, logf, 0.0)
    b = _cumsum_rows(logf)
    bmid = b[CHUNK // 2 - 1:CHUNK // 2, :]
    blast = b[CHUNK - 1:CHUNK, :]
    incl, _, _ = _tri_masks(CHUNK)
    att = _dot_nt(q * jnp.exp(b - bmid), k * jnp.exp(bmid - b))
    att = jnp.where(incl, att, 0.0)
    st = st_ref[...]
    o = _dot(att, v) + _dot_nt(q * jnp.exp(b), st)
    st_new = st * jnp.exp(blast) + _dot_tn(v, k * jnp.exp(blast - b))
    st_ref[...] = st_new

    var = jnp.mean(o * o, axis=-1, keepdims=True)
    y = o * lax.rsqrt(var + EPS) * nrm_ref[...]
    y = y[:cin] * _silu(r_ref[...])
    o_ref[0] = y

    @pl.when(c == pl.num_programs(2) - 1)
    def _():
        so_ref[0, 0] = st_new.T


def gla_mixer(pm, psm, wg_pad, b_gate, nrm, s0, *, row0, nb, t):
    cin = min(t, CHUNK)
    nc = t // cin
    rb0 = row0 // cin
    has_state = s0 is not None
    if s0 is None:
        s0 = jnp.zeros((1, 1, GLA_DK, GLA_DV), F32)
        s0_map = lambda b, h, c: (0, 0, 0, 0)
    else:
        s0_map = lambda b, h, c: (b, h, 0, 0)

    def rmap(col0, width):
        return lambda b, h, c: (rb0 + b * nc + c, col0 // width + h)

    body = functools.partial(_gla_body, cin=cin, has_state=has_state)
    return pl.pallas_call(
        body,
        grid=(nb, GLA_HEADS, nc),
        in_specs=[pl.BlockSpec((cin, GLA_DK), rmap(EV_Q, GLA_DK)),
                  pl.BlockSpec((cin, GLA_DK), rmap(EV_K, GLA_DK)),
                  pl.BlockSpec((cin, GLA_DV), rmap(EV_V, GLA_DV)),
                  pl.BlockSpec((cin, GLA_DV), rmap(EV_R, GLA_DV)),
                  pl.BlockSpec((cin, LANES), lambda b, h, c: (rb0 + b * nc + c, 0)),
                  pl.BlockSpec((LANES, GLA_DK), lambda b, h, c: (0, h)),
                  pl.BlockSpec((1, GLA_DK), lambda b, h, c: (0, h)),
                  pl.BlockSpec((1, GLA_DV), lambda b, h, c: (0, 0)),
                  pl.BlockSpec((1, 1, GLA_DK, GLA_DV), s0_map)],
        out_specs=[pl.BlockSpec((1, cin, GLA_DV), lambda b, h, c: (b, c, h)),
                   pl.BlockSpec((1, 1, GLA_DK, GLA_DV), lambda b, h, c: (b, h, 0, 0))],
        out_shape=[jax.ShapeDtypeStruct((nb, t, GLA_VW), F32),
                   jax.ShapeDtypeStruct((nb, GLA_HEADS, GLA_DK, GLA_DV), F32)],
        scratch_shapes=[pltpu.VMEM((GLA_DV, GLA_DK), F32)],
        compiler_params=_cparams("arbitrary", "arbitrary", "arbitrary"),
        name="gla_mixer",
    )(pm, pm, pm, pm, psm, wg_pad, b_gate.reshape(1, GLA_QK), nrm.reshape(1, GLA_DV), s0)


def _ssd_body(z_ref, x_ref, b_ref, c_ref, sm_ref, cwx_ref, cwb_ref, cwc_ref, cbx_ref, cbb_ref, cbc_ref,
              dtb_ref, alog_ref, dexp_ref, nrm_ref, e_ref, s0_ref, c0x_ref, c0b_ref, c0c_ref,
              y_ref, so_ref, cox_ref, cob_ref, coc_ref,
              st_ref, ex_ref, eb_ref, ec_ref, *, cin, has_state):
    c = pl.program_id(1)

    @pl.when(c == 0)
    def _():
        if has_state:
            st_ref[...] = s0_ref[0].T
        else:
            st_ref[...] = jnp.zeros(st_ref.shape, F32)
        _conv_init(ex_ref, c0x_ref, has_state)
        _conv_init(eb_ref, c0b_ref, has_state)
        _conv_init(ec_ref, c0c_ref, has_state)

    xs = _silu(_conv_chunk(ex_ref, x_ref[...], cwx_ref, cin) + cbx_ref[...])
    bm = _silu(_conv_chunk(eb_ref, b_ref[...], cwb_ref, cin) + cbb_ref[...])
    cm = _silu(_conv_chunk(ec_ref, c_ref[...], cwc_ref, cin) + cbc_ref[...])

    dt = jax.nn.softplus(_pad_rows(sm_ref[...], CHUNK) + dtb_ref[...])
    if cin < CHUNK:
        rows = lax.broadcasted_iota(jnp.int32, dt.shape, 0)
        dt = jnp.where(rows < cin, dt, 0.0)
    a = dt * (-jnp.exp(alog_ref[...]))
    cum = _cumsum_rows(a)
    cum_t = _rows_of(cum, SSD_HEADS)
    last = cum[CHUNK - 1:CHUNK, :]
    stacked = jnp.concatenate([dt, jnp.exp(cum), jnp.exp(last - cum)], axis=0)
    ex = _dot_exact_rhs(stacked, e_ref[...])
    dt_e, ecum_e, w_e = ex[:CHUNK], ex[CHUNK:2 * CHUNK], ex[2 * CHUNK:]
    xdt = xs * dt_e
    xw = xdt * w_e
    incl, _, _ = _tri_masks(CHUNK)
    lane = lax.broadcasted_iota(jnp.int32, (CHUNK, LANES), 1)
    st = st_ref[...]
    gw = SSD_HPG * SSD_P
    ys = []
    for g in range(SSD_GROUPS):
        bg = bm[:, g * SSD_STATE:(g + 1) * SSD_STATE]
        cg = cm[:, g * SSD_STATE:(g + 1) * SSD_STATE]
        cb = _dot_nt(cg, bg)
        for pr in range(SSD_HPG // 2):
            h0 = g * SSD_HPG + 2 * pr
            atts = []
            for h in (h0, h0 + 1):
                seg = cum[:, h:h + 1] - cum_t[h:h + 1, :]
                atts.append(cb * jnp.exp(jnp.where(incl, seg, NEG_BIG)))
            xp = xdt[:, h0 * SSD_P:(h0 + 2) * SSD_P]
            yy = _dot(jnp.concatenate(atts, axis=0), xp)
            ys.append(jnp.where(lane < SSD_P, yy[:CHUNK], yy[CHUNK:]))
        sg = st[:, g * gw:(g + 1) * gw]
        ys_inter = _dot(cg, sg) * ecum_e[:, g * gw:(g + 1) * gw]
        for pr in range(SSD_HPG // 2):
            idx = g * (SSD_HPG // 2) + pr
            ys[idx] = ys[idx] + ys_inter[:, pr * LANES:(pr + 1) * LANES]
        st_ref[:, g * gw:(g + 1) * gw] = sg * ecum_e[CHUNK - 1:CHUNK, g * gw:(g + 1) * gw] + _dot_tn(
            bg, xw[:, g * gw:(g + 1) * gw])
    y = jnp.concatenate(ys, axis=1) + xs * dexp_ref[...]
    y = y[:cin] * _silu(z_ref[...])
    var = jnp.mean(y * y, axis=-1, keepdims=True)
    y_ref[0] = y * lax.rsqrt(var + EPS) * nrm_ref[...]

    @pl.when(c == pl.num_programs(1) - 1)
    def _():
        so_ref[0] = st_ref[...].T
        cox_ref[0] = _conv_tail(ex_ref, cin)
        cob_ref[0] = _conv_tail(eb_ref, cin)
        coc_ref[0] = _conv_tail(ec_ref, cin)

    if cin == CHUNK:
        _conv_advance(ex_ref)
        _conv_advance(eb_ref)
        _conv_advance(ec_ref)


def ssd_mixer(pm, psm, conv_w, conv_b, dtb_pad, alog_pad, d_exp, nrm, expand, s0, c0, *, row0, nb, t):
    cin = min(t, CHUNK)
    nc = t // cin
    rb0 = row0 // cin
    has_state = s0 is not None
    hp = SSD_INNER
    if s0 is None:
        s0 = jnp.zeros((1, hp, SSD_STATE), F32)
        c0 = jnp.zeros((1, CONV_W - 1, SSD_CONV_DIM), F32)
        bmap = lambda b, c: 0
    else:
        s0 = s0.reshape(nb, hp, SSD_STATE)
        bmap = lambda b, c: b

    def rmap(col0, width):
        return lambda b, c: (rb0 + b * nc + c, col0 // width)

    const = lambda b, c: (0, 0)
    kconv = CONV_W - 1
    body = functools.partial(_ssd_body, cin=cin, has_state=has_state)
    outs = pl.pallas_call(
        body,
        grid=(nb, nc),
        in_specs=[pl.BlockSpec((cin, hp), rmap(EV_Z, hp)),
                  pl.BlockSpec((cin, hp), rmap(EV_X, hp)),
                  pl.BlockSpec((cin, SSD_BC), rmap(EV_B, SSD_BC)),
                  pl.BlockSpec((cin, SSD_BC), rmap(EV_C, SSD_BC)),
                  pl.BlockSpec((cin, LANES), lambda b, c: (rb0 + b * nc + c, 1)),
                  pl.BlockSpec((CONV_W, hp), lambda b, c: (0, 0)),
                  pl.BlockSpec((CONV_W, SSD_BC), lambda b, c: (0, hp // SSD_BC)),
                  pl.BlockSpec((CONV_W, SSD_BC), lambda b, c: (0, hp // SSD_BC + 1)),
                  pl.BlockSpec((1, hp), lambda b, c: (0, 0)),
                  pl.BlockSpec((1, SSD_BC), lambda b, c: (0, hp // SSD_BC)),
                  pl.BlockSpec((1, SSD_BC), lambda b, c: (0, hp // SSD_BC + 1)),
                  pl.BlockSpec((1, LANES), const),
                  pl.BlockSpec((1, LANES), const),
                  pl.BlockSpec((1, hp), const),
                  pl.BlockSpec((1, hp), const),
                  pl.BlockSpec((LANES, hp), const),
                  pl.BlockSpec((1, hp, SSD_STATE), lambda b, c: (bmap(b, c), 0, 0)),
                  pl.BlockSpec((1, kconv, hp), lambda b, c: (bmap(b, c), 0, 0)),
                  pl.BlockSpec((1, kconv, SSD_BC), lambda b, c: (bmap(b, c), 0, hp // SSD_BC)),
                  pl.BlockSpec((1, kconv, SSD_BC), lambda b, c: (bmap(b, c), 0, hp // SSD_BC + 1))],
        out_specs=[pl.BlockSpec((1, cin, hp), lambda b, c: (b, c, 0)),
                   pl.BlockSpec((1, hp, SSD_STATE), lambda b, c: (b, 0, 0)),
                   pl.BlockSpec((1, kconv, hp), lambda b, c: (b, 0, 0)),
                   pl.BlockSpec((1, kconv, SSD_BC), lambda b, c: (b, 0, 0)),
                   pl.BlockSpec((1, kconv, SSD_BC), lambda b, c: (b, 0, 0))],
        out_shape=[jax.ShapeDtypeStruct((nb, t, hp), F32),
                   jax.ShapeDtypeStruct((nb, hp, SSD_STATE), F32),
                   jax.ShapeDtypeStruct((nb, kconv, hp), F32),
                   jax.ShapeDtypeStruct((nb, kconv, SSD_BC), F32),
                   jax.ShapeDtypeStruct((nb, kconv, SSD_BC), F32)],
        scratch_shapes=[pltpu.VMEM((SSD_STATE, hp), F32),
                        pltpu.VMEM((CHUNK + SUBLANES, hp), F32),
                        pltpu.VMEM((CHUNK + SUBLANES, SSD_BC), F32),
                        pltpu.VMEM((CHUNK + SUBLANES, SSD_BC), F32)],
        compiler_params=_cparams("arbitrary", "arbitrary"),
        name="ssd_mixer",
    )(pm, pm, pm, pm, psm, conv_w, conv_w, conv_w, conv_b, conv_b, conv_b, dtb_pad, alog_pad, d_exp,
      nrm.reshape(1, hp), expand, s0, c0, c0, c0)
    y, s_new, cx, cb_, cc = outs
    conv_s = jnp.concatenate([cx, cb_, cc], axis=-1)
    return y, s_new.reshape(nb, SSD_HEADS, SSD_P, SSD_STATE), conv_s


def _l2norm_heads(x, n_heads, width, scale):
    outs = []
    for e in range(n_heads):
        xe = x[:, e * width:(e + 1) * width]
        ss = jnp.sum(xe * xe, axis=-1, keepdims=True)
        outs.append(xe * (lax.rsqrt(ss + EPS) * scale))
    return outs


def _unit_lower_inverse(a_mat, eye):
    nm = -a_mat
    t_inv = eye + nm
    p = nm
    steps = int(math.log2(CHUNK)) - 1
    for _ in range(steps):
        p = _dot(p, p)
        t_inv = t_inv + _dot(t_inv, p)
    ah, al = _split2(a_mat)
    th, tl = _split2(t_inv)
    at = (jnp.dot(ah, th, preferred_element_type=F32) + jnp.dot(ah, tl, preferred_element_type=F32)
          + jnp.dot(al, th, preferred_element_type=F32))
    resid = eye - t_inv - at
    return t_inv + _dot(t_inv, resid)


def _gdn_body(q_ref, k_ref, v_ref, z_ref, sm_ref, cwq_ref, cwk_ref, cwv_ref, alog_ref, dtb_ref, nrm_ref,
              s0_ref, c0q_ref, c0k_ref, c0v_ref,
              o_ref, so_ref, coq_ref, cok_ref, cov_ref,
              s_ref, eq_ref, ek_ref, ev_ref, *, cin, has_state):
    c = pl.program_id(2)
    nqk = GDN_HB // (GDN_V_HEADS // GDN_QK_HEADS)

    @pl.when(c == 0)
    def _():
        if has_state:
            s_ref[...] = s0_ref[0]
        else:
            s_ref[...] = jnp.zeros(s_ref.shape, F32)
        _conv_init(eq_ref, c0q_ref, has_state)
        _conv_init(ek_ref, c0k_ref, has_state)
        _conv_init(ev_ref, c0v_ref, has_state)

    qc = _silu(_conv_chunk(eq_ref, q_ref[...], cwq_ref, cin))
    kc = _silu(_conv_chunk(ek_ref, k_ref[...], cwk_ref, cin))
    vc = _silu(_conv_chunk(ev_ref, v_ref[...], cwv_ref, cin))
    qs = _l2norm_heads(qc, nqk, GDN_DK, GDN_DK ** -0.5)
    ks = _l2norm_heads(kc, nqk, GDN_DK, 1.0)

    sm = _pad_rows(sm_ref[...], CHUNK)
    beta = jax.nn.sigmoid(sm)
    gl = -jnp.exp(alog_ref[...]) * jax.nn.softplus(sm + dtb_ref[...])
    if cin < CHUNK:
        rows = lax.broadcasted_iota(jnp.int32, sm.shape, 0)
        beta = jnp.where(rows < cin, beta, 0.0)
        gl = jnp.where(rows < cin, gl, 0.0)
    cum = _cumsum_rows(gl)
    cum_t = _rows_of(cum, 2 * GDN_HB)
    incl, strict, diag = _tri_masks(CHUNK)
    eye = jnp.where(diag, 1.0, 0.0)
    rep = GDN_V_HEADS // GDN_QK_HEADS
    outs = []
    for e in range(nqk):
        qe, ke = qs[e], ks[e]
        kk = _dot_nt(ke, ke)
        qk = _dot_nt(qe, ke)
        for i in range(e * rep, (e + 1) * rep):
            col = cum[:, GDN_HB + i:GDN_HB + i + 1]
            row = cum_t[GDN_HB + i:GDN_HB + i + 1, :]
            bcol = beta[:, i:i + 1]
            dec = jnp.exp(jnp.where(incl, col - row, NEG_BIG))
            a_mat = jnp.where(strict, bcol * kk * dec, 0.0)
            t_inv = _unit_lower_inverse(a_mat, eye)
            ecol = jnp.exp(col)
            vi = vc[:, i * GDN_DV:(i + 1) * GDN_DV]
            rhs = jnp.concatenate([vi * bcol, ke * (bcol * ecol)], axis=1)
            sol = _dot(t_inv, rhs)
            s_i = s_ref[i]
            delta = sol[:, :GDN_DV] - _dot(sol[:, GDN_DV:], s_i)
            o_i = _dot(qe * ecol, s_i) + _dot(qk * dec, delta)
            lastv = col[CHUNK - 1:CHUNK, :]
            s_ref[i] = s_i * jnp.exp(lastv) + _dot_tn(ke * jnp.exp(lastv - col), delta)
            var = jnp.mean(o_i * o_i, axis=-1, keepdims=True)
            outs.append(o_i * lax.rsqrt(var + EPS) * nrm_ref[...])
    o = jnp.concatenate(outs, axis=1)
    o_ref[0] = o[:cin] * _silu(z_ref[...])

    @pl.when(c == pl.num_programs(2) - 1)
    def _():
        so_ref[0] = s_ref[...]
        coq_ref[0] = _conv_tail(eq_ref, cin)
        cok_ref[0] = _conv_tail(ek_ref, cin)
        cov_ref[0] = _conv_tail(ev_ref, cin)

    if cin == CHUNK:
        _conv_advance(eq_ref)
        _conv_advance(ek_ref)
        _conv_advance(ev_ref)


def gdn_mixer(pm, psm, conv_w, alog_blk, dtb_blk, nrm, s0, c0, *, row0, nb, t):
    cin = min(t, CHUNK)
    nc = t // cin
    rb0 = row0 // cin
    has_state = s0 is not None
    nhb = GDN_V_HEADS // GDN_HB
    wq = GDN_QK // nhb
    wv = GDN_VW // nhb
    kconv = CONV_W - 1
    if s0 is None:
        s0 = jnp.zeros((1, GDN_HB, GDN_DK, GDN_DV), F32)
        c0 = jnp.zeros((1, kconv, GDN_CONV_DIM), F32)
        bmap = lambda b: 0
    else:
        bmap = lambda b: b

    def rmap(col0, width):
        return lambda b, hb, c: (rb0 + b * nc + c, col0 // width + hb)

    body = functools.partial(_gdn_body, cin=cin, has_state=has_state)
    outs = pl.pallas_call(
        body,
        grid=(nb, nhb, nc),
        in_specs=[pl.BlockSpec((cin, wq), rmap(OD_Q, wq)),
                  pl.BlockSpec((cin, wq), rmap(OD_K, wq)),
                  pl.BlockSpec((cin, wv), rmap(OD_V, wv)),
                  pl.BlockSpec((cin, wv), rmap(OD_Z, wv)),
                  pl.BlockSpec((cin, LANES), lambda b, hb, c: (rb0 + b * nc + c, hb)),
                  pl.BlockSpec((CONV_W, wq), lambda b, hb, c: (0, hb)),
                  pl.BlockSpec((CONV_W, wq), lambda b, hb, c: (0, GDN_QK // wq + hb)),
                  pl.BlockSpec((CONV_W, wv), lambda b, hb, c: (0, 2 * GDN_QK // wv + hb)),
                  pl.BlockSpec((None, 1, LANES), lambda b, hb, c: (hb, 0, 0)),
                  pl.BlockSpec((None, 1, LANES), lambda b, hb, c: (hb, 0, 0)),
                  pl.BlockSpec((1, GDN_DV), lambda b, hb, c: (0, 0)),
                  pl.BlockSpec((1, GDN_HB, GDN_DK, GDN_DV), lambda b, hb, c: (bmap(b), hb if has_state else 0, 0, 0)),
                  pl.BlockSpec((1, kconv, wq), lambda b, hb, c: (bmap(b), 0, hb)),
                  pl.BlockSpec((1, kconv, wq), lambda b, hb, c: (bmap(b), 0, GDN_QK // wq + hb)),
                  pl.BlockSpec((1, kconv, wv), lambda b, hb, c: (bmap(b), 0, 2 * GDN_QK // wv + hb))],
        out_specs=[pl.BlockSpec((1, cin, wv), lambda b, hb, c: (b, c, hb)),
                   pl.BlockSpec((1, GDN_HB, GDN_DK, GDN_DV), lambda b, hb, c: (b, hb, 0, 0)),
                   pl.BlockSpec((1, kconv, wq), lambda b, hb, c: (b, 0, hb)),
                   pl.BlockSpec((1, kconv, wq), lambda b, hb, c: (b, 0, hb)),
                   pl.BlockSpec((1, kconv, wv), lambda b, hb, c: (b, 0, hb))],
        out_shape=[jax.ShapeDtypeStruct((nb, t, GDN_VW), F32),
                   jax.ShapeDtypeStruct((nb, GDN_V_HEADS, GDN_DK, GDN_DV), F32),
                   jax.ShapeDtypeStruct((nb, kconv, GDN_QK), F32),
                   jax.ShapeDtypeStruct((nb, kconv, GDN_QK), F32),
                   jax.ShapeDtypeStruct((nb, kconv, GDN_VW), F32)],
        scratch_shapes=[pltpu.VMEM((GDN_HB, GDN_DK, GDN_DV), F32),
                        pltpu.VMEM((CHUNK + SUBLANES, wq), F32),
                        pltpu.VMEM((CHUNK + SUBLANES, wq), F32),
                        pltpu.VMEM((CHUNK + SUBLANES, wv), F32)],
        compiler_params=_cparams("arbitrary", "arbitrary", "arbitrary"),
        name="gdn_mixer",
    )(pm, pm, pm, pm, psm, conv_w, conv_w, conv_w, alog_blk, dtb_blk, nrm.reshape(1, GDN_DV),
      s0, c0, c0, c0)
    o, s_new, cq, ck, cv = outs
    return o, s_new, jnp.concatenate([cq, ck, cv], axis=-1)


def _rope_body(q_ref, k_ref, inv_ref, qo_ref, ko_ref, *, n_prompt_tiles, tiles_per_seq, ts):
    i = pl.program_id(0)
    rt = q_ref.shape[0]
    r = lax.broadcasted_iota(jnp.int32, (rt, LANES), 0)
    pos_prompt = (i % tiles_per_seq) * rt + r
    pos_sample = PAST_LEN + r % ts
    pos = jnp.where(i < n_prompt_tiles, pos_prompt, pos_sample).astype(F32)
    ang = pos * inv_ref[...]
    cos = jnp.cos(ang)
    sin = jnp.sin(ang)
    lane = lax.broadcasted_iota(jnp.int32, (rt, LANES), 1)
    sin = jnp.where(lane < LANES // 2, -sin, sin)
    for src, dst in ((q_ref, qo_ref), (k_ref, ko_ref)):
        for h in range(DSA_W // DSA_HD):
            x = src[:, h * DSA_HD:(h + 1) * DSA_HD]
            dst[:, h * DSA_HD:(h + 1) * DSA_HD] = x * cos + pltpu.roll(x, DSA_HD // 2, 1) * sin


def rope_qk(pm, geom):
    bp, tp, bs, ts = geom
    mrows = pm.shape[0]
    half = DSA_HD // 2
    inv = ROPE_THETA ** (-jnp.arange(half, dtype=F32) / half)
    inv2 = jnp.concatenate([inv, inv]).reshape(1, DSA_HD)
    body = functools.partial(_rope_body, n_prompt_tiles=bp * tp // ROW_TILE, tiles_per_seq=tp // ROW_TILE, ts=ts)
    spec_o = pl.BlockSpec((ROW_TILE, DSA_W), lambda i: (i, 0))
    return pl.pallas_call(
        body,
        grid=(mrows // ROW_TILE,),
        in_specs=[pl.BlockSpec((ROW_TILE, DSA_W), lambda i: (i, OD_DQ // DSA_W)),
                  pl.BlockSpec((ROW_TILE, DSA_W), lambda i: (i, OD_DK // DSA_W)),
                  pl.BlockSpec((1, DSA_HD), lambda i: (0, 0))],
        out_specs=[spec_o, spec_o],
        out_shape=[jax.ShapeDtypeStruct((mrows, DSA_W), F32)] * 2,
        compiler_params=_cparams("arbitrary"),
        name="rope_qk",
    )(pm, pm, inv2)


def _dsa_prompt_body(q_ref, kp_ref, kc_ref, vp_ref, vc_ref, o_ref, l_ref, *, jmax):
    n = pl.program_id(2)
    blk = DSA_BLOCK
    i_idx = lax.broadcasted_iota(jnp.int32, (blk, 2 * blk), 0)
    m_idx = lax.broadcasted_iota(jnp.int32, (blk, 2 * blk), 1)
    dist = i_idx + blk - m_idx
    valid = (dist >= 0) & (dist <= jmax) & ((m_idx >= blk) | (n > 0))
    scale = DSA_HD ** -0.5
    for h in range(DSA_HEADS):
        sl = slice(h * DSA_HD, (h + 1) * DSA_HD)
        kk = jnp.concatenate([kp_ref[:, sl], kc_ref[:, sl]], axis=0)
        vv = jnp.concatenate([vp_ref[:, sl], vc_ref[:, sl]], axis=0)
        s = _dot_nt(q_ref[:, sl], kk) * scale
        s = jnp.where(valid, s, NEG_BIG)
        mx = jnp.max(s, axis=-1, keepdims=True)
        p = jnp.exp(s - mx)
        den = jnp.sum(p, axis=-1, keepdims=True)
        o_ref[:, sl] = _dot(p, vv) / den
        l_ref[:, sl] = jnp.broadcast_to(mx + jnp.log(den), (blk, DSA_HD))


def dsa_prompt(rq, rk, pm, g, geom):
    bp, tp, _, _ = geom
    window, dil = DSA_GROUPS[g]
    ls = tp // dil
    nblk = ls // DSA_BLOCK
    mrows = rq.shape[0]
    rq2 = rq.reshape(mrows // dil, dil * DSA_W)
    rk2 = rk.reshape(mrows // dil, dil * DSA_W)
    pm2 = pm.reshape(mrows // dil, dil * OD_MAIN)
    qk_cb = DSA_W // DSA_GW
    v_cb = OD_MAIN // DSA_GW
    v_off = OD_DV // DSA_GW

    def cur(cb, off):
        return lambda b, r, n: (b * nblk + n, r * cb + off + g)

    def prev(cb, off):
        return lambda b, r, n: (b * nblk + jnp.maximum(n - 1, 0), r * cb + off + g)

    blk = (DSA_BLOCK, DSA_GW)
    o, lse = pl.pallas_call(
        functools.partial(_dsa_prompt_body, jmax=window // dil),
        grid=(bp, dil, nblk),
        in_specs=[pl.BlockSpec(blk, cur(qk_cb, 0)),
                  pl.BlockSpec(blk, prev(qk_cb, 0)),
                  pl.BlockSpec(blk, cur(qk_cb, 0)),
                  pl.BlockSpec(blk, prev(v_cb, v_off)),
                  pl.BlockSpec(blk, cur(v_cb, v_off))],
        out_specs=[pl.BlockSpec(blk, lambda b, r, n: (b * nblk + n, r)),
                   pl.BlockSpec(blk, lambda b, r, n: (b * nblk + n, r))],
        out_shape=[jax.ShapeDtypeStruct((bp * ls, dil * DSA_GW), F32)] * 2,
        compiler_params=_cparams("arbitrary", "arbitrary", "arbitrary"),
        name=f"dsa_prompt_w{window}",
    )(rq2, rk2, rk2, pm2, pm2)
    return o.reshape(bp * tp, DSA_GW), lse.reshape(bp * tp, DSA_GW)


def _dsa_sample_body(q_ref, kn_ref, vn_ref, cache_ref, o_ref, l_ref, m_scr, d_scr, acc_scr, *, window, dil, lb):
    j = pl.program_id(1)
    nj = pl.num_programs(1)
    ts = q_ref.shape[0]
    scale = DSA_HD ** -0.5

    @pl.when(j == 0)
    def _():
        m_scr[...] = jnp.full(m_scr.shape, NEG_BIG, F32)
        d_scr[...] = jnp.zeros(d_scr.shape, F32)
        acc_scr[...] = jnp.zeros(acc_scr.shape, F32)

    def update(h, s, vals, ok):
        sl = slice(h * DSA_HD, (h + 1) * DSA_HD)
        s = jnp.where(ok, s, NEG_BIG)
        m_old = m_scr[:, sl]
        m_new = jnp.maximum(m_old, jnp.max(s, axis=-1, keepdims=True))
        alpha = jnp.exp(m_old - m_new)
        p = jnp.where(ok, jnp.exp(s - m_new[:, 0:1]), 0.0)
        d_scr[:, sl] = alpha * d_scr[:, sl] + jnp.sum(p, axis=-1, keepdims=True)
        acc_scr[:, sl] = alpha * acc_scr[:, sl] + _dot(p, vals)
        m_scr[:, sl] = m_new

    srow = lax.broadcasted_iota(jnp.int32, (ts, lb), 0)
    col = lax.broadcasted_iota(jnp.int32, (ts, lb), 1)
    dist = window + srow - (j * lb + col)
    valid = ((dist & (dil - 1)) == 0) & (dist <= window)
    for h in range(DSA_HEADS):
        sl = slice(h * DSA_HD, (h + 1) * DSA_HD)
        kb = cache_ref[0, :, sl]
        vb = cache_ref[0, :, DSA_GW + h * DSA_HD:DSA_GW + (h + 1) * DSA_HD]
        update(h, _dot_nt(q_ref[:, sl], kb) * scale, vb, valid)

    @pl.when(j == nj - 1)
    def _():
        r2 = lax.broadcasted_iota(jnp.int32, (ts, ts), 0)
        c2 = lax.broadcasted_iota(jnp.int32, (ts, ts), 1)
        d2 = r2 - c2
        valid2 = (d2 >= 0) & ((d2 & (dil - 1)) == 0)
        for h in range(DSA_HEADS):
            sl = slice(h * DSA_HD, (h + 1) * DSA_HD)
            update(h, _dot_nt(q_ref[:, sl], kn_ref[:, sl]) * scale, vn_ref[:, sl], valid2)
        o_ref[...] = acc_scr[...] / d_scr[...]
        l_ref[...] = m_scr[...] + jnp.log(d_scr[...])


def dsa_sample(rq, rk, pm, cache, g, geom):
    bp, tp, bs, ts = geom
    window, dil = DSA_GROUPS[g]
    lcache = cache.shape[1]
    assert lcache == window, "cache must hold exactly the window"
    lb = min(lcache, 512)
    rb0 = bp * tp // ts
    cache2 = cache.reshape(bs, lcache, 2 * DSA_GW)
    blk = (ts, DSA_GW)
    body = functools.partial(_dsa_sample_body, window=window, dil=dil, lb=lb)
    return pl.pallas_call(
        body,
        grid=(bs, lcache // lb),
        in_specs=[pl.BlockSpec(blk, lambda b, j: (rb0 + b, g)),
                  pl.BlockSpec(blk, lambda b, j: (rb0 + b, g)),
                  pl.BlockSpec(blk, lambda b, j: (rb0 + b, OD_DV // DSA_GW + g)),
                  pl.BlockSpec((1, lb, 2 * DSA_GW), lambda b, j: (b, j, 0))],
        out_specs=[pl.BlockSpec(blk, lambda b, j: (b, 0)), pl.BlockSpec(blk, lambda b, j: (b, 0))],
        out_shape=[jax.ShapeDtypeStruct((bs * ts, DSA_GW), F32)] * 2,
        scratch_shapes=[pltpu.VMEM(blk, F32)] * 3,
        compiler_params=_cparams("arbitrary", "arbitrary"),
        name=f"dsa_sample_w{window}",
    )(rq, rk, pm, cache2)


def _dsa_merge_body(o0, o1, o2, l0, l1, l2, out_ref):
    a, b, c = l0[...], l1[...], l2[...]
    mx = jnp.maximum(jnp.maximum(a, b), c)
    wa, wb, wc = jnp.exp(a - mx), jnp.exp(b - mx), jnp.exp(c - mx)
    out_ref[...] = (wa * o0[...] + wb * o1[...] + wc * o2[...]) / (wa + wb + wc)


def dsa_merge(os_, ls_):
    mrows, w = os_[0].shape
    spec = pl.BlockSpec((ROW_TILE, w), lambda i: (i, 0))
    return pl.pallas_call(
        _dsa_merge_body,
        grid=(mrows // ROW_TILE,),
        in_specs=[spec] * 6,
        out_specs=spec,
        out_shape=jax.ShapeDtypeStruct((mrows, w), F32),
        compiler_params=_cparams("arbitrary"),
        name="dsa_merge",
    )(*os_, *ls_)


def _even_weights(w):
    o = np.cumsum([0, GLA_QK, GLA_QK, GLA_VW, GLA_VW, GLA_RANK, SSD_INNER, SSD_CONV_DIM, SSD_HEADS])
    q, k, v, r, glr, z, xbc, dt = (w[:, o[i]:o[i + 1]] for i in range(8))
    xs, bm, cm = xbc[:, :SSD_INNER], xbc[:, SSD_INNER:SSD_INNER + SSD_BC], xbc[:, SSD_INNER + SSD_BC:]
    main = jnp.concatenate([z, xs, v, r, q, k, bm, cm], axis=1)
    d = w.shape[0]
    small = jnp.concatenate([glr, jnp.zeros((d, LANES - GLA_RANK), F32), dt,
                             jnp.zeros((d, LANES - SSD_HEADS), F32)], axis=1)
    return main, small


def _odd_weights(w):
    o = np.cumsum([0, GDN_QK, GDN_QK, GDN_VW, GDN_VW, GDN_V_HEADS, GDN_V_HEADS, DSA_W, DSA_W, DSA_W])
    q, k, v, z, b, a, dq, dk, dv = (w[:, o[i]:o[i + 1]] for i in range(9))
    main = jnp.concatenate([v, z, q, k, dq, dk, dv], axis=1)
    d = w.shape[0]
    parts = []
    for hb in range(GDN_V_HEADS // GDN_HB):
        parts += [b[:, hb * GDN_HB:(hb + 1) * GDN_HB], a[:, hb * GDN_HB:(hb + 1) * GDN_HB],
                  jnp.zeros((d, LANES - 2 * GDN_HB), F32)]
    return main, jnp.concatenate(parts, axis=1)


def _gdn_head_blocks(vec):
    nhb = GDN_V_HEADS // GDN_HB
    v = vec.reshape(nhb, GDN_HB)
    out = jnp.zeros((nhb, LANES), F32).at[:, GDN_HB:2 * GDN_HB].set(v)
    return out.reshape(nhb, 1, LANES)


def _pad_lanes(vec, n=LANES):
    return jnp.zeros((1, n), F32).at[0, :vec.shape[0]].set(vec)


def kernel(x_prompt, x_sample, c_prompt, c_sample, state_gla, state_ssd, state_ssd_conv, state_gdn, state_gdn_conv, cache_dsa_w128, cache_dsa_w512, cache_dsa_w2048, ln_ffn1, ln_mix, ln_ffn2, w_ada, b_ada, ffn1_w_in, ffn1_w_out, ffn2_w_in, ffn2_w_out, even_w_in, gla_w_gate2, gla_b_gate, gla_norm, ssd_conv_w, ssd_conv_b, ssd_dt_bias, ssd_A_log, ssd_D, ssd_norm, even_w_out, odd_w_in, gdn_conv_w, gdn_A_log, gdn_dt_bias, gdn_norm, odd_w_out, final_norm):
    bp, tp, d = x_prompt.shape
    bs, ts, _ = x_sample.shape
    depth = w_ada.shape[0]
    geom = (bp, tp, bs, ts)
    assert tp % ROW_TILE == 0 and bs * ts == ROW_TILE and ts == SUBLANES
    mp = bp * tp
    caches = (cache_dsa_w128, cache_dsa_w512, cache_dsa_w2048)

    x = jnp.concatenate([x_prompt.reshape(mp, d), x_sample.reshape(bs * ts, d)], axis=0)
    n_pad = (-(bs + bp)) % SUBLANES
    c_all = jnp.concatenate([c_sample, c_prompt, jnp.zeros((n_pad, d), F32)], axis=0)
    mod4 = ada_mod(c_all, w_ada, b_ada).reshape(depth, c_all.shape[0], N_MOD, d)
    expand = jnp.asarray((np.arange(LANES)[:, None] == (np.arange(SSD_INNER)[None, :] // SSD_P)).astype(np.float32))

    new = {k: ([], []) for k in ("gla", "ssd", "ssd_conv", "gdn", "gdn_conv", "dsa0", "dsa1", "dsa2")}

    h = rows_update(x, None, mod4, ln_ffn1[0], geom, mod_layer=0, shift_idx=0, scale_idx=1, out_h=True)[0]
    for l in range(depth):
        act = matmul_swiglu(h, ffn1_w_in, l)
        m = matmul(act, ffn1_w_out, layer=l, tm_cands=(384, 256), tn=512, name="ffn_out")
        x, h = rows_update(x, m, mod4, ln_mix[l], geom, gate_layer=l, mod_layer=l, coef=0.5, gate_idx=2,
                           shift_idx=3, scale_idx=4, out_x=True, out_h=True)
        i = l // 2
        if l % 2 == 0:
            w_main, w_small = _even_weights(even_w_in[i])
            pm = matmul(h, w_main, tn=1024, name="even_in")
            psm = matmul(h, w_small, tn=256, name="even_in_small")
            wg_pad = jnp.zeros((LANES, GLA_QK), F32).at[:GLA_RANK].set(gla_w_gate2[i])
            dtb = _pad_lanes(ssd_dt_bias[i])
            alog = _pad_lanes(ssd_A_log[i])
            d_exp = jnp.repeat(ssd_D[i], SSD_P).reshape(1, SSD_INNER)
            ys = []
            for grp, (row0, nb, t, s_gla, s_ssd, s_conv) in enumerate((
                    (0, bp, tp, None, None, None),
                    (mp, bs, ts, state_gla[i], state_ssd[i], state_ssd_conv[i]))):
                o_gla, gla_new = gla_mixer(pm, psm, wg_pad, gla_b_gate[i], gla_norm[i], s_gla,
                                           row0=row0, nb=nb, t=t)
                y_ssd, ssd_new, conv_new = ssd_mixer(pm, psm, ssd_conv_w[i], ssd_conv_b[i].reshape(1, -1), dtb,
                                                     alog, d_exp, ssd_norm[i], expand, s_ssd, s_conv,
                                                     row0=row0, nb=nb, t=t)
                new["gla"][grp].append(gla_new)
                new["ssd"][grp].append(ssd_new)
                new["ssd_conv"][grp].append(conv_new)
                ys.append(jnp.concatenate([o_gla.reshape(nb * t, GLA_VW), y_ssd.reshape(nb * t, SSD_INNER)],
                                          axis=1))
            y_all = jnp.concatenate(ys, axis=0).astype(BF16)
            m = matmul(y_all, even_w_out[i], tn=512, name="even_out")
        else:
            w_main, w_small = _odd_weights(odd_w_in[i])
            pm = matmul(h, w_main, tn=1024, name="odd_in")
            psm = matmul(h, w_small, tn=LANES * (GDN_V_HEADS // GDN_HB), name="odd_in_small")
            rq, rk = rope_qk(pm, geom)
            alog_blk = _gdn_head_blocks(gdn_A_log[i])
            dtb_blk = _gdn_head_blocks(gdn_dt_bias[i])
            ys = []
            for grp, (row0, nb, t, s_gdn, s_conv) in enumerate((
                    (0, bp, tp, None, None), (mp, bs, ts, state_gdn[i], state_gdn_conv[i]))):
                o_gdn, gdn_new, conv_new = gdn_mixer(pm, psm, gdn_conv_w[i], alog_blk, dtb_blk, gdn_norm[i],
                                                     s_gdn, s_conv, row0=row0, nb=nb, t=t)
                new["gdn"][grp].append(gdn_new)
                new["gdn_conv"][grp].append(conv_new)
                os_, ls_ = [], []
                for g, (window, dil) in enumerate(DSA_GROUPS):
                    if grp == 0:
                        o_g, l_g = dsa_prompt(rq, rk, pm, g, geom)
                        keep = min(window, tp)
                        kg = rk[:mp].reshape(bp, tp, DSA_W)[:, tp - keep:, g * DSA_GW:(g + 1) * DSA_GW]
                        vg = pm[:mp].reshape(bp, tp, OD_MAIN)[:, tp - keep:, OD_DV + g * DSA_GW:OD_DV + (g + 1) * DSA_GW]
                    else:
                        o_g, l_g = dsa_sample(rq, rk, pm, caches[g][i], g, geom)
                        keep = ts
                        kg = rk[mp:].reshape(bs, ts, DSA_W)[:, :, g * DSA_GW:(g + 1) * DSA_GW]
                        vg = pm[mp:].reshape(bs, ts, OD_MAIN)[:, :, OD_DV + g * DSA_GW:OD_DV + (g + 1) * DSA_GW]
                    rows = jnp.stack([kg.reshape(nb, keep, DSA_HEADS, DSA_HD),
                                      vg.reshape(nb, keep, DSA_HEADS, DSA_HD)], axis=2)
                    new["dsa%d" % g][grp].append(rows)
                    os_.append(o_g)
                    ls_.append(l_g)
                o_dsa = dsa_merge(os_, ls_)
                ys.append(jnp.concatenate([o_gdn.reshape(nb * t, GDN_VW), o_dsa], axis=1))
            y_all = jnp.concatenate(ys, axis=0).astype(BF16)
            m = matmul(y_all, odd_w_out[i], tn=512, name="odd_out")
        x, h = rows_update(x, m, mod4, ln_ffn2[l], geom, gate_layer=l, mod_layer=l, coef=1.0, gate_idx=5,
                           shift_idx=6, scale_idx=7, out_x=True, out_h=True)
        act = matmul_swiglu(h, ffn2_w_in, l)
        m = matmul(act, ffn2_w_out, layer=l, tm_cands=(384, 256), tn=512, name="ffn_out")
        if l + 1 < depth:
            x, h = rows_update(x, m, mod4, ln_ffn1[l + 1], geom, gate_layer=l, mod_layer=l + 1, coef=0.5,
                               gate_idx=8, shift_idx=0, scale_idx=1, out_x=True, out_h=True)
        else:
            y = rows_update(x, m, mod4, final_norm, geom, gate_layer=l, coef=0.5, gate_idx=8, out_y=True)[0]

    y_prompt = y[:mp].reshape(bp, tp, d)
    y_sample = y[mp:].reshape(bs, ts, d)
    outs = [y_prompt, y_sample]
    for name in ("gla", "ssd", "ssd_conv", "gdn", "gdn_conv", "dsa0", "dsa1", "dsa2"):
        outs.append(jnp.stack(new[name][0]))
        outs.append(jnp.stack(new[name][1]))
    return tuple(outs)
```

```python
import functools
import math

import numpy as np
import jax
import jax.numpy as jnp
from jax import lax
from jax.experimental import pallas as pl
from jax.experimental.pallas import tpu as pltpu

F32 = jnp.float32
BF16 = jnp.bfloat16

D_MODEL = 2048
D_FF = 5632
N_MOD = 9
EPS = 1e-6
CONV_W = 4
ROPE_THETA = 10000.0
PAST_LEN = 8192

GLA_HEADS, GLA_DK, GLA_DV, GLA_RANK, GLA_TAU = 4, 128, 256, 16, 16.0
GLA_QK, GLA_VW = GLA_HEADS * GLA_DK, GLA_HEADS * GLA_DV
SSD_HEADS, SSD_P, SSD_STATE, SSD_GROUPS = 32, 64, 128, 4
SSD_HPG = SSD_HEADS // SSD_GROUPS
SSD_INNER = SSD_HEADS * SSD_P
SSD_BC = SSD_GROUPS * SSD_STATE
SSD_CONV_DIM = SSD_INNER + 2 * SSD_BC
GDN_QK_HEADS, GDN_V_HEADS, GDN_DK, GDN_DV = 8, 16, 128, 128
GDN_QK, GDN_VW = GDN_QK_HEADS * GDN_DK, GDN_V_HEADS * GDN_DV
GDN_CONV_DIM = 2 * GDN_QK + GDN_VW
GDN_REP = GDN_V_HEADS // GDN_QK_HEADS
DSA_GROUPS = ((128, 1), (512, 4), (2048, 16))
DSA_HEADS, DSA_HD, DSA_BLOCK = 8, 128, 128
DSA_GW = DSA_HEADS * DSA_HD
DSA_W = len(DSA_GROUPS) * DSA_GW

LANES = 128
SUBLANES = 8
MXU_DIM = 256
VMEM_LIMIT_BYTES = 56 * 1024 * 1024

ROW_TILE = 256
CHUNK = 64
GDN_PACK = MXU_DIM // CHUNK
NEG_BIG = -1e30

EV_Z, EV_X, EV_V, EV_R, EV_Q, EV_K, EV_B, EV_C = 0, 2048, 4096, 5120, 6144, 6656, 7168, 7680
EV_MAIN = 8192
OD_V, OD_Z, OD_Q, OD_K, OD_DQ, OD_DK, OD_DV = 0, 2048, 4096, 5120, 6144, 9216, 12288
OD_MAIN = 15360


def _cparams(*sem):
    return pltpu.CompilerParams(dimension_semantics=sem, vmem_limit_bytes=VMEM_LIMIT_BYTES)


def _dot(a, b):
    return jnp.dot(a.astype(BF16), b.astype(BF16), preferred_element_type=F32)


def _dot_nt(a, b):
    return lax.dot_general(a.astype(BF16), b.astype(BF16), (((1,), (1,)), ((), ())), preferred_element_type=F32)


def _dot_tn(a, b):
    return lax.dot_general(a.astype(BF16), b.astype(BF16), (((0,), (0,)), ((), ())), preferred_element_type=F32)


def _split2(x):
    hi = x.astype(BF16)
    lo = (x - hi.astype(F32)).astype(BF16)
    return hi, lo


def _split3(x):
    hi = x.astype(BF16)
    r1 = x - hi.astype(F32)
    mid = r1.astype(BF16)
    lo = (r1 - mid.astype(F32)).astype(BF16)
    return hi, mid, lo


def _dot_exact_rhs(x, e):
    hi, lo = _split2(x)
    eb = e.astype(BF16)
    return (jnp.dot(hi, eb, preferred_element_type=F32) + jnp.dot(lo, eb, preferred_element_type=F32))


def _dot_exact_lhs(e, x):
    eb = e.astype(BF16)
    return sum(jnp.dot(eb, p, preferred_element_type=F32) for p in _split3(x))


def _dot_nt_exact_lhs(e, x):
    eb = e.astype(BF16)
    nt = (((1,), (1,)), ((), ()))
    return sum(lax.dot_general(eb, p, nt, preferred_element_type=F32) for p in _split3(x))


def _cumsum_rows(x):
    n = x.shape[0]
    r = lax.broadcasted_iota(jnp.int32, (n, n), 0)
    c = lax.broadcasted_iota(jnp.int32, (n, n), 1)
    return _dot_exact_lhs(jnp.where(r >= c, 1.0, 0.0), x)


def _rows_of(x, n_rows):
    r = lax.broadcasted_iota(jnp.int32, (n_rows, x.shape[1]), 0)
    c = lax.broadcasted_iota(jnp.int32, (n_rows, x.shape[1]), 1)
    return _dot_nt_exact_lhs(jnp.where(r == c, 1.0, 0.0), x)


def _pad_rows(x, n):
    if x.shape[0] == n:
        return x
    return jnp.concatenate([x, jnp.zeros((n - x.shape[0],) + x.shape[1:], x.dtype)], axis=0)


def _silu(x):
    return x * jax.nn.sigmoid(x)


def _tri_masks(n):
    r = lax.broadcasted_iota(jnp.int32, (n, n), 0)
    c = lax.broadcasted_iota(jnp.int32, (n, n), 1)
    return r >= c, r > c, r == c


def _ada_body(c_ref, w_ref, b_ref, o_ref):
    cond = _silu(c_ref[...])
    o_ref[...] = _dot(cond, w_ref[...]) + b_ref[...]


def ada_mod(c_all, w_ada, b_ada):
    depth, d, n = w_ada.shape
    rows = c_all.shape[0]
    tn = 1024
    return pl.pallas_call(
        _ada_body,
        grid=(depth, n // tn),
        in_specs=[pl.BlockSpec((rows, d), lambda l, j: (0, 0)),
                  pl.BlockSpec((None, d, tn), lambda l, j: (l, 0, j)),
                  pl.BlockSpec((None, 1, tn), lambda l, j: (l, 0, j))],
        out_specs=pl.BlockSpec((None, rows, tn), lambda l, j: (l, 0, j)),
        out_shape=jax.ShapeDtypeStruct((depth, rows, n), F32),
        compiler_params=_cparams("arbitrary", "arbitrary"),
        name="ada_mod",
    )(c_all, w_ada, b_ada.reshape(depth, 1, n))


def _rows_body(*refs, has_m, coef, gate_idx, shift_idx, scale_idx, out_x, out_h, out_y, n_prompt_tiles, ts):
    it = iter(refs)
    x_ref = next(it)
    m_ref = next(it) if has_m else None
    gp_ref, gs_ref = (next(it), next(it)) if has_m else (None, None)
    mp_ref, ms_ref = (next(it), next(it)) if out_h else (None, None)
    gain_ref = next(it)
    xo_ref = next(it) if out_x else None
    h_ref = next(it) if out_h else None
    y_ref = next(it) if out_y else None
    i = pl.program_id(0)
    gain = gain_ref[...]

    def compute(shape3, gate, mod):
        x = x_ref[...]
        if shape3 is not None:
            x = x.reshape(shape3)
        if has_m:
            m = m_ref[...]
            if shape3 is not None:
                m = m.reshape(shape3)
            x = x + coef * gate(gate_idx) * m
        if out_x:
            xo_ref[...] = x.reshape(xo_ref.shape)
        var = jnp.mean(x * x, axis=-1, keepdims=True)
        y = x * lax.rsqrt(var + EPS) * gain
        if out_y:
            y_ref[...] = y.reshape(y_ref.shape)
        if out_h:
            h = y * (1.0 + mod(scale_idx)) + mod(shift_idx)
            h_ref[...] = h.reshape(h_ref.shape).astype(h_ref.dtype)

    @pl.when(i < n_prompt_tiles)
    def _():
        compute(None, lambda k: gp_ref[0, k:k + 1, :], lambda k: mp_ref[0, k:k + 1, :])

    @pl.when(i >= n_prompt_tiles)
    def _():
        rt, d = x_ref.shape
        compute((rt // ts, ts, d), lambda k: gs_ref[:, k:k + 1, :], lambda k: ms_ref[:, k:k + 1, :])


def rows_update(x, m, mod4, gain, geom, *, gate_layer=0, mod_layer=0, coef=0.0, gate_idx=0, shift_idx=0,
                scale_idx=0, out_x=False, out_h=False, out_y=False):
    bp, tp, bs, ts = geom
    mrows, d = x.shape
    n_prompt_tiles = bp * tp // ROW_TILE
    tiles_per_seq = tp // ROW_TILE
    grid = (mrows // ROW_TILE,)
    has_m = m is not None
    row_spec = pl.BlockSpec((ROW_TILE, d), lambda i: (i, 0))

    def mod_specs(layer):
        return [pl.BlockSpec((None, 1, N_MOD, d),
                             lambda i: (layer, bs + jnp.minimum(i // tiles_per_seq, bp - 1), 0, 0)),
                pl.BlockSpec((None, bs, N_MOD, d), lambda i: (layer, 0, 0, 0))]

    in_specs = [row_spec]
    args = [x]
    if has_m:
        in_specs += [row_spec] + mod_specs(gate_layer)
        args += [m, mod4, mod4]
    if out_h:
        in_specs += mod_specs(mod_layer)
        args += [mod4, mod4]
    in_specs.append(pl.BlockSpec((1, d), lambda i: (0, 0)))
    args.append(gain.reshape(1, d))
    out_specs, out_shape = [], []
    if out_x:
        out_specs.append(row_spec)
        out_shape.append(jax.ShapeDtypeStruct((mrows, d), F32))
    if out_h:
        out_specs.append(row_spec)
        out_shape.append(jax.ShapeDtypeStruct((mrows, d), BF16))
    if out_y:
        out_specs.append(row_spec)
        out_shape.append(jax.ShapeDtypeStruct((mrows, d), F32))
    body = functools.partial(_rows_body, has_m=has_m, coef=coef, gate_idx=gate_idx, shift_idx=shift_idx,
                             scale_idx=scale_idx, out_x=out_x, out_h=out_h, out_y=out_y,
                             n_prompt_tiles=n_prompt_tiles, ts=ts)
    return pl.pallas_call(
        body, grid=grid, in_specs=in_specs, out_specs=out_specs, out_shape=out_shape,
        compiler_params=_cparams("arbitrary"), name="rows_update",
    )(*args)


def _mm_body(a_ref, w_ref, o_ref, wb_ref):
    @pl.when(pl.program_id(1) == 0)
    def _():
        wb_ref[...] = w_ref[...].astype(BF16)

    o_ref[...] = jnp.dot(a_ref[...], wb_ref[...], preferred_element_type=F32).astype(o_ref.dtype)


def _mm_swiglu_body(a_ref, wa_ref, wg_ref, o_ref, wb_ref):
    tn = o_ref.shape[1]

    @pl.when(pl.program_id(1) == 0)
    def _():
        wb_ref[:, :tn] = wa_ref[...].astype(BF16)
        wb_ref[:, tn:] = wg_ref[...].astype(BF16)

    u = jnp.dot(a_ref[...], wb_ref[...], preferred_element_type=F32)
    o_ref[...] = (_silu(u[:, :tn]) * u[:, tn:]).astype(o_ref.dtype)


def _pick(n, cands):
    for c in cands:
        if n % c == 0:
            return c
    raise ValueError(f"no tile for {n} in {cands}")


def matmul(a, w, *, layer=None, tm_cands=(768, 512, 256), tn=512, out_dtype=F32, name="matmul"):
    mrows, k = a.shape
    n = w.shape[-1]
    tm = _pick(mrows, tm_cands)
    if w.ndim == 3:
        w_spec = pl.BlockSpec((None, k, tn), lambda j, i: (layer, 0, j))
    else:
        w_spec = pl.BlockSpec((k, tn), lambda j, i: (0, j))
    return pl.pallas_call(
        _mm_body,
        grid=(n // tn, mrows // tm),
        in_specs=[pl.BlockSpec((tm, k), lambda j, i: (i, 0)), w_spec],
        out_specs=pl.BlockSpec((tm, tn), lambda j, i: (i, j)),
        out_shape=jax.ShapeDtypeStruct((mrows, n), out_dtype),
        scratch_shapes=[pltpu.VMEM((k, tn), BF16)],
        compiler_params=_cparams("arbitrary", "arbitrary"),
        name=name,
    )(a, w)


def matmul_swiglu(a, w_in, layer, *, tn=512, tm_cands=(768, 512, 256)):
    mrows, k = a.shape
    f = w_in.shape[-1] // 2
    tm = _pick(mrows, tm_cands)
    nt = f // tn
    return pl.pallas_call(
        _mm_swiglu_body,
        grid=(nt, mrows // tm),
        in_specs=[pl.BlockSpec((tm, k), lambda j, i: (i, 0)),
                  pl.BlockSpec((None, k, tn), lambda j, i: (layer, 0, j)),
                  pl.BlockSpec((None, k, tn), lambda j, i: (layer, 0, j + nt))],
        out_specs=pl.BlockSpec((tm, tn), lambda j, i: (i, j)),
        out_shape=jax.ShapeDtypeStruct((mrows, f), BF16),
        scratch_shapes=[pltpu.VMEM((k, 2 * tn), BF16)],
        compiler_params=_cparams("arbitrary", "arbitrary"),
        name="ffn_in_swiglu",
    )(a, w_in, w_in)


def _conv_chunk(ext_ref, raw, w_ref, cin):
    ext_ref[SUBLANES:SUBLANES + CHUNK, :] = _pad_rows(raw, CHUNK)
    acc = None
    for j in range(CONV_W):
        off = SUBLANES - (CONV_W - 1) + j
        term = ext_ref[off:off + CHUNK, :] * w_ref[j:j + 1, :]
        acc = term if acc is None else acc + term
    return acc


def _conv_init(ext_ref, c0_ref, has_state):
    ext_ref[0:SUBLANES, :] = jnp.zeros((SUBLANES, ext_ref.shape[1]), F32)
    if has_state:
        ext_ref[SUBLANES - (CONV_W - 1):SUBLANES, :] = c0_ref[0]


def _conv_tail(ext_ref, cin):
    return ext_ref[cin + SUBLANES - (CONV_W - 1):cin + SUBLANES, :]


def _conv_advance(ext_ref):
    ext_ref[0:SUBLANES, :] = ext_ref[CHUNK:CHUNK + SUBLANES, :]


def _gla_body(q_ref, k_ref, v_ref, r_ref, g_ref, wg_ref, bg_ref, nrm_ref, s0_ref, o_ref, so_ref, st_ref,
              *, cin, has_state):
    c = pl.program_id(2)

    @pl.when(c == 0)
    def _():
        if has_state:
            st_ref[...] = s0_ref[0, 0].T
        else:
            st_ref[...] = jnp.zeros(st_ref.shape, F32)

    q = _pad_rows(q_ref[...], CHUNK) * (GLA_DK ** -0.5)
    k = _pad_rows(k_ref[...], CHUNK)
    v = _pad_rows(v_ref[...], CHUNK)
    glr = _pad_rows(g_ref[...], CHUNK)
    x = _dot(glr, wg_ref[...]) + bg_ref[...]
    logf = jax.nn.log_sigmoid(x) * (1.0 / GLA_TAU)
    if cin < CHUNK:
        rows = lax.broadcasted_iota(jnp.int32, logf.shape, 0)
        logf = jnp.where(rows < cin, logf, 0.0)
    b = _cumsum_rows(logf)
    bmid = b[CHUNK // 2 - 1:CHUNK // 2, :]
    blast = b[CHUNK - 1:CHUNK, :]
    incl, _, _ = _tri_masks(CHUNK)
    att = _dot_nt(q * jnp.exp(b - bmid), k * jnp.exp(bmid - b))
    att = jnp.where(incl, att, 0.0)
    st = st_ref[...]
    o = _dot(att, v) + _dot_nt(q * jnp.exp(b), st)
    st_new = st * jnp.exp(blast) + _dot_tn(v, k * jnp.exp(blast - b))
    st_ref[...] = st_new

    var = jnp.mean(o * o, axis=-1, keepdims=True)
    y = o * lax.rsqrt(var + EPS) * nrm_ref[...]
    y = y[:cin] * _silu(r_ref[...])
    o_ref[0] = y

    @pl.when(c == pl.num_programs(2) - 1)
    def _():
        so_ref[0, 0] = st_new.T


def gla_mixer(pm, psm, wg_pad, b_gate, nrm, s0, *, row0, nb, t):
    cin = min(t, CHUNK)
    nc = t // cin
    rb0 = row0 // cin
    has_state = s0 is not None
    if s0 is None:
        s0 = jnp.zeros((1, 1, GLA_DK, GLA_DV), F32)
        s0_map = lambda b, h, c: (0, 0, 0, 0)
    else:
        s0_map = lambda b, h, c: (b, h, 0, 0)

    def rmap(col0, width):
        return lambda b, h, c: (rb0 + b * nc + c, col0 // width + h)

    body = functools.partial(_gla_body, cin=cin, has_state=has_state)
    return pl.pallas_call(
        body,
        grid=(nb, GLA_HEADS, nc),
        in_specs=[pl.BlockSpec((cin, GLA_DK), rmap(EV_Q, GLA_DK)),
                  pl.BlockSpec((cin, GLA_DK), rmap(EV_K, GLA_DK)),
                  pl.BlockSpec((cin, GLA_DV), rmap(EV_V, GLA_DV)),
                  pl.BlockSpec((cin, GLA_DV), rmap(EV_R, GLA_DV)),
                  pl.BlockSpec((cin, LANES), lambda b, h, c: (rb0 + b * nc + c, 0)),
                  pl.BlockSpec((LANES, GLA_DK), lambda b, h, c: (0, h)),
                  pl.BlockSpec((1, GLA_DK), lambda b, h, c: (0, h)),
                  pl.BlockSpec((1, GLA_DV), lambda b, h, c: (0, 0)),
                  pl.BlockSpec((1, 1, GLA_DK, GLA_DV), s0_map)],
        out_specs=[pl.BlockSpec((1, cin, GLA_DV), lambda b, h, c: (b, c, h)),
                   pl.BlockSpec((1, 1, GLA_DK, GLA_DV), lambda b, h, c: (b, h, 0, 0))],
        out_shape=[jax.ShapeDtypeStruct((nb, t, GLA_VW), F32),
                   jax.ShapeDtypeStruct((nb, GLA_HEADS, GLA_DK, GLA_DV), F32)],
        scratch_shapes=[pltpu.VMEM((GLA_DV, GLA_DK), F32)],
        compiler_params=_cparams("arbitrary", "arbitrary", "arbitrary"),
        name="gla_mixer",
    )(pm, pm, pm, pm, psm, wg_pad, b_gate.reshape(1, GLA_QK), nrm.reshape(1, GLA_DV), s0)


def _ssd_body(z_ref, x_ref, b_ref, c_ref, sm_ref, cwx_ref, cwb_ref, cwc_ref, cbx_ref, cbb_ref, cbc_ref,
              dtb_ref, alog_ref, dexp_ref, nrm_ref, e_ref, s0_ref, c0x_ref, c0b_ref, c0c_ref,
              y_ref, so_ref, cox_ref, cob_ref, coc_ref,
              st_ref, ex_ref, eb_ref, ec_ref, *, cin, has_state):
    c = pl.program_id(1)

    @pl.when(c == 0)
    def _():
        if has_state:
            st_ref[...] = s0_ref[0].T
        else:
            st_ref[...] = jnp.zeros(st_ref.shape, F32)
        _conv_init(ex_ref, c0x_ref, has_state)
        _conv_init(eb_ref, c0b_ref, has_state)
        _conv_init(ec_ref, c0c_ref, has_state)

    xs = _silu(_conv_chunk(ex_ref, x_ref[...], cwx_ref, cin) + cbx_ref[...])
    bm = _silu(_conv_chunk(eb_ref, b_ref[...], cwb_ref, cin) + cbb_ref[...])
    cm = _silu(_conv_chunk(ec_ref, c_ref[...], cwc_ref, cin) + cbc_ref[...])

    dt = jax.nn.softplus(_pad_rows(sm_ref[...], CHUNK) + dtb_ref[...])
    if cin < CHUNK:
        rows = lax.broadcasted_iota(jnp.int32, dt.shape, 0)
        dt = jnp.where(rows < cin, dt, 0.0)
    a = dt * (-jnp.exp(alog_ref[...]))
    cum = _cumsum_rows(a)
    cum_t = _rows_of(cum, SSD_HEADS)
    last = cum[CHUNK - 1:CHUNK, :]
    stacked = jnp.concatenate([dt, jnp.exp(cum), jnp.exp(last - cum)], axis=0)
    ex = _dot_exact_rhs(stacked, e_ref[...])
    dt_e, ecum_e, w_e = ex[:CHUNK], ex[CHUNK:2 * CHUNK], ex[2 * CHUNK:]
    xdt = xs * dt_e
    xw = xdt * w_e
    incl, _, _ = _tri_masks(CHUNK)
    lane = lax.broadcasted_iota(jnp.int32, (CHUNK, LANES), 1)
    st = st_ref[...]
    gw = SSD_HPG * SSD_P
    ys = []
    for g in range(SSD_GROUPS):
        bg = bm[:, g * SSD_STATE:(g + 1) * SSD_STATE]
        cg = cm[:, g * SSD_STATE:(g + 1) * SSD_STATE]
        cb = _dot_nt(cg, bg)
        for pr in range(SSD_HPG // 2):
            h0 = g * SSD_HPG + 2 * pr
            atts = []
            for h in (h0, h0 + 1):
                seg = cum[:, h:h + 1] - cum_t[h:h + 1, :]
                atts.append(cb * jnp.exp(jnp.where(incl, seg, NEG_BIG)))
            xp = xdt[:, h0 * SSD_P:(h0 + 2) * SSD_P]
            yy = _dot(jnp.concatenate(atts, axis=0), xp)
            ys.append(jnp.where(lane < SSD_P, yy[:CHUNK], yy[CHUNK:]))
        sg = st[:, g * gw:(g + 1) * gw]
        ys_inter = _dot(cg, sg) * ecum_e[:, g * gw:(g + 1) * gw]
        for pr in range(SSD_HPG // 2):
            idx = g * (SSD_HPG // 2) + pr
            ys[idx] = ys[idx] + ys_inter[:, pr * LANES:(pr + 1) * LANES]
        st_ref[:, g * gw:(g + 1) * gw] = sg * ecum_e[CHUNK - 1:CHUNK, g * gw:(g + 1) * gw] + _dot_tn(
            bg, xw[:, g * gw:(g + 1) * gw])
    y = jnp.concatenate(ys, axis=1) + xs * dexp_ref[...]
    y = y[:cin] * _silu(z_ref[...])
    var = jnp.mean(y * y, axis=-1, keepdims=True)
    y_ref[0] = y * lax.rsqrt(var + EPS) * nrm_ref[...]

    @pl.when(c == pl.num_programs(1) - 1)
    def _():
        so_ref[0] = st_ref[...].T
        cox_ref[0] = _conv_tail(ex_ref, cin)
        cob_ref[0] = _conv_tail(eb_ref, cin)
        coc_ref[0] = _conv_tail(ec_ref, cin)

    if cin == CHUNK:
        _conv_advance(ex_ref)
        _conv_advance(eb_ref)
        _conv_advance(ec_ref)


def ssd_mixer(pm, psm, conv_w, conv_b, dtb_pad, alog_pad, d_exp, nrm, expand, s0, c0, *, row0, nb, t):
    cin = min(t, CHUNK)
    nc = t // cin
    rb0 = row0 // cin
    has_state = s0 is not None
    hp = SSD_INNER
    if s0 is None:
        s0 = jnp.zeros((1, hp, SSD_STATE), F32)
        c0 = jnp.zeros((1, CONV_W - 1, SSD_CONV_DIM), F32)
        bmap = lambda b, c: 0
    else:
        s0 = s0.reshape(nb, hp, SSD_STATE)
        bmap = lambda b, c: b

    def rmap(col0, width):
        return lambda b, c: (rb0 + b * nc + c, col0 // width)

    const = lambda b, c: (0, 0)
    kconv = CONV_W - 1
    body = functools.partial(_ssd_body, cin=cin, has_state=has_state)
    outs = pl.pallas_call(
        body,
        grid=(nb, nc),
        in_specs=[pl.BlockSpec((cin, hp), rmap(EV_Z, hp)),
                  pl.BlockSpec((cin, hp), rmap(EV_X, hp)),
                  pl.BlockSpec((cin, SSD_BC), rmap(EV_B, SSD_BC)),
                  pl.BlockSpec((cin, SSD_BC), rmap(EV_C, SSD_BC)),
                  pl.BlockSpec((cin, LANES), lambda b, c: (rb0 + b * nc + c, 1)),
                  pl.BlockSpec((CONV_W, hp), lambda b, c: (0, 0)),
                  pl.BlockSpec((CONV_W, SSD_BC), lambda b, c: (0, hp // SSD_BC)),
                  pl.BlockSpec((CONV_W, SSD_BC), lambda b, c: (0, hp // SSD_BC + 1)),
                  pl.BlockSpec((1, hp), lambda b, c: (0, 0)),
                  pl.BlockSpec((1, SSD_BC), lambda b, c: (0, hp // SSD_BC)),
                  pl.BlockSpec((1, SSD_BC), lambda b, c: (0, hp // SSD_BC + 1)),
                  pl.BlockSpec((1, LANES), const),
                  pl.BlockSpec((1, LANES), const),
                  pl.BlockSpec((1, hp), const),
                  pl.BlockSpec((1, hp), const),
                  pl.BlockSpec((LANES, hp), const),
                  pl.BlockSpec((1, hp, SSD_STATE), lambda b, c: (bmap(b, c), 0, 0)),
                  pl.BlockSpec((1, kconv, hp), lambda b, c: (bmap(b, c), 0, 0)),
                  pl.BlockSpec((1, kconv, SSD_BC), lambda b, c: (bmap(b, c), 0, hp // SSD_BC)),
                  pl.BlockSpec((1, kconv, SSD_BC), lambda b, c: (bmap(b, c), 0, hp // SSD_BC + 1))],
        out_specs=[pl.BlockSpec((1, cin, hp), lambda b, c: (b, c, 0)),
                   pl.BlockSpec((1, hp, SSD_STATE), lambda b, c: (b, 0, 0)),
                   pl.BlockSpec((1, kconv, hp), lambda b, c: (b, 0, 0)),
                   pl.BlockSpec((1, kconv, SSD_BC), lambda b, c: (b, 0, 0)),
                   pl.BlockSpec((1, kconv, SSD_BC), lambda b, c: (b, 0, 0))],
        out_shape=[jax.ShapeDtypeStruct((nb, t, hp), F32),
                   jax.ShapeDtypeStruct((nb, hp, SSD_STATE), F32),
                   jax.ShapeDtypeStruct((nb, kconv, hp), F32),
                   jax.ShapeDtypeStruct((nb, kconv, SSD_BC), F32),
                   jax.ShapeDtypeStruct((nb, kconv, SSD_BC), F32)],
        scratch_shapes=[pltpu.VMEM((SSD_STATE, hp), F32),
                        pltpu.VMEM((CHUNK + SUBLANES, hp), F32),
                        pltpu.VMEM((CHUNK + SUBLANES, SSD_BC), F32),
                        pltpu.VMEM((CHUNK + SUBLANES, SSD_BC), F32)],
        compiler_params=_cparams("arbitrary", "arbitrary"),
        name="ssd_mixer",
    )(pm, pm, pm, pm, psm, conv_w, conv_w, conv_w, conv_b, conv_b, conv_b, dtb_pad, alog_pad, d_exp,
      nrm.reshape(1, hp), expand, s0, c0, c0, c0)
    y, s_new, cx, cb_, cc = outs
    conv_s = jnp.concatenate([cx, cb_, cc], axis=-1)
    return y, s_new.reshape(nb, SSD_HEADS, SSD_P, SSD_STATE), conv_s


def _l2norm_heads(x, n_heads, width, scale):
    outs = []
    for e in range(n_heads):
        xe = x[:, e * width:(e + 1) * width]
        ss = jnp.sum(xe * xe, axis=-1, keepdims=True)
        outs.append(xe * (lax.rsqrt(ss + EPS) * scale))
    return outs


def _unit_lower_inverse(a_mat, eye):
    nm = -a_mat
    t_inv = eye + nm
    p = nm
    for _ in range(int(math.log2(CHUNK)) - 1):
        p = _dot(p, p)
        t_inv = t_inv + _dot(t_inv, p)
    ah, al = _split2(a_mat)
    th, tl = _split2(t_inv)
    at = (jnp.dot(ah, th, preferred_element_type=F32) + jnp.dot(ah, tl, preferred_element_type=F32)
          + jnp.dot(al, th, preferred_element_type=F32))
    resid = eye - t_inv - at
    return t_inv + _dot(t_inv, resid)


def _gdn_body(q_ref, k_ref, v_ref, z_ref, sm_ref, cwq_ref, cwk_ref, cwv_ref, alog_ref, dtb_ref, nrm_ref,
              s0_ref, c0q_ref, c0k_ref, c0v_ref,
              o_ref, so_ref, coq_ref, cok_ref, cov_ref,
              s_ref, eq_ref, ek_ref, ev_ref, *, cin, has_state):
    c = pl.program_id(1)
    n = GDN_PACK * CHUNK

    @pl.when(c == 0)
    def _():
        if has_state:
            s_ref[...] = s0_ref[0]
        else:
            s_ref[...] = jnp.zeros(s_ref.shape, F32)
        _conv_init(eq_ref, c0q_ref, has_state)
        _conv_init(ek_ref, c0k_ref, has_state)
        _conv_init(ev_ref, c0v_ref, has_state)

    qc = _silu(_conv_chunk(eq_ref, q_ref[...], cwq_ref, cin))
    kc = _silu(_conv_chunk(ek_ref, k_ref[...], cwk_ref, cin))
    vc = _silu(_conv_chunk(ev_ref, v_ref[...], cwv_ref, cin))
    qs = _l2norm_heads(qc, GDN_QK_HEADS, GDN_DK, GDN_DK ** -0.5)
    ks = _l2norm_heads(kc, GDN_QK_HEADS, GDN_DK, 1.0)

    sm = _pad_rows(sm_ref[...], CHUNK)
    beta = jax.nn.sigmoid(sm)
    gl = -jnp.exp(alog_ref[...]) * jax.nn.softplus(sm + dtb_ref[...])
    if cin < CHUNK:
        rows = lax.broadcasted_iota(jnp.int32, sm.shape, 0)
        beta = jnp.where(rows < cin, beta, 0.0)
        gl = jnp.where(rows < cin, gl, 0.0)
    cum = _cumsum_rows(gl)
    last_row = cum[CHUNK - 1:CHUNK, :]

    rr = lax.broadcasted_iota(jnp.int32, (n, n), 0)
    cc = lax.broadcasted_iota(jnp.int32, (n, n), 1)
    same = (rr // CHUNK) == (cc // CHUNK)
    incl = same & (rr >= cc)
    strict = same & (rr > cc)
    eye = jnp.where(rr == cc, 1.0, 0.0)
    head_of_row = lax.broadcasted_iota(jnp.int32, (n, LANES), 0) // CHUNK
    lane = lax.broadcasted_iota(jnp.int32, (n, LANES), 1)
    ones = jnp.ones((n, LANES), BF16)
    bd_mask = (lax.broadcasted_iota(jnp.int32, (n, GDN_PACK * GDN_DK), 0) // CHUNK
               == lax.broadcasted_iota(jnp.int32, (n, GDN_PACK * GDN_DK), 1) // GDN_DK)
    srow = lax.broadcasted_iota(jnp.int32, (GDN_PACK * GDN_DK, LANES), 0) // GDN_DK
    slane = lax.broadcasted_iota(jnp.int32, (GDN_PACK * GDN_DK, LANES), 1)

    def block_diag(x):
        return jnp.where(bd_mask, jnp.concatenate([x] * GDN_PACK, axis=1), 0.0)

    def stack(parts):
        return jnp.concatenate(parts, axis=0)

    cum_rep = stack([cum] * GDN_PACK)
    beta_rep = stack([beta] * GDN_PACK)
    last_rep = jnp.broadcast_to(last_row, (n, LANES))
    outs = []
    for j in range(GDN_V_HEADS // GDN_PACK):
        h0 = j * GDN_PACK
        heads = range(h0, h0 + GDN_PACK)
        k_st = stack([ks[i // GDN_REP] for i in heads])
        q_st = stack([qs[i // GDN_REP] for i in heads])
        v_st = stack([vc[:, i * GDN_DV:(i + 1) * GDN_DV] for i in heads])
        g_sel = lane == GDN_V_HEADS + h0 + head_of_row
        cum_sel = jnp.where(g_sel, cum_rep, 0.0)
        col = jnp.sum(cum_sel, axis=1, keepdims=True)
        rowm = _dot_nt_exact_lhs(ones, cum_sel)
        bcol = jnp.sum(jnp.where(lane == h0 + head_of_row, beta_rep, 0.0), axis=1, keepdims=True)
        lastc = jnp.sum(jnp.where(g_sel, last_rep, 0.0), axis=1, keepdims=True)
        dec = jnp.exp(jnp.where(incl, col - rowm, NEG_BIG))
        kk = _dot_nt(k_st, k_st)
        qk = _dot_nt(q_st, k_st)
        a_mat = jnp.where(strict, bcol * kk * dec, 0.0)
        t_inv = _unit_lower_inverse(a_mat, eye)
        ecol = jnp.exp(col)
        rhs = jnp.concatenate([v_st * bcol, k_st * (bcol * ecol)], axis=1)
        sol = _dot(t_inv, rhs)
        s_st = s_ref[h0:h0 + GDN_PACK].reshape(GDN_PACK * GDN_DK, GDN_DV)
        delta = sol[:, :GDN_DV] - _dot(block_diag(sol[:, GDN_DV:]), s_st)
        o_st = _dot(block_diag(q_st * ecol), s_st) + _dot(jnp.where(incl, qk * dec, 0.0), delta)
        kw_bd = block_diag(k_st * jnp.exp(lastc - col))
        s_dec = jnp.exp(jnp.sum(jnp.where(slane == GDN_V_HEADS + h0 + srow,
                                          jnp.broadcast_to(last_row, srow.shape), 0.0), axis=1, keepdims=True))
        s_new = s_st * s_dec + _dot_tn(kw_bd, delta)
        s_ref[h0:h0 + GDN_PACK] = s_new.reshape(GDN_PACK, GDN_DK, GDN_DV)
        var = jnp.mean(o_st * o_st, axis=-1, keepdims=True)
        y_st = o_st * lax.rsqrt(var + EPS) * nrm_ref[...]
        outs += [y_st[i * CHUNK:(i + 1) * CHUNK] for i in range(GDN_PACK)]
    o = jnp.concatenate(outs, axis=1)
    o_ref[0] = o[:cin] * _silu(z_ref[...])

    @pl.when(c == pl.num_programs(1) - 1)
    def _():
        so_ref[0] = s_ref[...]
        coq_ref[0] = _conv_tail(eq_ref, cin)
        cok_ref[0] = _conv_tail(ek_ref, cin)
        cov_ref[0] = _conv_tail(ev_ref, cin)

    if cin == CHUNK:
        _conv_advance(eq_ref)
        _conv_advance(ek_ref)
        _conv_advance(ev_ref)


def gdn_mixer(pm, psm, conv_w, alog_pad, dtb_pad, nrm, s0, c0, *, row0, nb, t):
    cin = min(t, CHUNK)
    nc = t // cin
    rb0 = row0 // cin
    has_state = s0 is not None
    kconv = CONV_W - 1
    if s0 is None:
        s0 = jnp.zeros((1, GDN_V_HEADS, GDN_DK, GDN_DV), F32)
        c0 = jnp.zeros((1, kconv, GDN_CONV_DIM), F32)
        bmap = lambda b: 0
    else:
        bmap = lambda b: b

    def rmap(col0, width):
        return lambda b, c: (rb0 + b * nc + c, col0 // width)

    const = lambda b, c: (0, 0)
    body = functools.partial(_gdn_body, cin=cin, has_state=has_state)
    outs = pl.pallas_call(
        body,
        grid=(nb, nc),
        in_specs=[pl.BlockSpec((cin, GDN_QK), rmap(OD_Q, GDN_QK)),
                  pl.BlockSpec((cin, GDN_QK), rmap(OD_K, GDN_QK)),
                  pl.BlockSpec((cin, GDN_VW), rmap(OD_V, GDN_VW)),
                  pl.BlockSpec((cin, GDN_VW), rmap(OD_Z, GDN_VW)),
                  pl.BlockSpec((cin, LANES), lambda b, c: (rb0 + b * nc + c, 0)),
                  pl.BlockSpec((CONV_W, GDN_QK), lambda b, c: (0, 0)),
                  pl.BlockSpec((CONV_W, GDN_QK), lambda b, c: (0, 1)),
                  pl.BlockSpec((CONV_W, GDN_VW), lambda b, c: (0, 1)),
                  pl.BlockSpec((1, LANES), const),
                  pl.BlockSpec((1, LANES), const),
                  pl.BlockSpec((1, GDN_DV), const),
                  pl.BlockSpec((1, GDN_V_HEADS, GDN_DK, GDN_DV), lambda b, c: (bmap(b), 0, 0, 0)),
                  pl.BlockSpec((1, kconv, GDN_QK), lambda b, c: (bmap(b), 0, 0)),
                  pl.BlockSpec((1, kconv, GDN_QK), lambda b, c: (bmap(b), 0, 1)),
                  pl.BlockSpec((1, kconv, GDN_VW), lambda b, c: (bmap(b), 0, 1))],
        out_specs=[pl.BlockSpec((1, cin, GDN_VW), lambda b, c: (b, c, 0)),
                   pl.BlockSpec((1, GDN_V_HEADS, GDN_DK, GDN_DV), lambda b, c: (b, 0, 0, 0)),
                   pl.BlockSpec((1, kconv, GDN_QK), lambda b, c: (b, 0, 0)),
                   pl.BlockSpec((1, kconv, GDN_QK), lambda b, c: (b, 0, 0)),
                   pl.BlockSpec((1, kconv, GDN_VW), lambda b, c: (b, 0, 0))],
        out_shape=[jax.ShapeDtypeStruct((nb, t, GDN_VW), F32),
                   jax.ShapeDtypeStruct((nb, GDN_V_HEADS, GDN_DK, GDN_DV), F32),
                   jax.ShapeDtypeStruct((nb, kconv, GDN_QK), F32),
                   jax.ShapeDtypeStruct((nb, kconv, GDN_QK), F32),
                   jax.ShapeDtypeStruct((nb, kconv, GDN_VW), F32)],
        scratch_shapes=[pltpu.VMEM((GDN_V_HEADS, GDN_DK, GDN_DV), F32),
                        pltpu.VMEM((CHUNK + SUBLANES, GDN_QK), F32),
                        pltpu.VMEM((CHUNK + SUBLANES, GDN_QK), F32),
                        pltpu.VMEM((CHUNK + SUBLANES, GDN_VW), F32)],
        compiler_params=_cparams("arbitrary", "arbitrary"),
        name="gdn_mixer",
    )(pm, pm, pm, pm, psm, conv_w, conv_w, conv_w, alog_pad, dtb_pad, nrm.reshape(1, GDN_DV),
      s0, c0, c0, c0)
    o, s_new, cq, ck, cv = outs
    return o, s_new, jnp.concatenate([cq, ck, cv], axis=-1)


def _residue_perm(dil, inverse=False):
    n = ROW_TILE // dil
    i = lax.broadcasted_iota(jnp.int32, (ROW_TILE, ROW_TILE), 0)
    j = lax.broadcasted_iota(jnp.int32, (ROW_TILE, ROW_TILE), 1)
    if inverse:
        i, j = j, i
    return jnp.where(j == (i % n) * dil + i // n, 1.0, 0.0).astype(BF16)


def _rope_body(q_ref, k_ref, v_ref, inv_ref, qo_ref, ko_ref, *rest, n_prompt_tiles, tiles_per_seq, ts):
    i = pl.program_id(0)
    rt = q_ref.shape[0]
    r = lax.broadcasted_iota(jnp.int32, (rt, LANES), 0)
    pos_prompt = (i % tiles_per_seq) * rt + r
    pos_sample = PAST_LEN + r % ts
    pos = jnp.where(i < n_prompt_tiles, pos_prompt, pos_sample).astype(F32)
    ang = pos * inv_ref[...]
    cos = jnp.cos(ang)
    sin = jnp.sin(ang)
    lane = lax.broadcasted_iota(jnp.int32, (rt, LANES), 1)
    sin = jnp.where(lane < LANES // 2, -sin, sin)
    for src, dst in ((q_ref, qo_ref), (k_ref, ko_ref)):
        for h in range(DSA_W // DSA_HD):
            x = src[:, h * DSA_HD:(h + 1) * DSA_HD]
            dst[:, h * DSA_HD:(h + 1) * DSA_HD] = x * cos + pltpu.roll(x, DSA_HD // 2, 1) * sin
    for g, (_, dil) in enumerate(DSA_GROUPS):
        perm = None if dil == 1 else _residue_perm(dil)
        for src, dst in zip((qo_ref, ko_ref, v_ref), rest[3 * g:3 * g + 3]):
            xb = src[:, g * DSA_GW:(g + 1) * DSA_GW].astype(BF16)
            if perm is not None:
                xb = jnp.dot(perm, xb, preferred_element_type=F32).astype(BF16)
            dst[...] = xb.reshape(dst.shape)


def rope_qk(pm, geom):
    bp, tp, bs, ts = geom
    mrows = pm.shape[0]
    half = DSA_HD // 2
    inv = ROPE_THETA ** (-jnp.arange(half, dtype=F32) / half)
    inv2 = jnp.concatenate([inv, inv]).reshape(1, DSA_HD)
    body = functools.partial(_rope_body, n_prompt_tiles=bp * tp // ROW_TILE, tiles_per_seq=tp // ROW_TILE, ts=ts)
    spec_o = pl.BlockSpec((ROW_TILE, DSA_W), lambda i: (i, 0))
    out_specs = [spec_o, spec_o]
    out_shape = [jax.ShapeDtypeStruct((mrows, DSA_W), F32)] * 2
    for _, dil in DSA_GROUPS:
        n = ROW_TILE // dil
        out_specs += [pl.BlockSpec((dil, n, DSA_GW), lambda i: (0, i, 0))] * 3
        out_shape += [jax.ShapeDtypeStruct((dil, mrows // dil, DSA_GW), BF16)] * 3
    outs = pl.pallas_call(
        body,
        grid=(mrows // ROW_TILE,),
        in_specs=[pl.BlockSpec((ROW_TILE, DSA_W), lambda i: (i, OD_DQ // DSA_W)),
                  pl.BlockSpec((ROW_TILE, DSA_W), lambda i: (i, OD_DK // DSA_W)),
                  pl.BlockSpec((ROW_TILE, DSA_W), lambda i: (i, OD_DV // DSA_W)),
                  pl.BlockSpec((1, DSA_HD), lambda i: (0, 0))],
        out_specs=out_specs,
        out_shape=out_shape,
        compiler_params=_cparams("arbitrary"),
        name="rope_qk",
    )(pm, pm, pm, inv2)
    return outs[0], outs[1], [tuple(outs[2 + 3 * g:5 + 3 * g]) for g in range(len(DSA_GROUPS))]


def _lse_lanes(lses):
    rows = lses[0].shape[0]
    lane = lax.broadcasted_iota(jnp.int32, (rows, LANES), 1)
    out = jnp.zeros((rows, LANES), F32)
    for h, v in enumerate(lses):
        out = jnp.where(lane == h, v, out)
    return out


def _dsa_prompt_body(q_ref, kp_ref, kc_ref, vp_ref, vc_ref, o_ref, l_ref, *, jmax):
    n = pl.program_id(2)
    blk = DSA_BLOCK
    i_idx = lax.broadcasted_iota(jnp.int32, (blk, 2 * blk), 0)
    m_idx = lax.broadcasted_iota(jnp.int32, (blk, 2 * blk), 1)
    dist = i_idx + blk - m_idx
    valid = (dist >= 0) & (dist <= jmax) & ((m_idx >= blk) | (n > 0))
    scale = DSA_HD ** -0.5
    lses = []
    for h in range(DSA_HEADS):
        sl = slice(h * DSA_HD, (h + 1) * DSA_HD)
        kk = jnp.concatenate([kp_ref[:, sl], kc_ref[:, sl]], axis=0)
        vv = jnp.concatenate([vp_ref[:, sl], vc_ref[:, sl]], axis=0)
        s = _dot_nt(q_ref[:, sl], kk) * scale
        s = jnp.where(valid, s, NEG_BIG)
        mx = jnp.max(s, axis=-1, keepdims=True)
        p = jnp.exp(s - mx)
        den = jnp.sum(p, axis=-1, keepdims=True)
        o_ref[:, sl] = (_dot(p, vv) / den).astype(o_ref.dtype)
        lses.append(mx + jnp.log(den))
    l_ref[...] = _lse_lanes(lses)


def dsa_prompt(qkv, g, geom):
    bp, tp, _, _ = geom
    window, dil = DSA_GROUPS[g]
    ls = tp // dil
    nblk = ls // DSA_BLOCK
    q, k, v = qkv
    mu = q.shape[1]

    cur = lambda b, r, n: (r, b * nblk + n, 0)
    prev = lambda b, r, n: (r, b * nblk + jnp.maximum(n - 1, 0), 0)
    blk = (None, DSA_BLOCK, DSA_GW)
    return pl.pallas_call(
        functools.partial(_dsa_prompt_body, jmax=window // dil),
        grid=(bp, dil, nblk),
        in_specs=[pl.BlockSpec(blk, cur), pl.BlockSpec(blk, prev), pl.BlockSpec(blk, cur),
                  pl.BlockSpec(blk, prev), pl.BlockSpec(blk, cur)],
        out_specs=[pl.BlockSpec(blk, cur), pl.BlockSpec((None, DSA_BLOCK, LANES), cur)],
        out_shape=[jax.ShapeDtypeStruct((dil, mu, DSA_GW), BF16),
                   jax.ShapeDtypeStruct((dil, mu, LANES), F32)],
        compiler_params=_cparams("arbitrary", "arbitrary", "arbitrary"),
        name=f"dsa_prompt_w{window}",
    )(q, k, k, v, v)


def _dsa_sample_body(q_ref, kn_ref, vn_ref, cache_ref, o_ref, l_ref, m_scr, d_scr, acc_scr, *, window, dil, lb):
    j = pl.program_id(1)
    nj = pl.num_programs(1)
    ts = q_ref.shape[0]
    nh = DSA_HEADS
    scale = DSA_HD ** -0.5

    def heads_to_rows(ref):
        return jnp.concatenate([ref[:, h * DSA_HD:(h + 1) * DSA_HD] for h in range(nh)], axis=0)

    @pl.when(j == 0)
    def _():
        m_scr[...] = jnp.full(m_scr.shape, NEG_BIG, F32)
        d_scr[...] = jnp.zeros(d_scr.shape, F32)
        acc_scr[...] = jnp.zeros(acc_scr.shape, F32)

    def update(s, vals, ok):
        s = jnp.where(ok, s, NEG_BIG)
        m_old = m_scr[...]
        m_new = jnp.maximum(m_old, jnp.max(s, axis=-1, keepdims=True))
        alpha = jnp.exp(m_old - m_new)
        p = jnp.where(ok, jnp.exp(s - m_new[:, 0:1]), 0.0)
        d_scr[...] = alpha * d_scr[...] + jnp.sum(p, axis=-1, keepdims=True)
        acc_scr[...] = alpha * acc_scr[...] + _dot(p, vals)
        m_scr[...] = m_new

    q_rows = heads_to_rows(q_ref) * scale
    x4 = cache_ref[0].reshape(lb, 2, nh, DSA_HD)
    k_all = x4[:, 0].reshape(lb * nh, DSA_HD)
    v_all = x4[:, 1].reshape(lb * nh, DSA_HD)
    r = lax.broadcasted_iota(jnp.int32, (nh * ts, lb * nh), 0)
    c = lax.broadcasted_iota(jnp.int32, (nh * ts, lb * nh), 1)
    dist = window + r % ts - (j * lb + c // nh)
    ok = (r // ts == c % nh) & ((dist & (dil - 1)) == 0) & (dist <= window)
    update(_dot_nt(q_rows, k_all), v_all, ok)

    @pl.when(j == nj - 1)
    def _():
        r2 = lax.broadcasted_iota(jnp.int32, (nh * ts, nh * ts), 0)
        c2 = lax.broadcasted_iota(jnp.int32, (nh * ts, nh * ts), 1)
        d2 = r2 % ts - c2 % ts
        ok2 = (r2 // ts == c2 // ts) & (d2 >= 0) & ((d2 & (dil - 1)) == 0)
        update(_dot_nt(q_rows, heads_to_rows(kn_ref)), heads_to_rows(vn_ref), ok2)
        o_rows = acc_scr[...] / d_scr[...]
        lse = m_scr[...] + jnp.log(d_scr[...])
        for h in range(nh):
            o_ref[:, h * DSA_HD:(h + 1) * DSA_HD] = o_rows[h * ts:(h + 1) * ts]
        l_ref[...] = _lse_lanes([lse[h * ts:(h + 1) * ts, 0:1] for h in range(nh)])


def dsa_sample(rq, rk, pm, cache, g, geom):
    bp, tp, bs, ts = geom
    window, dil = DSA_GROUPS[g]
    lcache = cache.shape[1]
    assert lcache == window, "cache must hold exactly the window"
    lb = min(lcache, 512)
    rb0 = bp * tp // ts
    kv_rows = 2 * DSA_HEADS
    cache3 = cache.reshape(bs, lcache * kv_rows, DSA_HD)
    blk = (ts, DSA_GW)
    body = functools.partial(_dsa_sample_body, window=window, dil=dil, lb=lb)
    return pl.pallas_call(
        body,
        grid=(bs, lcache // lb),
        in_specs=[pl.BlockSpec(blk, lambda b, j: (rb0 + b, g)),
                  pl.BlockSpec(blk, lambda b, j: (rb0 + b, g)),
                  pl.BlockSpec(blk, lambda b, j: (rb0 + b, OD_DV // DSA_GW + g)),
                  pl.BlockSpec((1, lb * kv_rows, DSA_HD), lambda b, j: (b, j, 0))],
        out_specs=[pl.BlockSpec(blk, lambda b, j: (b, 0)), pl.BlockSpec((ts, LANES), lambda b, j: (b, 0))],
        out_shape=[jax.ShapeDtypeStruct((bs * ts, DSA_GW), F32),
                   jax.ShapeDtypeStruct((bs * ts, LANES), F32)],
        scratch_shapes=[pltpu.VMEM((DSA_HEADS * ts, DSA_HD), F32)] * 3,
        compiler_params=_cparams("arbitrary", "arbitrary"),
        name=f"dsa_sample_w{window}",
    )(rq, rk, pm, cache3)


def _dsa_merge_body(o0, o1, o2, l0, l1, l2, e_ref, out_ref, *, dils):
    os_, ls_ = [], []
    for o_ref, l_ref, dil in zip((o0, o1, o2), (l0, l1, l2), dils):
        o = o_ref[...].reshape(ROW_TILE, DSA_GW)
        l = l_ref[...].reshape(ROW_TILE, LANES)
        if dil > 1:
            pinv = _residue_perm(dil, inverse=True)
            o = jnp.dot(pinv, o.astype(BF16), preferred_element_type=F32)
            l = _dot_exact_lhs(pinv, l)
        os_.append(o.astype(F32))
        ls_.append(l)
    mx = jnp.maximum(jnp.maximum(ls_[0], ls_[1]), ls_[2])
    ws = [jnp.exp(l - mx) for l in ls_]
    tot = ws[0] + ws[1] + ws[2]
    acc = None
    for w, o in zip(ws, os_):
        term = _dot_exact_rhs(w / tot, e_ref[...]) * o
        acc = term if acc is None else acc + term
    out_ref[...] = acc


def dsa_merge(os_, ls_, dils, n_tiles):
    in_specs = []
    for width in (DSA_GW, LANES):
        for dil in dils:
            if dil:
                in_specs.append(pl.BlockSpec((dil, ROW_TILE // dil, width), lambda i: (0, i, 0)))
            else:
                in_specs.append(pl.BlockSpec((ROW_TILE, width), lambda i: (i, 0)))
    in_specs.append(pl.BlockSpec((LANES, DSA_GW), lambda i: (0, 0)))
    expand = jnp.asarray((np.arange(LANES)[:, None] == (np.arange(DSA_GW)[None, :] // DSA_HD)).astype(np.float32))
    return pl.pallas_call(
        functools.partial(_dsa_merge_body, dils=tuple(max(d, 1) for d in dils)),
        grid=(n_tiles,),
        in_specs=in_specs,
        out_specs=pl.BlockSpec((ROW_TILE, DSA_GW), lambda i: (i, 0)),
        out_shape=jax.ShapeDtypeStruct((n_tiles * ROW_TILE, DSA_GW), F32),
        compiler_params=_cparams("arbitrary"),
        name="dsa_merge",
    )(*os_, *ls_, expand)


def _even_weights(w):
    o = np.cumsum([0, GLA_QK, GLA_QK, GLA_VW, GLA_VW, GLA_RANK, SSD_INNER, SSD_CONV_DIM, SSD_HEADS])
    q, k, v, r, glr, z, xbc, dt = (w[:, o[i]:o[i + 1]] for i in range(8))
    xs, bm, cm = xbc[:, :SSD_INNER], xbc[:, SSD_INNER:SSD_INNER + SSD_BC], xbc[:, SSD_INNER + SSD_BC:]
    main = jnp.concatenate([z, xs, v, r, q, k, bm, cm], axis=1)
    d = w.shape[0]
    small = jnp.concatenate([glr, jnp.zeros((d, LANES - GLA_RANK), F32), dt,
                             jnp.zeros((d, LANES - SSD_HEADS), F32)], axis=1)
    return main, small


def _odd_weights(w):
    o = np.cumsum([0, GDN_QK, GDN_QK, GDN_VW, GDN_VW, GDN_V_HEADS, GDN_V_HEADS, DSA_W, DSA_W, DSA_W])
    q, k, v, z, b, a, dq, dk, dv = (w[:, o[i]:o[i + 1]] for i in range(9))
    main = jnp.concatenate([v, z, q, k, dq, dk, dv], axis=1)
    small = jnp.concatenate([b, a, jnp.zeros((w.shape[0], LANES - 2 * GDN_V_HEADS), F32)], axis=1)
    return main, small


def _pad_lanes(vec, offset=0):
    return jnp.zeros((1, LANES), F32).at[0, offset:offset + vec.shape[0]].set(vec)


def kernel(x_prompt, x_sample, c_prompt, c_sample, state_gla, state_ssd, state_ssd_conv, state_gdn, state_gdn_conv, cache_dsa_w128, cache_dsa_w512, cache_dsa_w2048, ln_ffn1, ln_mix, ln_ffn2, w_ada, b_ada, ffn1_w_in, ffn1_w_out, ffn2_w_in, ffn2_w_out, even_w_in, gla_w_gate2, gla_b_gate, gla_norm, ssd_conv_w, ssd_conv_b, ssd_dt_bias, ssd_A_log, ssd_D, ssd_norm, even_w_out, odd_w_in, gdn_conv_w, gdn_A_log, gdn_dt_bias, gdn_norm, odd_w_out, final_norm):
    bp, tp, d = x_prompt.shape
    bs, ts, _ = x_sample.shape
    depth = w_ada.shape[0]
    geom = (bp, tp, bs, ts)
    assert tp % ROW_TILE == 0 and bs * ts == ROW_TILE and ts == SUBLANES
    mp = bp * tp
    caches = (cache_dsa_w128, cache_dsa_w512, cache_dsa_w2048)

    x = jnp.concatenate([x_prompt.reshape(mp, d), x_sample.reshape(bs * ts, d)], axis=0)
    n_pad = (-(bs + bp)) % SUBLANES
    c_all = jnp.concatenate([c_sample, c_prompt, jnp.zeros((n_pad, d), F32)], axis=0)
    mod4 = ada_mod(c_all, w_ada, b_ada).reshape(depth, c_all.shape[0], N_MOD, d)
    expand = jnp.asarray((np.arange(LANES)[:, None] == (np.arange(SSD_INNER)[None, :] // SSD_P)).astype(np.float32))

    new = {k: ([], []) for k in ("gla", "ssd", "ssd_conv", "gdn", "gdn_conv", "dsa0", "dsa1", "dsa2")}

    h = rows_update(x, None, mod4, ln_ffn1[0], geom, mod_layer=0, shift_idx=0, scale_idx=1, out_h=True)[0]
    for l in range(depth):
        act = matmul_swiglu(h, ffn1_w_in, l)
        m = matmul(act, ffn1_w_out, layer=l, tm_cands=(384, 256), tn=512, name="ffn_out")
        x, h = rows_update(x, m, mod4, ln_mix[l], geom, gate_layer=l, mod_layer=l, coef=0.5, gate_idx=2,
                           shift_idx=3, scale_idx=4, out_x=True, out_h=True)
        i = l // 2
        if l % 2 == 0:
            w_main, w_small = _even_weights(even_w_in[i])
            pm = matmul(h, w_main, tn=1024, name="even_in")
            psm = matmul(h, w_small, tn=256, name="even_in_small")
            wg_pad = jnp.zeros((LANES, GLA_QK), F32).at[:GLA_RANK].set(gla_w_gate2[i])
            dtb = _pad_lanes(ssd_dt_bias[i])
            alog = _pad_lanes(ssd_A_log[i])
            d_exp = jnp.repeat(ssd_D[i], SSD_P).reshape(1, SSD_INNER)
            ys = []
            for grp, (row0, nb, t, s_gla, s_ssd, s_conv) in enumerate((
                    (0, bp, tp, None, None, None),
                    (mp, bs, ts, state_gla[i], state_ssd[i], state_ssd_conv[i]))):
                o_gla, gla_new = gla_mixer(pm, psm, wg_pad, gla_b_gate[i], gla_norm[i], s_gla,
                                           row0=row0, nb=nb, t=t)
                y_ssd, ssd_new, conv_new = ssd_mixer(pm, psm, ssd_conv_w[i], ssd_conv_b[i].reshape(1, -1), dtb,
                                                     alog, d_exp, ssd_norm[i], expand, s_ssd, s_conv,
                                                     row0=row0, nb=nb, t=t)
                new["gla"][grp].append(gla_new)
                new["ssd"][grp].append(ssd_new)
                new["ssd_conv"][grp].append(conv_new)
                ys.append(jnp.concatenate([o_gla.reshape(nb * t, GLA_VW), y_ssd.reshape(nb * t, SSD_INNER)],
                                          axis=1))
            y_all = jnp.concatenate(ys, axis=0).astype(BF16)
            m = matmul(y_all, even_w_out[i], tn=512, name="even_out")
        else:
            w_main, w_small = _odd_weights(odd_w_in[i])
            pm = matmul(h, w_main, tn=1024, name="odd_in")
            psm = matmul(h, w_small, tn=LANES, name="odd_in_small")
            rq, rk, qkv = rope_qk(pm, geom)
            alog = _pad_lanes(gdn_A_log[i], GDN_V_HEADS)
            dtb = _pad_lanes(gdn_dt_bias[i], GDN_V_HEADS)
            ys = []
            for grp, (row0, nb, t, s_gdn, s_conv) in enumerate((
                    (0, bp, tp, None, None), (mp, bs, ts, state_gdn[i], state_gdn_conv[i]))):
                o_gdn, gdn_new, conv_new = gdn_mixer(pm, psm, gdn_conv_w[i], alog, dtb, gdn_norm[i],
                                                     s_gdn, s_conv, row0=row0, nb=nb, t=t)
                new["gdn"][grp].append(gdn_new)
                new["gdn_conv"][grp].append(conv_new)
                os_, ls_ = [], []
                for g, (window, dil) in enumerate(DSA_GROUPS):
                    vcol = OD_DV + g * DSA_GW
                    if grp == 0:
                        o_g, l_g = dsa_prompt(qkv[g], g, geom)
                        keep = min(window, tp)
                        kg = jnp.stack([rk[(b + 1) * tp - keep:(b + 1) * tp, g * DSA_GW:(g + 1) * DSA_GW]
                                        for b in range(bp)])
                        vg = jnp.stack([pm[(b + 1) * tp - keep:(b + 1) * tp, vcol:vcol + DSA_GW]
                                        for b in range(bp)])
                    else:
                        o_g, l_g = dsa_sample(rq, rk, pm, caches[g][i], g, geom)
                        keep = ts
                        kg = rk[mp:, g * DSA_GW:(g + 1) * DSA_GW]
                        vg = pm[mp:, vcol:vcol + DSA_GW]
                    rows = jnp.stack([kg.reshape(nb, keep, DSA_HEADS, DSA_HD),
                                      vg.reshape(nb, keep, DSA_HEADS, DSA_HD)], axis=2)
                    new["dsa%d" % g][grp].append(rows)
                    os_.append(o_g)
                    ls_.append(l_g)
                if grp == 0:
                    o_dsa = dsa_merge(os_, ls_, [dil for _, dil in DSA_GROUPS], mp // ROW_TILE)
                else:
                    o_dsa = dsa_merge(os_, ls_, [0] * len(DSA_GROUPS), 1)
                ys.append(jnp.concatenate([o_gdn.reshape(nb * t, GDN_VW), o_dsa], axis=1))
            y_all = jnp.concatenate(ys, axis=0).astype(BF16)
            m = matmul(y_all, odd_w_out[i], tn=512, name="odd_out")
        x, h = rows_update(x, m, mod4, ln_ffn2[l], geom, gate_layer=l, mod_layer=l, coef=1.0, gate_idx=5,
                           shift_idx=6, scale_idx=7, out_x=True, out_h=True)
        act = matmul_swiglu(h, ffn2_w_in, l)
        m = matmul(act, ffn2_w_out, layer=l, tm_cands=(384, 256), tn=512, name="ffn_out")
        if l + 1 < depth:
            x, h = rows_update(x, m, mod4, ln_ffn1[l + 1], geom, gate_layer=l, mod_layer=l + 1, coef=0.5,
                               gate_idx=8, shift_idx=0, scale_idx=1, out_x=True, out_h=True)
        else:
            y = rows_update(x, m, mod4, final_norm, geom, gate_layer=l, coef=0.5, gate_idx=8, out_y=True)[0]

    y_prompt = y[:mp].reshape(bp, tp, d)
    y_sample = y[mp:].reshape(bs, ts, d)
    outs = [y_prompt, y_sample]
    for name in ("gla", "ssd", "ssd_conv", "gdn", "gdn_conv", "dsa0", "dsa1", "dsa2"):
        outs.append(jnp.stack(new[name][0]))
        outs.append(jnp.stack(new[name][1]))
    return tuple(outs)
```

```python
import functools
import math

import numpy as np
import jax
import jax.numpy as jnp
from jax import lax
from jax.experimental import pallas as pl
from jax.experimental.pallas import tpu as pltpu

F32 = jnp.float32
BF16 = jnp.bfloat16

D_MODEL = 2048
D_FF = 5632
N_MOD = 9
EPS = 1e-6
CONV_W = 4
ROPE_THETA = 10000.0
PAST_LEN = 8192

GLA_HEADS, GLA_DK, GLA_DV, GLA_RANK, GLA_TAU = 4, 128, 256, 16, 16.0
GLA_QK, GLA_VW = GLA_HEADS * GLA_DK, GLA_HEADS * GLA_DV
SSD_HEADS, SSD_P, SSD_STATE, SSD_GROUPS = 32, 64, 128, 4
SSD_HPG = SSD_HEADS // SSD_GROUPS
SSD_INNER = SSD_HEADS * SSD_P
SSD_BC = SSD_GROUPS * SSD_STATE
SSD_CONV_DIM = SSD_INNER + 2 * SSD_BC
GDN_QK_HEADS, GDN_V_HEADS, GDN_DK, GDN_DV = 8, 16, 128, 128
GDN_QK, GDN_VW = GDN_QK_HEADS * GDN_DK, GDN_V_HEADS * GDN_DV
GDN_CONV_DIM = 2 * GDN_QK + GDN_VW
GDN_REP = GDN_V_HEADS // GDN_QK_HEADS
DSA_GROUPS = ((128, 1), (512, 4), (2048, 16))
DSA_HEADS, DSA_HD, DSA_BLOCK = 8, 128, 128
DSA_GW = DSA_HEADS * DSA_HD
DSA_W = len(DSA_GROUPS) * DSA_GW

LANES = 128
SUBLANES = 8
MXU_DIM = 256
VMEM_LIMIT_BYTES = 56 * 1024 * 1024

ROW_TILE = 256
CHUNK = 64
GDN_PACK = MXU_DIM // CHUNK
NEG_BIG = -1e30

EVA_Q, EVA_K, EVA_V, EVA_R = 0, GLA_QK, 2 * GLA_QK, 2 * GLA_QK + GLA_VW
EVA_W = 2 * GLA_QK + 2 * GLA_VW
EV_GLR_COL = EVA_W
EVB_COL = EVA_W + GLA_RANK
EVB_Z, EVB_X, EVB_B, EVB_C = 0, SSD_INNER, 2 * SSD_INNER, 2 * SSD_INNER + SSD_BC
EVB_W = SSD_INNER + SSD_CONV_DIM
EV_DT_COL = EVB_COL + EVB_W
EVEN_IN = EV_DT_COL + SSD_HEADS
SSD_DT_LANE0 = EV_DT_COL % LANES
ODA_Q, ODA_K, ODA_V, ODA_Z = 0, GDN_QK, 2 * GDN_QK, 2 * GDN_QK + GDN_VW
ODA_W = 2 * GDN_QK + 2 * GDN_VW
OD_BA_COL = ODA_W
ODB_COL = ODA_W + 2 * GDN_V_HEADS
ODB_DQ, ODB_DK, ODB_DV = 0, DSA_W, 2 * DSA_W
ODB_W = 3 * DSA_W
ODD_IN = ODB_COL + ODB_W


def _cparams(*sem):
    return pltpu.CompilerParams(dimension_semantics=sem, vmem_limit_bytes=VMEM_LIMIT_BYTES)


def _dot(a, b):
    return jnp.dot(a.astype(BF16), b.astype(BF16), preferred_element_type=F32)


def _dot_nt(a, b):
    return lax.dot_general(a.astype(BF16), b.astype(BF16), (((1,), (1,)), ((), ())), preferred_element_type=F32)


def _dot_tn(a, b):
    return lax.dot_general(a.astype(BF16), b.astype(BF16), (((0,), (0,)), ((), ())), preferred_element_type=F32)


def _split2(x):
    hi = x.astype(BF16)
    lo = (x - hi.astype(F32)).astype(BF16)
    return hi, lo


def _split3(x):
    hi = x.astype(BF16)
    r1 = x - hi.astype(F32)
    mid = r1.astype(BF16)
    lo = (r1 - mid.astype(F32)).astype(BF16)
    return hi, mid, lo


def _dot_exact_rhs(x, e):
    hi, lo = _split2(x)
    eb = e.astype(BF16)
    return (jnp.dot(hi, eb, preferred_element_type=F32) + jnp.dot(lo, eb, preferred_element_type=F32))


def _dot_exact_lhs(e, x):
    eb = e.astype(BF16)
    return sum(jnp.dot(eb, p, preferred_element_type=F32) for p in _split3(x))


def _dot_nt_exact_lhs(e, x):
    eb = e.astype(BF16)
    nt = (((1,), (1,)), ((), ()))
    return sum(lax.dot_general(eb, p, nt, preferred_element_type=F32) for p in _split3(x))


def _cumsum_rows(x):
    n = x.shape[0]
    r = lax.broadcasted_iota(jnp.int32, (n, n), 0)
    c = lax.broadcasted_iota(jnp.int32, (n, n), 1)
    return _dot_exact_lhs(jnp.where(r >= c, 1.0, 0.0), x)


def _rows_of(x, n_rows):
    r = lax.broadcasted_iota(jnp.int32, (n_rows, x.shape[1]), 0)
    c = lax.broadcasted_iota(jnp.int32, (n_rows, x.shape[1]), 1)
    return _dot_nt_exact_lhs(jnp.where(r == c, 1.0, 0.0), x)


def _pad_rows(x, n):
    if x.shape[0] == n:
        return x
    return jnp.concatenate([x, jnp.zeros((n - x.shape[0],) + x.shape[1:], x.dtype)], axis=0)


def _silu(x):
    return x * jax.nn.sigmoid(x)


def _tri_masks(n):
    r = lax.broadcasted_iota(jnp.int32, (n, n), 0)
    c = lax.broadcasted_iota(jnp.int32, (n, n), 1)
    return r >= c, r > c, r == c


def _ada_body(c_ref, w_ref, b_ref, o_ref):
    cond = _silu(c_ref[...])
    o_ref[...] = _dot(cond, w_ref[...]) + b_ref[...]


def ada_mod(c_all, w_ada, b_ada):
    depth, d, n = w_ada.shape
    rows = c_all.shape[0]
    tn = 1024
    return pl.pallas_call(
        _ada_body,
        grid=(depth, n // tn),
        in_specs=[pl.BlockSpec((rows, d), lambda l, j: (0, 0)),
                  pl.BlockSpec((None, d, tn), lambda l, j: (l, 0, j)),
                  pl.BlockSpec((None, 1, tn), lambda l, j: (l, 0, j))],
        out_specs=pl.BlockSpec((None, rows, tn), lambda l, j: (l, 0, j)),
        out_shape=jax.ShapeDtypeStruct((depth, rows, n), F32),
        compiler_params=_cparams("arbitrary", "arbitrary"),
        name="ada_mod",
    )(c_all, w_ada, b_ada.reshape(depth, 1, n))


def _rows_body(*refs, has_m, coef, gate_idx, shift_idx, scale_idx, out_x, out_h, out_y, n_prompt_tiles, ts):
    it = iter(refs)
    x_ref = next(it)
    m_ref = next(it) if has_m else None
    gp_ref, gs_ref = (next(it), next(it)) if has_m else (None, None)
    mp_ref, ms_ref = (next(it), next(it)) if out_h else (None, None)
    gain_ref = next(it)
    xo_ref = next(it) if out_x else None
    h_ref = next(it) if out_h else None
    y_ref = next(it) if out_y else None
    i = pl.program_id(0)
    gain = gain_ref[...]

    def compute(shape3, gate, mod):
        x = x_ref[...]
        if shape3 is not None:
            x = x.reshape(shape3)
        if has_m:
            m = m_ref[...]
            if shape3 is not None:
                m = m.reshape(shape3)
            x = x + coef * gate(gate_idx) * m
        if out_x:
            xo_ref[...] = x.reshape(xo_ref.shape)
        var = jnp.mean(x * x, axis=-1, keepdims=True)
        y = x * lax.rsqrt(var + EPS) * gain
        if out_y:
            y_ref[...] = y.reshape(y_ref.shape)
        if out_h:
            h = y * (1.0 + mod(scale_idx)) + mod(shift_idx)
            h_ref[...] = h.reshape(h_ref.shape).astype(h_ref.dtype)

    @pl.when(i < n_prompt_tiles)
    def _():
        compute(None, lambda k: gp_ref[0, k:k + 1, :], lambda k: mp_ref[0, k:k + 1, :])

    @pl.when(i >= n_prompt_tiles)
    def _():
        rt, d = x_ref.shape
        compute((rt // ts, ts, d), lambda k: gs_ref[:, k:k + 1, :], lambda k: ms_ref[:, k:k + 1, :])


def rows_update(x, m, mod4, gain, geom, *, gate_layer=0, mod_layer=0, coef=0.0, gate_idx=0, shift_idx=0,
                scale_idx=0, out_x=False, out_h=False, out_y=False):
    bp, tp, bs, ts = geom
    mrows, d = x.shape
    n_prompt_tiles = bp * tp // ROW_TILE
    tiles_per_seq = tp // ROW_TILE
    grid = (mrows // ROW_TILE,)
    has_m = m is not None
    row_spec = pl.BlockSpec((ROW_TILE, d), lambda i: (i, 0))

    def mod_specs(layer):
        return [pl.BlockSpec((None, 1, N_MOD, d),
                             lambda i: (layer, bs + jnp.minimum(i // tiles_per_seq, bp - 1), 0, 0)),
                pl.BlockSpec((None, bs, N_MOD, d), lambda i: (layer, 0, 0, 0))]

    in_specs = [row_spec]
    args = [x]
    if has_m:
        in_specs += [row_spec] + mod_specs(gate_layer)
        args += [m, mod4, mod4]
    if out_h:
        in_specs += mod_specs(mod_layer)
        args += [mod4, mod4]
    in_specs.append(pl.BlockSpec((1, d), lambda i: (0, 0)))
    args.append(gain.reshape(1, d))
    out_specs, out_shape = [], []
    if out_x:
        out_specs.append(row_spec)
        out_shape.append(jax.ShapeDtypeStruct((mrows, d), F32))
    if out_h:
        out_specs.append(row_spec)
        out_shape.append(jax.ShapeDtypeStruct((mrows, d), BF16))
    if out_y:
        out_specs.append(row_spec)
        out_shape.append(jax.ShapeDtypeStruct((mrows, d), F32))
    body = functools.partial(_rows_body, has_m=has_m, coef=coef, gate_idx=gate_idx, shift_idx=shift_idx,
                             scale_idx=scale_idx, out_x=out_x, out_h=out_h, out_y=out_y,
                             n_prompt_tiles=n_prompt_tiles, ts=ts)
    return pl.pallas_call(
        body, grid=grid, in_specs=in_specs, out_specs=out_specs, out_shape=out_shape,
        compiler_params=_cparams("arbitrary"), name="rows_update",
    )(*args)


def _mm_body(a_ref, w_ref, o_ref, wb_ref):
    @pl.when(pl.program_id(1) == 0)
    def _():
        wb_ref[...] = w_ref[...].astype(BF16)

    o_ref[...] = jnp.dot(a_ref[...], wb_ref[...], preferred_element_type=F32).astype(o_ref.dtype)


def _mm_swiglu_body(a_ref, wa_ref, wg_ref, o_ref, wb_ref):
    tn = o_ref.shape[1]

    @pl.when(pl.program_id(1) == 0)
    def _():
        wb_ref[:, :tn] = wa_ref[...].astype(BF16)
        wb_ref[:, tn:] = wg_ref[...].astype(BF16)

    u = jnp.dot(a_ref[...], wb_ref[...], preferred_element_type=F32)
    o_ref[...] = (_silu(u[:, :tn]) * u[:, tn:]).astype(o_ref.dtype)


def _pick(n, cands):
    for c in cands:
        if n % c == 0:
            return c
    raise ValueError(f"no tile for {n} in {cands}")


def matmul(a, w, *, layer=None, n_cols=None, tm_cands=(768, 512, 256), tn=512, out_dtype=F32, name="matmul"):
    mrows, k = a.shape
    n = w.shape[-1] if n_cols is None else n_cols
    assert n % tn == 0
    tm = _pick(mrows, tm_cands)
    if w.ndim == 3:
        w_spec = pl.BlockSpec((None, k, tn), lambda j, i: (layer, 0, j))
    else:
        w_spec = pl.BlockSpec((k, tn), lambda j, i: (0, j))
    return pl.pallas_call(
        _mm_body,
        grid=(n // tn, mrows // tm),
        in_specs=[pl.BlockSpec((tm, k), lambda j, i: (i, 0)), w_spec],
        out_specs=pl.BlockSpec((tm, tn), lambda j, i: (i, j)),
        out_shape=jax.ShapeDtypeStruct((mrows, n), out_dtype),
        scratch_shapes=[pltpu.VMEM((k, tn), BF16)],
        compiler_params=_cparams("arbitrary", "arbitrary"),
        name=name,
    )(a, w)


def matmul_swiglu(a, w_in, layer, *, tn=512, tm_cands=(768, 512, 256)):
    mrows, k = a.shape
    f = w_in.shape[-1] // 2
    tm = _pick(mrows, tm_cands)
    nt = f // tn
    return pl.pallas_call(
        _mm_swiglu_body,
        grid=(nt, mrows // tm),
        in_specs=[pl.BlockSpec((tm, k), lambda j, i: (i, 0)),
                  pl.BlockSpec((None, k, tn), lambda j, i: (layer, 0, j)),
                  pl.BlockSpec((None, k, tn), lambda j, i: (layer, 0, j + nt))],
        out_specs=pl.BlockSpec((tm, tn), lambda j, i: (i, j)),
        out_shape=jax.ShapeDtypeStruct((mrows, f), BF16),
        scratch_shapes=[pltpu.VMEM((k, 2 * tn), BF16)],
        compiler_params=_cparams("arbitrary", "arbitrary"),
        name="ffn_in_swiglu",
    )(a, w_in, w_in)


def _mm_shift_body(a_ref, w_ref, wx_ref, o_ref, wb_ref, *, shift):
    @pl.when(pl.program_id(1) == 0)
    def _():
        tn = o_ref.shape[1]
        full = jnp.concatenate([w_ref[...], wx_ref[...]], axis=1)
        wb_ref[...] = full[:, shift:shift + tn].astype(BF16)

    o_ref[...] = jnp.dot(a_ref[...], wb_ref[...], preferred_element_type=F32)


def matmul_shift(a, w, layer, col0, n, *, tn=1024, tm_cands=(768, 512, 256), name="matmul_shift"):
    mrows, k = a.shape
    tm = _pick(mrows, tm_cands)
    shift = col0 % LANES
    base = col0 - shift
    assert base % tn == 0 and n % tn == 0 and shift > 0
    return pl.pallas_call(
        functools.partial(_mm_shift_body, shift=shift),
        grid=(n // tn, mrows // tm),
        in_specs=[pl.BlockSpec((tm, k), lambda j, i: (i, 0)),
                  pl.BlockSpec((None, k, tn), lambda j, i: (layer, 0, base // tn + j)),
                  pl.BlockSpec((None, k, LANES), lambda j, i: (layer, 0, (base + (j + 1) * tn) // LANES))],
        out_specs=pl.BlockSpec((tm, tn), lambda j, i: (i, j)),
        out_shape=jax.ShapeDtypeStruct((mrows, n), F32),
        scratch_shapes=[pltpu.VMEM((k, tn), BF16)],
        compiler_params=_cparams("arbitrary", "arbitrary"),
        name=name,
    )(a, w, w)


def _mm_cols_body(a_ref, w_ref, o_ref, wb_ref, *, n_valid):
    @pl.when(pl.program_id(0) == 0)
    def _():
        w = w_ref[...]
        if n_valid < LANES:
            lane = lax.broadcasted_iota(jnp.int32, w.shape, 1)
            w = jnp.where(lane < n_valid, w, 0.0)
        wb_ref[...] = w.astype(BF16)

    o_ref[...] = jnp.dot(a_ref[...], wb_ref[...], preferred_element_type=F32)


def matmul_cols128(a, w, layer, col0, *, tm_cands=(768, 512, 256), name="matmul_cols128"):
    mrows, k = a.shape
    tm = _pick(mrows, tm_cands)
    cb = col0 // LANES
    n_valid = min(LANES, w.shape[-1] - cb * LANES)
    return pl.pallas_call(
        functools.partial(_mm_cols_body, n_valid=n_valid),
        grid=(mrows // tm,),
        in_specs=[pl.BlockSpec((tm, k), lambda i: (i, 0)),
                  pl.BlockSpec((None, k, LANES), lambda i: (layer, 0, cb))],
        out_specs=pl.BlockSpec((tm, LANES), lambda i: (i, 0)),
        out_shape=jax.ShapeDtypeStruct((mrows, LANES), F32),
        scratch_shapes=[pltpu.VMEM((k, LANES), BF16)],
        compiler_params=_cparams("arbitrary"),
        name=name,
    )(a, w)


def _mm_pair_body(p1_ref, p2_ref, s1_ref, s2_ref, w_ref, o_ref, wb_ref, *, n_prompt_tiles):
    i = pl.program_id(1)
    k1 = p1_ref.shape[1]

    @pl.when(i == 0)
    def _():
        wb_ref[...] = w_ref[...].astype(BF16)

    def go(r1, r2):
        o_ref[...] = (jnp.dot(r1[...].astype(BF16), wb_ref[:k1, :], preferred_element_type=F32)
                      + jnp.dot(r2[...].astype(BF16), wb_ref[k1:, :], preferred_element_type=F32))

    @pl.when(i < n_prompt_tiles)
    def _():
        go(p1_ref, p2_ref)

    @pl.when(i >= n_prompt_tiles)
    def _():
        go(s1_ref, s2_ref)


def matmul_pair(p1, p2, s1, s2, w, *, tn=512, name="matmul_pair"):
    mp, k1 = p1.shape
    k2 = p2.shape[1]
    n = w.shape[-1]
    npt = mp // ROW_TILE
    assert s1.shape[0] == ROW_TILE

    pmap = lambda j, i: (jnp.minimum(i, npt - 1), 0)
    smap = lambda j, i: (0, 0)
    return pl.pallas_call(
        functools.partial(_mm_pair_body, n_prompt_tiles=npt),
        grid=(n // tn, npt + 1),
        in_specs=[pl.BlockSpec((ROW_TILE, k1), pmap), pl.BlockSpec((ROW_TILE, k2), pmap),
                  pl.BlockSpec((ROW_TILE, k1), smap), pl.BlockSpec((ROW_TILE, k2), smap),
                  pl.BlockSpec((k1 + k2, tn), lambda j, i: (0, j))],
        out_specs=pl.BlockSpec((ROW_TILE, tn), lambda j, i: (i, j)),
        out_shape=jax.ShapeDtypeStruct((mp + ROW_TILE, n), F32),
        scratch_shapes=[pltpu.VMEM((k1 + k2, tn), BF16)],
        compiler_params=_cparams("arbitrary", "arbitrary"),
        name=name,
    )(p1, p2, s1, s2, w)


def _conv_chunk(ext_ref, raw, w_ref, cin):
    ext_ref[SUBLANES:SUBLANES + CHUNK, :] = _pad_rows(raw, CHUNK)
    acc = None
    for j in range(CONV_W):
        off = SUBLANES - (CONV_W - 1) + j
        term = ext_ref[off:off + CHUNK, :] * w_ref[j:j + 1, :]
        acc = term if acc is None else acc + term
    return acc


def _conv_init(ext_ref, c0_ref, has_state):
    ext_ref[0:SUBLANES, :] = jnp.zeros((SUBLANES, ext_ref.shape[1]), F32)
    if has_state:
        ext_ref[SUBLANES - (CONV_W - 1):SUBLANES, :] = c0_ref[0]


def _conv_tail(ext_ref, cin):
    return ext_ref[cin + SUBLANES - (CONV_W - 1):cin + SUBLANES, :]


def _conv_advance(ext_ref):
    ext_ref[0:SUBLANES, :] = ext_ref[CHUNK:CHUNK + SUBLANES, :]


def _gla_body(q_ref, k_ref, v_ref, r_ref, g_ref, wg_ref, bg_ref, nrm_ref, s0_ref, o_ref, so_ref, st_ref,
              *, cin, has_state):
    c = pl.program_id(1)
    heads = range(GLA_HEADS)

    @pl.when(c == 0)
    def _():
        for h in heads:
            if has_state:
                st_ref[h] = s0_ref[0, h].T
            else:
                st_ref[h] = jnp.zeros(st_ref.shape[1:], F32)

    q = _pad_rows(q_ref[...], CHUNK) * (GLA_DK ** -0.5)
    k = _pad_rows(k_ref[...], CHUNK)
    v = _pad_rows(v_ref[...], CHUNK)
    glr = _pad_rows(g_ref[...], CHUNK)
    x = _dot(glr, wg_ref[...]) + bg_ref[...]
    logf = jax.nn.log_sigmoid(x) * (1.0 / GLA_TAU)
    if cin < CHUNK:
        rows = lax.broadcasted_iota(jnp.int32, logf.shape, 0)
        logf = jnp.where(rows < cin, logf, 0.0)
    b = _cumsum_rows(logf)
    bmid = b[CHUNK // 2 - 1:CHUNK // 2, :]
    blast = b[CHUNK - 1:CHUNK, :]
    incl, _, _ = _tri_masks(CHUNK)
    qe = q * jnp.exp(b - bmid)
    ke = k * jnp.exp(bmid - b)
    qb = q * jnp.exp(b)
    kw = k * jnp.exp(blast - b)
    sdec = jnp.exp(blast)

    def hk(a, h):
        return a[:, h * GLA_DK:(h + 1) * GLA_DK]

    def hv(a, h):
        return a[:, h * GLA_DV:(h + 1) * GLA_DV]

    sts = [st_ref[h] for h in heads]
    atts = [jnp.where(incl, _dot_nt(hk(qe, h), hk(ke, h)), 0.0) for h in heads]
    os_ = [_dot(atts[h], hv(v, h)) + _dot_nt(hk(qb, h), sts[h]) for h in heads]
    st_news = [sts[h] * hk(sdec, h) + _dot_tn(hv(v, h), hk(kw, h)) for h in heads]
    ys = []
    for h in heads:
        st_ref[h] = st_news[h]
        o = os_[h]
        var = jnp.mean(o * o, axis=-1, keepdims=True)
        ys.append(o * lax.rsqrt(var + EPS) * nrm_ref[...])
    y = jnp.concatenate(ys, axis=1)
    o_ref[0] = y[:cin] * _silu(r_ref[...])

    @pl.when(c == pl.num_programs(1) - 1)
    def _():
        for h in heads:
            so_ref[0, h] = st_news[h].T


def gla_mixer(pm, psm, wg_pad, b_gate, nrm, s0, *, row0, nb, t):
    cin = min(t, CHUNK)
    nc = t // cin
    rb0 = row0 // cin
    has_state = s0 is not None
    if s0 is None:
        s0 = jnp.zeros((1, GLA_HEADS, GLA_DK, GLA_DV), F32)
        s0_map = lambda b, c: (0, 0, 0, 0)
    else:
        s0_map = lambda b, c: (b, 0, 0, 0)

    def rmap(col0, width):
        return lambda b, c: (rb0 + b * nc + c, col0 // width)

    const = lambda b, c: (0, 0)
    body = functools.partial(_gla_body, cin=cin, has_state=has_state)
    return pl.pallas_call(
        body,
        grid=(nb, nc),
        in_specs=[pl.BlockSpec((cin, GLA_QK), rmap(EVA_Q, GLA_QK)),
                  pl.BlockSpec((cin, GLA_QK), rmap(EVA_K, GLA_QK)),
                  pl.BlockSpec((cin, GLA_VW), rmap(EVA_V, GLA_VW)),
                  pl.BlockSpec((cin, GLA_VW), rmap(EVA_R, GLA_VW)),
                  pl.BlockSpec((cin, LANES), lambda b, c: (rb0 + b * nc + c, 0)),
                  pl.BlockSpec((LANES, GLA_QK), const),
                  pl.BlockSpec((1, GLA_QK), const),
                  pl.BlockSpec((1, GLA_DV), const),
                  pl.BlockSpec((1, GLA_HEADS, GLA_DK, GLA_DV), s0_map)],
        out_specs=[pl.BlockSpec((1, cin, GLA_VW), lambda b, c: (b, c, 0)),
                   pl.BlockSpec((1, GLA_HEADS, GLA_DK, GLA_DV), lambda b, c: (b, 0, 0, 0))],
        out_shape=[jax.ShapeDtypeStruct((nb, t, GLA_VW), F32),
                   jax.ShapeDtypeStruct((nb, GLA_HEADS, GLA_DK, GLA_DV), F32)],
        scratch_shapes=[pltpu.VMEM((GLA_HEADS, GLA_DV, GLA_DK), F32)],
        compiler_params=_cparams("arbitrary", "arbitrary"),
        name="gla_mixer",
    )(pm, pm, pm, pm, psm, wg_pad, b_gate.reshape(1, GLA_QK), nrm.reshape(1, GLA_DV), s0)


def _ssd_body(z_ref, x_ref, b_ref, c_ref, sm_ref, cwx_ref, cwb_ref, cwc_ref, cbx_ref, cbb_ref, cbc_ref,
              dtb_ref, alog_ref, dexp_ref, nrm_ref, e_ref, s0_ref, c0x_ref, c0b_ref, c0c_ref,
              y_ref, so_ref, cox_ref, cob_ref, coc_ref,
              st_ref, ex_ref, eb_ref, ec_ref, *, cin, has_state):
    c = pl.program_id(1)

    @pl.when(c == 0)
    def _():
        if has_state:
            st_ref[...] = s0_ref[0].T
        else:
            st_ref[...] = jnp.zeros(st_ref.shape, F32)
        _conv_init(ex_ref, c0x_ref, has_state)
        _conv_init(eb_ref, c0b_ref, has_state)
        _conv_init(ec_ref, c0c_ref, has_state)

    xs = _silu(_conv_chunk(ex_ref, x_ref[...], cwx_ref, cin) + cbx_ref[...])
    bm = _silu(_conv_chunk(eb_ref, b_ref[...], cwb_ref, cin) + cbb_ref[...])
    cm = _silu(_conv_chunk(ec_ref, c_ref[...], cwc_ref, cin) + cbc_ref[...])

    dt = jax.nn.softplus(_pad_rows(sm_ref[...], CHUNK) + dtb_ref[...])
    if cin < CHUNK:
        rows = lax.broadcasted_iota(jnp.int32, dt.shape, 0)
        dt = jnp.where(rows < cin, dt, 0.0)
    a = dt * (-jnp.exp(alog_ref[...]))
    cum = _cumsum_rows(a)
    cum_t = _rows_of(cum, SSD_DT_LANE0 + SSD_HEADS)
    last = cum[CHUNK - 1:CHUNK, :]
    stacked = jnp.concatenate([dt, jnp.exp(cum), jnp.exp(last - cum)], axis=0)
    ex = _dot_exact_rhs(stacked, e_ref[...])
    dt_e, ecum_e, w_e = ex[:CHUNK], ex[CHUNK:2 * CHUNK], ex[2 * CHUNK:]
    xdt = xs * dt_e
    xw = xdt * w_e
    incl, _, _ = _tri_masks(CHUNK)
    lane = lax.broadcasted_iota(jnp.int32, (CHUNK, LANES), 1)
    st = st_ref[...]
    gw = SSD_HPG * SSD_P
    ys = []
    for g in range(SSD_GROUPS):
        bg = bm[:, g * SSD_STATE:(g + 1) * SSD_STATE]
        cg = cm[:, g * SSD_STATE:(g + 1) * SSD_STATE]
        cb = _dot_nt(cg, bg)
        for pr in range(SSD_HPG // 2):
            h0 = g * SSD_HPG + 2 * pr
            atts = []
            for h in (h0, h0 + 1):
                ln = SSD_DT_LANE0 + h
                seg = cum[:, ln:ln + 1] - cum_t[ln:ln + 1, :]
                atts.append(cb * jnp.exp(jnp.where(incl, seg, NEG_BIG)))
            xp = xdt[:, h0 * SSD_P:(h0 + 2) * SSD_P]
            yy = _dot(jnp.concatenate(atts, axis=0), xp)
            ys.append(jnp.where(lane < SSD_P, yy[:CHUNK], yy[CHUNK:]))
        sg = st[:, g * gw:(g + 1) * gw]
        ys_inter = _dot(cg, sg) * ecum_e[:, g * gw:(g + 1) * gw]
        for pr in range(SSD_HPG // 2):
            idx = g * (SSD_HPG // 2) + pr
            ys[idx] = ys[idx] + ys_inter[:, pr * LANES:(pr + 1) * LANES]
        st_ref[:, g * gw:(g + 1) * gw] = sg * ecum_e[CHUNK - 1:CHUNK, g * gw:(g + 1) * gw] + _dot_tn(
            bg, xw[:, g * gw:(g + 1) * gw])
    y = jnp.concatenate(ys, axis=1) + xs * dexp_ref[...]
    y = y[:cin] * _silu(z_ref[...])
    var = jnp.mean(y * y, axis=-1, keepdims=True)
    y_ref[0] = y * lax.rsqrt(var + EPS) * nrm_ref[...]

    @pl.when(c == pl.num_programs(1) - 1)
    def _():
        so_ref[0] = st_ref[...].T
        cox_ref[0] = _conv_tail(ex_ref, cin)
        cob_ref[0] = _conv_tail(eb_ref, cin)
        coc_ref[0] = _conv_tail(ec_ref, cin)

    if cin == CHUNK:
        _conv_advance(ex_ref)
        _conv_advance(eb_ref)
        _conv_advance(ec_ref)


def ssd_mixer(pm, psm, conv_w, conv_b, dtb_pad, alog_pad, d_exp, nrm, expand, s0, c0, *, row0, nb, t):
    cin = min(t, CHUNK)
    nc = t // cin
    rb0 = row0 // cin
    has_state = s0 is not None
    hp = SSD_INNER
    if s0 is None:
        s0 = jnp.zeros((1, hp, SSD_STATE), F32)
        c0 = jnp.zeros((1, CONV_W - 1, SSD_CONV_DIM), F32)
        bmap = lambda b, c: 0
    else:
        s0 = s0.reshape(nb, hp, SSD_STATE)
        bmap = lambda b, c: b

    def rmap(col0, width):
        return lambda b, c: (rb0 + b * nc + c, col0 // width)

    const = lambda b, c: (0, 0)
    kconv = CONV_W - 1
    body = functools.partial(_ssd_body, cin=cin, has_state=has_state)
    outs = pl.pallas_call(
        body,
        grid=(nb, nc),
        in_specs=[pl.BlockSpec((cin, hp), rmap(EVB_Z, hp)),
                  pl.BlockSpec((cin, hp), rmap(EVB_X, hp)),
                  pl.BlockSpec((cin, SSD_BC), rmap(EVB_B, SSD_BC)),
                  pl.BlockSpec((cin, SSD_BC), rmap(EVB_C, SSD_BC)),
                  pl.BlockSpec((cin, LANES), lambda b, c: (rb0 + b * nc + c, 0)),
                  pl.BlockSpec((CONV_W, hp), lambda b, c: (0, 0)),
                  pl.BlockSpec((CONV_W, SSD_BC), lambda b, c: (0, hp // SSD_BC)),
                  pl.BlockSpec((CONV_W, SSD_BC), lambda b, c: (0, hp // SSD_BC + 1)),
                  pl.BlockSpec((1, hp), lambda b, c: (0, 0)),
                  pl.BlockSpec((1, SSD_BC), lambda b, c: (0, hp // SSD_BC)),
                  pl.BlockSpec((1, SSD_BC), lambda b, c: (0, hp // SSD_BC + 1)),
                  pl.BlockSpec((1, LANES), const),
                  pl.BlockSpec((1, LANES), const),
                  pl.BlockSpec((1, hp), const),
                  pl.BlockSpec((1, hp), const),
                  pl.BlockSpec((LANES, hp), const),
                  pl.BlockSpec((1, hp, SSD_STATE), lambda b, c: (bmap(b, c), 0, 0)),
                  pl.BlockSpec((1, kconv, hp), lambda b, c: (bmap(b, c), 0, 0)),
                  pl.BlockSpec((1, kconv, SSD_BC), lambda b, c: (bmap(b, c), 0, hp // SSD_BC)),
                  pl.BlockSpec((1, kconv, SSD_BC), lambda b, c: (bmap(b, c), 0, hp // SSD_BC + 1))],
        out_specs=[pl.BlockSpec((1, cin, hp), lambda b, c: (b, c, 0)),
                   pl.BlockSpec((1, hp, SSD_STATE), lambda b, c: (b, 0, 0)),
                   pl.BlockSpec((1, kconv, hp), lambda b, c: (b, 0, 0)),
                   pl.BlockSpec((1, kconv, SSD_BC), lambda b, c: (b, 0, 0)),
                   pl.BlockSpec((1, kconv, SSD_BC), lambda b, c: (b, 0, 0))],
        out_shape=[jax.ShapeDtypeStruct((nb, t, hp), F32),
                   jax.ShapeDtypeStruct((nb, hp, SSD_STATE), F32),
                   jax.ShapeDtypeStruct((nb, kconv, hp), F32),
                   jax.ShapeDtypeStruct((nb, kconv, SSD_BC), F32),
                   jax.ShapeDtypeStruct((nb, kconv, SSD_BC), F32)],
        scratch_shapes=[pltpu.VMEM((SSD_STATE, hp), F32),
                        pltpu.VMEM((CHUNK + SUBLANES, hp), F32),
                        pltpu.VMEM((CHUNK + SUBLANES, SSD_BC), F32),
                        pltpu.VMEM((CHUNK + SUBLANES, SSD_BC), F32)],
        compiler_params=_cparams("arbitrary", "arbitrary"),
        name="ssd_mixer",
    )(pm, pm, pm, pm, psm, conv_w, conv_w, conv_w, conv_b, conv_b, conv_b, dtb_pad, alog_pad, d_exp,
      nrm.reshape(1, hp), expand, s0, c0, c0, c0)
    y, s_new, cx, cb_, cc = outs
    conv_s = jnp.concatenate([cx, cb_, cc], axis=-1)
    return y, s_new.reshape(nb, SSD_HEADS, SSD_P, SSD_STATE), conv_s


def _l2norm_heads(x, n_heads, width, scale):
    outs = []
    for e in range(n_heads):
        xe = x[:, e * width:(e + 1) * width]
        ss = jnp.sum(xe * xe, axis=-1, keepdims=True)
        outs.append(xe * (lax.rsqrt(ss + EPS) * scale))
    return outs


def _unit_lower_inverse(a_mats, eye):
    def mm(a, b):
        return jnp.dot(a, b, preferred_element_type=F32)

    ts = [eye - a for a in a_mats]
    pbs = [(-a).astype(BF16) for a in a_mats]
    for _ in range(int(math.log2(CHUNK)) - 1):
        ps = [mm(pb, pb) for pb in pbs]
        pbs = [p.astype(BF16) for p in ps]
        ts = [t + mm(t.astype(BF16), pb) for t, pb in zip(ts, pbs)]
    outs = []
    splits = [(_split2(a), _split2(t)) for a, t in zip(a_mats, ts)]
    ats = [mm(ah, th) + mm(ah, tl) + mm(al, th) for (ah, al), (th, tl) in splits]
    for t, at, (_, (th, _)) in zip(ts, ats, splits):
        resid = eye - t - at
        outs.append(t + mm(th, resid.astype(BF16)))
    return outs


def _gdn_body(q_ref, k_ref, v_ref, z_ref, sm_ref, cwq_ref, cwk_ref, cwv_ref, alog_ref, dtb_ref, nrm_ref,
              s0_ref, c0q_ref, c0k_ref, c0v_ref,
              o_ref, so_ref, coq_ref, cok_ref, cov_ref,
              s_ref, eq_ref, ek_ref, ev_ref, *, cin, has_state):
    c = pl.program_id(1)
    n = GDN_PACK * CHUNK

    @pl.when(c == 0)
    def _():
        if has_state:
            s_ref[...] = s0_ref[0]
        else:
            s_ref[...] = jnp.zeros(s_ref.shape, F32)
        _conv_init(eq_ref, c0q_ref, has_state)
        _conv_init(ek_ref, c0k_ref, has_state)
        _conv_init(ev_ref, c0v_ref, has_state)

    qc = _silu(_conv_chunk(eq_ref, q_ref[...], cwq_ref, cin))
    kc = _silu(_conv_chunk(ek_ref, k_ref[...], cwk_ref, cin))
    vc = _silu(_conv_chunk(ev_ref, v_ref[...], cwv_ref, cin))
    qs = _l2norm_heads(qc, GDN_QK_HEADS, GDN_DK, GDN_DK ** -0.5)
    ks = _l2norm_heads(kc, GDN_QK_HEADS, GDN_DK, 1.0)

    sm = _pad_rows(sm_ref[...], CHUNK)
    beta = jax.nn.sigmoid(sm)
    gl = -jnp.exp(alog_ref[...]) * jax.nn.softplus(sm + dtb_ref[...])
    if cin < CHUNK:
        rows = lax.broadcasted_iota(jnp.int32, sm.shape, 0)
        beta = jnp.where(rows < cin, beta, 0.0)
        gl = jnp.where(rows < cin, gl, 0.0)
    cum = _cumsum_rows(gl)
    last_row = cum[CHUNK - 1:CHUNK, :]

    rr = lax.broadcasted_iota(jnp.int32, (n, n), 0)
    cc = lax.broadcasted_iota(jnp.int32, (n, n), 1)
    same = (rr // CHUNK) == (cc // CHUNK)
    incl = same & (rr >= cc)
    strict = same & (rr > cc)
    eye = jnp.where(rr == cc, 1.0, 0.0)
    head_of_row = lax.broadcasted_iota(jnp.int32, (n, LANES), 0) // CHUNK
    lane = lax.broadcasted_iota(jnp.int32, (n, LANES), 1)
    bd_mask = (lax.broadcasted_iota(jnp.int32, (n, GDN_PACK * GDN_DK), 0) // CHUNK
               == lax.broadcasted_iota(jnp.int32, (n, GDN_PACK * GDN_DK), 1) // GDN_DK)
    srow = lax.broadcasted_iota(jnp.int32, (GDN_PACK * GDN_DK, LANES), 0) // GDN_DK
    slane = lax.broadcasted_iota(jnp.int32, (GDN_PACK * GDN_DK, LANES), 1)

    def block_diag(x):
        return jnp.where(bd_mask, jnp.concatenate([x] * GDN_PACK, axis=1), 0.0)

    def stack(parts):
        return jnp.concatenate(parts, axis=0)

    cum_rep = stack([cum] * GDN_PACK)
    beta_rep = stack([beta] * GDN_PACK)
    last_rep = jnp.broadcast_to(last_row, (n, LANES))
    cum_t = _rows_of(cum, 2 * GDN_V_HEADS)
    packs = range(GDN_V_HEADS // GDN_PACK)

    pre = []
    for j in packs:
        h0 = j * GDN_PACK
        heads = range(h0, h0 + GDN_PACK)
        k_st = stack([ks[i // GDN_REP] for i in heads])
        q_st = stack([qs[i // GDN_REP] for i in heads])
        v_st = stack([vc[:, i * GDN_DV:(i + 1) * GDN_DV] for i in heads])
        g_sel = lane == GDN_V_HEADS + h0 + head_of_row
        col = jnp.sum(jnp.where(g_sel, cum_rep, 0.0), axis=1, keepdims=True)
        rowv = jnp.concatenate([cum_t[GDN_V_HEADS + i:GDN_V_HEADS + i + 1, :] for i in heads], axis=1)
        bcol = jnp.sum(jnp.where(lane == h0 + head_of_row, beta_rep, 0.0), axis=1, keepdims=True)
        lastc = jnp.sum(jnp.where(g_sel, last_rep, 0.0), axis=1, keepdims=True)
        dec = jnp.exp(jnp.where(incl, col - rowv, NEG_BIG))
        ecol = jnp.exp(col)
        s_st = s_ref[h0:h0 + GDN_PACK].reshape(GDN_PACK * GDN_DK, GDN_DV)
        pre.append(dict(k=k_st, q=q_st, v=v_st, col=col, bcol=bcol, lastc=lastc, dec=dec, ecol=ecol, s=s_st))
    kks = [_dot_nt(p["k"], p["k"]) for p in pre]
    qks = [_dot_nt(p["q"], p["k"]) for p in pre]
    a_mats = [jnp.where(strict, p["bcol"] * kk * p["dec"], 0.0) for p, kk in zip(pre, kks)]
    t_invs = _unit_lower_inverse(a_mats, eye)
    rhss = [jnp.concatenate([p["v"] * p["bcol"], p["k"] * (p["bcol"] * p["ecol"])], axis=1) for p in pre]
    sols = [_dot(t, r) for t, r in zip(t_invs, rhss)]
    deltas = [sol[:, :GDN_DV] - _dot(block_diag(sol[:, GDN_DV:]), p["s"]) for sol, p in zip(sols, pre)]
    o_sts = [_dot(block_diag(p["q"] * p["ecol"]), p["s"]) + _dot(jnp.where(incl, qk * p["dec"], 0.0), dl)
             for p, qk, dl in zip(pre, qks, deltas)]
    outs = []
    for j, (p, dl, o_st) in enumerate(zip(pre, deltas, o_sts)):
        h0 = j * GDN_PACK
        kw_bd = block_diag(p["k"] * jnp.exp(p["lastc"] - p["col"]))
        s_dec = jnp.exp(jnp.sum(jnp.where(slane == GDN_V_HEADS + h0 + srow,
                                          jnp.broadcast_to(last_row, srow.shape), 0.0), axis=1, keepdims=True))
        s_new = p["s"] * s_dec + _dot_tn(kw_bd, dl)
        s_ref[h0:h0 + GDN_PACK] = s_new.reshape(GDN_PACK, GDN_DK, GDN_DV)
        var = jnp.mean(o_st * o_st, axis=-1, keepdims=True)
        y_st = o_st * lax.rsqrt(var + EPS) * nrm_ref[...]
        outs += [y_st[i * CHUNK:(i + 1) * CHUNK] for i in range(GDN_PACK)]
    o = jnp.concatenate(outs, axis=1)
    o_ref[0] = o[:cin] * _silu(z_ref[...])

    @pl.when(c == pl.num_programs(1) - 1)
    def _():
        so_ref[0] = s_ref[...]
        coq_ref[0] = _conv_tail(eq_ref, cin)
        cok_ref[0] = _conv_tail(ek_ref, cin)
        cov_ref[0] = _conv_tail(ev_ref, cin)

    if cin == CHUNK:
        _conv_advance(eq_ref)
        _conv_advance(ek_ref)
        _conv_advance(ev_ref)


def gdn_mixer(pm, psm, conv_w, alog_pad, dtb_pad, nrm, s0, c0, *, row0, nb, t):
    cin = min(t, CHUNK)
    nc = t // cin
    rb0 = row0 // cin
    has_state = s0 is not None
    kconv = CONV_W - 1
    if s0 is None:
        s0 = jnp.zeros((1, GDN_V_HEADS, GDN_DK, GDN_DV), F32)
        c0 = jnp.zeros((1, kconv, GDN_CONV_DIM), F32)
        bmap = lambda b: 0
    else:
        bmap = lambda b: b

    def rmap(col0, width):
        return lambda b, c: (rb0 + b * nc + c, col0 // width)

    const = lambda b, c: (0, 0)
    body = functools.partial(_gdn_body, cin=cin, has_state=has_state)
    outs = pl.pallas_call(
        body,
        grid=(nb, nc),
        in_specs=[pl.BlockSpec((cin, GDN_QK), rmap(ODA_Q, GDN_QK)),
                  pl.BlockSpec((cin, GDN_QK), rmap(ODA_K, GDN_QK)),
                  pl.BlockSpec((cin, GDN_VW), rmap(ODA_V, GDN_VW)),
                  pl.BlockSpec((cin, GDN_VW), rmap(ODA_Z, GDN_VW)),
                  pl.BlockSpec((cin, LANES), lambda b, c: (rb0 + b * nc + c, 0)),
                  pl.BlockSpec((CONV_W, GDN_QK), lambda b, c: (0, 0)),
                  pl.BlockSpec((CONV_W, GDN_QK), lambda b, c: (0, 1)),
                  pl.BlockSpec((CONV_W, GDN_VW), lambda b, c: (0, 1)),
                  pl.BlockSpec((1, LANES), const),
                  pl.BlockSpec((1, LANES), const),
                  pl.BlockSpec((1, GDN_DV), const),
                  pl.BlockSpec((1, GDN_V_HEADS, GDN_DK, GDN_DV), lambda b, c: (bmap(b), 0, 0, 0)),
                  pl.BlockSpec((1, kconv, GDN_QK), lambda b, c: (bmap(b), 0, 0)),
                  pl.BlockSpec((1, kconv, GDN_QK), lambda b, c: (bmap(b), 0, 1)),
                  pl.BlockSpec((1, kconv, GDN_VW), lambda b, c: (bmap(b), 0, 1))],
        out_specs=[pl.BlockSpec((1, cin, GDN_VW), lambda b, c: (b, c, 0)),
                   pl.BlockSpec((1, GDN_V_HEADS, GDN_DK, GDN_DV), lambda b, c: (b, 0, 0, 0)),
                   pl.BlockSpec((1, kconv, GDN_QK), lambda b, c: (b, 0, 0)),
                   pl.BlockSpec((1, kconv, GDN_QK), lambda b, c: (b, 0, 0)),
                   pl.BlockSpec((1, kconv, GDN_VW), lambda b, c: (b, 0, 0))],
        out_shape=[jax.ShapeDtypeStruct((nb, t, GDN_VW), F32),
                   jax.ShapeDtypeStruct((nb, GDN_V_HEADS, GDN_DK, GDN_DV), F32),
                   jax.ShapeDtypeStruct((nb, kconv, GDN_QK), F32),
                   jax.ShapeDtypeStruct((nb, kconv, GDN_QK), F32),
                   jax.ShapeDtypeStruct((nb, kconv, GDN_VW), F32)],
        scratch_shapes=[pltpu.VMEM((GDN_V_HEADS, GDN_DK, GDN_DV), F32),
                        pltpu.VMEM((CHUNK + SUBLANES, GDN_QK), F32),
                        pltpu.VMEM((CHUNK + SUBLANES, GDN_QK), F32),
                        pltpu.VMEM((CHUNK + SUBLANES, GDN_VW), F32)],
        compiler_params=_cparams("arbitrary", "arbitrary"),
        name="gdn_mixer",
    )(pm, pm, pm, pm, psm, conv_w, conv_w, conv_w, alog_pad, dtb_pad, nrm.reshape(1, GDN_DV),
      s0, c0, c0, c0)
    o, s_new, cq, ck, cv = outs
    return o, s_new, jnp.concatenate([cq, ck, cv], axis=-1)


def _residue_perm(dil, inverse=False):
    n = ROW_TILE // dil
    i = lax.broadcasted_iota(jnp.int32, (ROW_TILE, ROW_TILE), 0)
    j = lax.broadcasted_iota(jnp.int32, (ROW_TILE, ROW_TILE), 1)
    if inverse:
        i, j = j, i
    return jnp.where(j == (i % n) * dil + i // n, 1.0, 0.0).astype(BF16)


def _rope_body(q_ref, k_ref, v_ref, inv_ref, qo_ref, ko_ref, *rest, n_prompt_tiles, tiles_per_seq, ts):
    i = pl.program_id(0)
    rt = q_ref.shape[0]
    r = lax.broadcasted_iota(jnp.int32, (rt, LANES), 0)
    pos_prompt = (i % tiles_per_seq) * rt + r
    pos_sample = PAST_LEN + r % ts
    pos = jnp.where(i < n_prompt_tiles, pos_prompt, pos_sample).astype(F32)
    ang = pos * inv_ref[...]
    cos = jnp.cos(ang)
    sin = jnp.sin(ang)
    lane = lax.broadcasted_iota(jnp.int32, (rt, LANES), 1)
    sin = jnp.where(lane < LANES // 2, -sin, sin)
    for src, dst in ((q_ref, qo_ref), (k_ref, ko_ref)):
        for h in range(DSA_W // DSA_HD):
            x = src[:, h * DSA_HD:(h + 1) * DSA_HD]
            dst[:, h * DSA_HD:(h + 1) * DSA_HD] = x * cos + pltpu.roll(x, DSA_HD // 2, 1) * sin
    for g, (_, dil) in enumerate(DSA_GROUPS):
        perm = None if dil == 1 else _residue_perm(dil)
        for src, dst in zip((qo_ref, ko_ref, v_ref), rest[3 * g:3 * g + 3]):
            xb = src[:, g * DSA_GW:(g + 1) * DSA_GW].astype(BF16)
            if perm is not None:
                xb = jnp.dot(perm, xb, preferred_element_type=F32).astype(BF16)
            dst[...] = xb.reshape(dst.shape)


def rope_qk(pm, geom):
    bp, tp, bs, ts = geom
    mrows = pm.shape[0]
    half = DSA_HD // 2
    inv = ROPE_THETA ** (-jnp.arange(half, dtype=F32) / half)
    inv2 = jnp.concatenate([inv, inv]).reshape(1, DSA_HD)
    body = functools.partial(_rope_body, n_prompt_tiles=bp * tp // ROW_TILE, tiles_per_seq=tp // ROW_TILE, ts=ts)
    spec_o = pl.BlockSpec((ROW_TILE, DSA_W), lambda i: (i, 0))
    out_specs = [spec_o, spec_o]
    out_shape = [jax.ShapeDtypeStruct((mrows, DSA_W), F32)] * 2
    for _, dil in DSA_GROUPS:
        n = ROW_TILE // dil
        out_specs += [pl.BlockSpec((dil, n, DSA_GW), lambda i: (0, i, 0))] * 3
        out_shape += [jax.ShapeDtypeStruct((dil, mrows // dil, DSA_GW), BF16)] * 3
    outs = pl.pallas_call(
        body,
        grid=(mrows // ROW_TILE,),
        in_specs=[pl.BlockSpec((ROW_TILE, DSA_W), lambda i: (i, ODB_DQ // DSA_W)),
                  pl.BlockSpec((ROW_TILE, DSA_W), lambda i: (i, ODB_DK // DSA_W)),
                  pl.BlockSpec((ROW_TILE, DSA_W), lambda i: (i, ODB_DV // DSA_W)),
                  pl.BlockSpec((1, DSA_HD), lambda i: (0, 0))],
        out_specs=out_specs,
        out_shape=out_shape,
        compiler_params=_cparams("arbitrary"),
        name="rope_qk",
    )(pm, pm, pm, inv2)
    return outs[0], outs[1], [tuple(outs[2 + 3 * g:5 + 3 * g]) for g in range(len(DSA_GROUPS))]


def _lse_lanes(lses):
    rows = lses[0].shape[0]
    lane = lax.broadcasted_iota(jnp.int32, (rows, LANES), 1)
    out = jnp.zeros((rows, LANES), F32)
    for h, v in enumerate(lses):
        out = jnp.where(lane == h, v, out)
    return out


def _dsa_prompt_body(q_ref, kp_ref, kc_ref, vp_ref, vc_ref, o_ref, l_ref, *, jmax):
    n = pl.program_id(2)
    blk = DSA_BLOCK
    i_idx = lax.broadcasted_iota(jnp.int32, (blk, 2 * blk), 0)
    m_idx = lax.broadcasted_iota(jnp.int32, (blk, 2 * blk), 1)
    dist = i_idx + blk - m_idx
    valid = (dist >= 0) & (dist <= jmax) & ((m_idx >= blk) | (n > 0))
    scale = DSA_HD ** -0.5
    lses = []
    for h in range(DSA_HEADS):
        sl = slice(h * DSA_HD, (h + 1) * DSA_HD)
        kk = jnp.concatenate([kp_ref[:, sl], kc_ref[:, sl]], axis=0)
        vv = jnp.concatenate([vp_ref[:, sl], vc_ref[:, sl]], axis=0)
        s = _dot_nt(q_ref[:, sl], kk) * scale
        s = jnp.where(valid, s, NEG_BIG)
        mx = jnp.max(s, axis=-1, keepdims=True)
        p = jnp.exp(s - mx)
        den = jnp.sum(p, axis=-1, keepdims=True)
        o_ref[:, sl] = (_dot(p, vv) / den).astype(o_ref.dtype)
        lses.append(mx + jnp.log(den))
    l_ref[...] = _lse_lanes(lses)


def dsa_prompt(qkv, g, geom):
    bp, tp, _, _ = geom
    window, dil = DSA_GROUPS[g]
    ls = tp // dil
    nblk = ls // DSA_BLOCK
    q, k, v = qkv
    mu = q.shape[1]

    cur = lambda b, r, n: (r, b * nblk + n, 0)
    prev = lambda b, r, n: (r, b * nblk + jnp.maximum(n - 1, 0), 0)
    blk = (None, DSA_BLOCK, DSA_GW)
    return pl.pallas_call(
        functools.partial(_dsa_prompt_body, jmax=window // dil),
        grid=(bp, dil, nblk),
        in_specs=[pl.BlockSpec(blk, cur), pl.BlockSpec(blk, prev), pl.BlockSpec(blk, cur),
                  pl.BlockSpec(blk, prev), pl.BlockSpec(blk, cur)],
        out_specs=[pl.BlockSpec(blk, cur), pl.BlockSpec((None, DSA_BLOCK, LANES), cur)],
        out_shape=[jax.ShapeDtypeStruct((dil, mu, DSA_GW), BF16),
                   jax.ShapeDtypeStruct((dil, mu, LANES), F32)],
        compiler_params=_cparams("arbitrary", "arbitrary", "arbitrary"),
        name=f"dsa_prompt_w{window}",
    )(q, k, k, v, v)


def _dsa_sample_body(q_ref, kn_ref, vn_ref, cache_ref, o_ref, l_ref, m_scr, d_scr, acc_scr, *, window, dil):
    j = pl.program_id(1)
    nj = pl.num_programs(1)
    ts = q_ref.shape[0]
    nh = DSA_HEADS
    scale = DSA_HD ** -0.5

    def heads_to_rows(ref):
        return jnp.concatenate([ref[:, h * DSA_HD:(h + 1) * DSA_HD] for h in range(nh)], axis=0)

    @pl.when(j == 0)
    def _():
        m_scr[...] = jnp.full(m_scr.shape, NEG_BIG, F32)
        d_scr[...] = jnp.zeros(d_scr.shape, F32)
        acc_scr[...] = jnp.zeros(acc_scr.shape, F32)

    def update(s, vals, ok):
        s = jnp.where(ok, s, NEG_BIG)
        m_old = m_scr[...]
        m_new = jnp.maximum(m_old, jnp.max(s, axis=-1, keepdims=True))
        alpha = jnp.exp(m_old - m_new)
        p = jnp.where(ok, jnp.exp(s - m_new[:, 0:1]), 0.0)
        d_scr[...] = alpha * d_scr[...] + jnp.sum(p, axis=-1, keepdims=True)
        acc_scr[...] = alpha * acc_scr[...] + _dot(p, vals)
        m_scr[...] = m_new

    q_rows = heads_to_rows(q_ref) * scale
    ub, n_res = cache_ref.shape[1], cache_ref.shape[2] // (2 * nh)
    lb = ub * n_res
    x4 = cache_ref[0].reshape(lb, 2, nh, DSA_HD)
    k_all = x4[:, 0].reshape(lb * nh, DSA_HD)
    v_all = x4[:, 1].reshape(lb * nh, DSA_HD)
    r = lax.broadcasted_iota(jnp.int32, (nh * ts, lb * nh), 0)
    c = lax.broadcasted_iota(jnp.int32, (nh * ts, lb * nh), 1)
    pos = c // nh
    dist = window + r % ts - ((j * ub + pos // n_res) * dil + pos % n_res)
    ok = (r // ts == c % nh) & ((dist & (dil - 1)) == 0) & (dist <= window)
    update(_dot_nt(q_rows, k_all), v_all, ok)

    @pl.when(j == nj - 1)
    def _():
        r2 = lax.broadcasted_iota(jnp.int32, (nh * ts, nh * ts), 0)
        c2 = lax.broadcasted_iota(jnp.int32, (nh * ts, nh * ts), 1)
        d2 = r2 % ts - c2 % ts
        ok2 = (r2 // ts == c2 // ts) & (d2 >= 0) & ((d2 & (dil - 1)) == 0)
        update(_dot_nt(q_rows, heads_to_rows(kn_ref)), heads_to_rows(vn_ref), ok2)
        o_rows = acc_scr[...] / d_scr[...]
        lse = m_scr[...] + jnp.log(d_scr[...])
        for h in range(nh):
            o_ref[:, h * DSA_HD:(h + 1) * DSA_HD] = o_rows[h * ts:(h + 1) * ts]
        l_ref[...] = _lse_lanes([lse[h * ts:(h + 1) * ts, 0:1] for h in range(nh)])


def dsa_sample(rq, rk, pm, cache, g, geom):
    bp, tp, bs, ts = geom
    window, dil = DSA_GROUPS[g]
    lcache = cache.shape[1]
    assert lcache == window and window % dil == 0, "cache must hold exactly the window"
    n_res = min(dil, ts)
    strides = lcache // dil
    ub = min(strides, 512 // n_res)
    rb0 = bp * tp // ts
    kv_rows = 2 * DSA_HEADS
    cache4 = cache.reshape(bs, strides, dil * kv_rows, DSA_HD)
    blk = (ts, DSA_GW)
    body = functools.partial(_dsa_sample_body, window=window, dil=dil)
    return pl.pallas_call(
        body,
        grid=(bs, strides // ub),
        in_specs=[pl.BlockSpec(blk, lambda b, j: (rb0 + b, g)),
                  pl.BlockSpec(blk, lambda b, j: (rb0 + b, g)),
                  pl.BlockSpec(blk, lambda b, j: (rb0 + b, ODB_DV // DSA_GW + g)),
                  pl.BlockSpec((1, ub, n_res * kv_rows, DSA_HD), lambda b, j: (b, j, 0, 0))],
        out_specs=[pl.BlockSpec(blk, lambda b, j: (b, 0)), pl.BlockSpec((ts, LANES), lambda b, j: (b, 0))],
        out_shape=[jax.ShapeDtypeStruct((bs * ts, DSA_GW), F32),
                   jax.ShapeDtypeStruct((bs * ts, LANES), F32)],
        scratch_shapes=[pltpu.VMEM((DSA_HEADS * ts, DSA_HD), F32)] * 3,
        compiler_params=_cparams("arbitrary", "arbitrary"),
        name=f"dsa_sample_w{window}",
    )(rq, rk, pm, cache4)


def _dsa_merge_body(o0, o1, o2, l0, l1, l2, e_ref, out_ref, *, dils):
    os_, ls_ = [], []
    for o_ref, l_ref, dil in zip((o0, o1, o2), (l0, l1, l2), dils):
        o = o_ref[...].reshape(ROW_TILE, DSA_GW)
        l = l_ref[...].reshape(ROW_TILE, LANES)
        if dil > 1:
            pinv = _residue_perm(dil, inverse=True)
            o = jnp.dot(pinv, o.astype(BF16), preferred_element_type=F32)
            l = _dot_exact_lhs(pinv, l)
        os_.append(o.astype(F32))
        ls_.append(l)
    mx = jnp.maximum(jnp.maximum(ls_[0], ls_[1]), ls_[2])
    ws = [jnp.exp(l - mx) for l in ls_]
    tot = ws[0] + ws[1] + ws[2]
    acc = None
    for w, o in zip(ws, os_):
        term = _dot_exact_rhs(w / tot, e_ref[...]) * o
        acc = term if acc is None else acc + term
    out_ref[...] = acc


def dsa_merge(os_, ls_, dils, n_tiles):
    in_specs = []
    for width in (DSA_GW, LANES):
        for dil in dils:
            if dil:
                in_specs.append(pl.BlockSpec((dil, ROW_TILE // dil, width), lambda i: (0, i, 0)))
            else:
                in_specs.append(pl.BlockSpec((ROW_TILE, width), lambda i: (i, 0)))
    in_specs.append(pl.BlockSpec((LANES, DSA_GW), lambda i: (0, 0)))
    expand = jnp.asarray((np.arange(LANES)[:, None] == (np.arange(DSA_GW)[None, :] // DSA_HD)).astype(np.float32))
    return pl.pallas_call(
        functools.partial(_dsa_merge_body, dils=tuple(max(d, 1) for d in dils)),
        grid=(n_tiles,),
        in_specs=in_specs,
        out_specs=pl.BlockSpec((ROW_TILE, DSA_GW), lambda i: (i, 0)),
        out_shape=jax.ShapeDtypeStruct((n_tiles * ROW_TILE, DSA_GW), F32),
        compiler_params=_cparams("arbitrary"),
        name="dsa_merge",
    )(*os_, *ls_, expand)


def _pad_lanes(vec, offset=0):
    return jnp.zeros((1, LANES), F32).at[0, offset:offset + vec.shape[0]].set(vec)


def kernel(x_prompt, x_sample, c_prompt, c_sample, state_gla, state_ssd, state_ssd_conv, state_gdn, state_gdn_conv, cache_dsa_w128, cache_dsa_w512, cache_dsa_w2048, ln_ffn1, ln_mix, ln_ffn2, w_ada, b_ada, ffn1_w_in, ffn1_w_out, ffn2_w_in, ffn2_w_out, even_w_in, gla_w_gate2, gla_b_gate, gla_norm, ssd_conv_w, ssd_conv_b, ssd_dt_bias, ssd_A_log, ssd_D, ssd_norm, even_w_out, odd_w_in, gdn_conv_w, gdn_A_log, gdn_dt_bias, gdn_norm, odd_w_out, final_norm):
    bp, tp, d = x_prompt.shape
    bs, ts, _ = x_sample.shape
    depth = w_ada.shape[0]
    geom = (bp, tp, bs, ts)
    assert tp % ROW_TILE == 0 and bs * ts == ROW_TILE and ts == SUBLANES
    mp = bp * tp
    caches = (cache_dsa_w128, cache_dsa_w512, cache_dsa_w2048)

    x = jnp.concatenate([x_prompt.reshape(mp, d), x_sample.reshape(bs * ts, d)], axis=0)
    n_pad = (-(bs + bp)) % SUBLANES
    c_all = jnp.concatenate([c_sample, c_prompt, jnp.zeros((n_pad, d), F32)], axis=0)
    mod4 = ada_mod(c_all, w_ada, b_ada).reshape(depth, c_all.shape[0], N_MOD, d)
    expand = jnp.asarray((np.arange(LANES)[:, None] - SSD_DT_LANE0
                          == (np.arange(SSD_INNER)[None, :] // SSD_P)).astype(np.float32))

    new ={k: ([], []) for k in ("gla", "ssd", "ssd_conv", "gdn", "gdn_conv", "dsa0", "dsa1", "dsa2")}

    h = rows_update(x, None, mod4, ln_ffn1[0], geom, mod_layer=0, shift_idx=0, scale_idx=1, out_h=True)[0]
    for l in range(depth):
        act = matmul_swiglu(h, ffn1_w_in, l)
        m = matmul(act, ffn1_w_out, layer=l, tm_cands=(384, 256), tn=512, name="ffn_out")
        x, h = rows_update(x, m, mod4, ln_mix[l], geom, gate_layer=l, mod_layer=l, coef=0.5, gate_idx=2,
                           shift_idx=3, scale_idx=4, out_x=True, out_h=True)
        i = l // 2
        if l % 2 == 0:
            assert even_w_in.shape[-1] == EVEN_IN
            pma = matmul(h, even_w_in, layer=i, n_cols=EVA_W, tn=1024, name="even_in_a")
            pmb = matmul_shift(h, even_w_in, i, EVB_COL, EVB_W, name="even_in_b")
            p_glr = matmul_cols128(h, even_w_in, i, EV_GLR_COL, name="even_in_glr")
            p_dt = matmul_cols128(h, even_w_in, i, EV_DT_COL, name="even_in_dt")
            wg_pad = jnp.zeros((LANES, GLA_QK), F32).at[:GLA_RANK].set(gla_w_gate2[i])
            dtb = _pad_lanes(ssd_dt_bias[i], SSD_DT_LANE0)
            alog = _pad_lanes(ssd_A_log[i], SSD_DT_LANE0)
            d_exp = jnp.repeat(ssd_D[i], SSD_P).reshape(1, SSD_INNER)
            mix = []
            for grp, (row0, nb, t, s_gla, s_ssd, s_conv) in enumerate((
                    (0, bp, tp, None, None, None),
                    (mp, bs, ts, state_gla[i], state_ssd[i], state_ssd_conv[i]))):
                o_gla, gla_new = gla_mixer(pma, p_glr, wg_pad, gla_b_gate[i], gla_norm[i], s_gla,
                                           row0=row0, nb=nb, t=t)
                y_ssd, ssd_new, conv_new = ssd_mixer(pmb, p_dt, ssd_conv_w[i], ssd_conv_b[i].reshape(1, -1), dtb,
                                                     alog, d_exp, ssd_norm[i], expand, s_ssd, s_conv,
                                                     row0=row0, nb=nb, t=t)
                new["gla"][grp].append(gla_new)
                new["ssd"][grp].append(ssd_new)
                new["ssd_conv"][grp].append(conv_new)
                mix += [o_gla.reshape(nb * t, GLA_VW), y_ssd.reshape(nb * t, SSD_INNER)]
            m = matmul_pair(*mix, even_w_out[i], tn=1024, name="even_out")
        else:
            assert odd_w_in.shape[-1] == ODD_IN
            pma = matmul(h, odd_w_in, layer=i, n_cols=ODA_W, tn=1024, name="odd_in_a")
            pmb = matmul_shift(h, odd_w_in, i, ODB_COL, ODB_W, name="odd_in_b")
            p_ba = matmul_cols128(h, odd_w_in, i, OD_BA_COL, name="odd_in_ba")
            rq, rk, qkv = rope_qk(pmb, geom)
            alog = _pad_lanes(gdn_A_log[i], GDN_V_HEADS)
            dtb = _pad_lanes(gdn_dt_bias[i], GDN_V_HEADS)
            mix = []
            for grp, (row0, nb, t, s_gdn, s_conv) in enumerate((
                    (0, bp, tp, None, None), (mp, bs, ts, state_gdn[i], state_gdn_conv[i]))):
                o_gdn, gdn_new, conv_new = gdn_mixer(pma, p_ba, gdn_conv_w[i], alog, dtb, gdn_norm[i],
                                                     s_gdn, s_conv, row0=row0, nb=nb, t=t)
                new["gdn"][grp].append(gdn_new)
                new["gdn_conv"][grp].append(conv_new)
                os_, ls_ = [], []
                for g, (window, dil) in enumerate(DSA_GROUPS):
                    vcol = ODB_DV + g * DSA_GW
                    if grp == 0:
                        o_g, l_g = dsa_prompt(qkv[g], g, geom)
                        keep = min(window, tp)
                        kg = jnp.stack([rk[(b + 1) * tp - keep:(b + 1) * tp, g * DSA_GW:(g + 1) * DSA_GW]
                                        for b in range(bp)])
                        vg = jnp.stack([pmb[(b + 1) * tp - keep:(b + 1) * tp, vcol:vcol + DSA_GW]
                                        for b in range(bp)])
                    else:
                        o_g, l_g = dsa_sample(rq, rk, pmb, caches[g][i], g, geom)
                        keep = ts
                        kg = rk[mp:, g * DSA_GW:(g + 1) * DSA_GW]
                        vg = pmb[mp:, vcol:vcol + DSA_GW]
                    rows = jnp.stack([kg.reshape(nb, keep, DSA_HEADS, DSA_HD),
                                      vg.reshape(nb, keep, DSA_HEADS, DSA_HD)], axis=2)
                    new["dsa%d" % g][grp].append(rows)
                    os_.append(o_g)
                    ls_.append(l_g)
                if grp == 0:
                    o_dsa = dsa_merge(os_, ls_, [dil for _, dil in DSA_GROUPS], mp // ROW_TILE)
                else:
                    o_dsa = dsa_merge(os_, ls_, [0] * len(DSA_GROUPS), 1)
                mix += [o_gdn.reshape(nb * t, GDN_VW), o_dsa]
            m = matmul_pair(*mix, odd_w_out[i], tn=1024, name="odd_out")
        x, h = rows_update(x, m, mod4, ln_ffn2[l], geom, gate_layer=l, mod_layer=l, coef=1.0, gate_idx=5,
                           shift_idx=6, scale_idx=7, out_x=True, out_h=True)
        act = matmul_swiglu(h, ffn2_w_in, l)
        m = matmul(act, ffn2_w_out, layer=l, tm_cands=(384, 256), tn=512, name="ffn_out")
        if l + 1 < depth:
            x, h = rows_update(x, m, mod4, ln_ffn1[l + 1], geom, gate_layer=l, mod_layer=l + 1, coef=0.5,
                               gate_idx=8, shift_idx=0, scale_idx=1, out_x=True, out_h=True)
        else:
            y = rows_update(x, m, mod4, final_norm, geom, gate_layer=l, coef=0.5, gate_idx=8, out_y=True)[0]

    y_prompt = y[:mp].reshape(bp, tp, d)
    y_sample = y[mp:].reshape(bs, ts, d)
    outs = [y_prompt, y_sample]
    for name in ("gla", "ssd", "ssd_conv", "gdn", "gdn_conv", "dsa0", "dsa1", "dsa2"):
        outs.append(jnp.stack(new[name][0]))
        outs.append(jnp.stack(new[name][1]))
    return tuple(outs)
```

```python
import functools
import math

import numpy as np
import jax
import jax.numpy as jnp
from jax import lax
from jax.experimental import pallas as pl
from jax.experimental.pallas import tpu as pltpu

F32 = jnp.float32
BF16 = jnp.bfloat16

D_MODEL = 2048
D_FF = 5632
N_MOD = 9
EPS = 1e-6
CONV_W = 4
ROPE_THETA = 10000.0
PAST_LEN = 8192

GLA_HEADS, GLA_DK, GLA_DV, GLA_RANK, GLA_TAU = 4, 128, 256, 16, 16.0
GLA_QK, GLA_VW = GLA_HEADS * GLA_DK, GLA_HEADS * GLA_DV
SSD_HEADS, SSD_P, SSD_STATE, SSD_GROUPS = 32, 64, 128, 4
SSD_HPG = SSD_HEADS // SSD_GROUPS
SSD_INNER = SSD_HEADS * SSD_P
SSD_BC = SSD_GROUPS * SSD_STATE
SSD_CONV_DIM = SSD_INNER + 2 * SSD_BC
GDN_QK_HEADS, GDN_V_HEADS, GDN_DK, GDN_DV = 8, 16, 128, 128
GDN_QK, GDN_VW = GDN_QK_HEADS * GDN_DK, GDN_V_HEADS * GDN_DV
GDN_CONV_DIM = 2 * GDN_QK + GDN_VW
GDN_REP = GDN_V_HEADS // GDN_QK_HEADS
DSA_GROUPS = ((128, 1), (512, 4), (2048, 16))
DSA_HEADS, DSA_HD, DSA_BLOCK = 8, 128, 128
DSA_GW = DSA_HEADS * DSA_HD
DSA_W = len(DSA_GROUPS) * DSA_GW

LANES = 128
SUBLANES = 8
MXU_DIM = 256
VMEM_LIMIT_BYTES = 56 * 1024 * 1024

ROW_TILE = 256
CHUNK = 64
GDN_PACK = MXU_DIM // CHUNK
NEG_BIG = -1e30

EVA_Q, EVA_K, EVA_V, EVA_R = 0, GLA_QK, 2 * GLA_QK, 2 * GLA_QK + GLA_VW
EVA_W = 2 * GLA_QK + 2 * GLA_VW
EV_GLR_COL = EVA_W
EVB_COL = EVA_W + GLA_RANK
EVB_Z, EVB_X, EVB_B, EVB_C = 0, SSD_INNER, 2 * SSD_INNER, 2 * SSD_INNER + SSD_BC
EVB_W = SSD_INNER + SSD_CONV_DIM
EV_DT_COL = EVB_COL + EVB_W
EVEN_IN = EV_DT_COL + SSD_HEADS
SSD_DT_LANE0 = EV_DT_COL % LANES
ODA_Q, ODA_K, ODA_V, ODA_Z = 0, GDN_QK, 2 * GDN_QK, 2 * GDN_QK + GDN_VW
ODA_W = 2 * GDN_QK + 2 * GDN_VW
OD_BA_COL = ODA_W
ODB_COL = ODA_W + 2 * GDN_V_HEADS
ODB_DQ, ODB_DK, ODB_DV = 0, DSA_W, 2 * DSA_W
ODB_W = 3 * DSA_W
ODD_IN = ODB_COL + ODB_W


def _cparams(*sem):
    return pltpu.CompilerParams(dimension_semantics=sem, vmem_limit_bytes=VMEM_LIMIT_BYTES)


def _dot(a, b):
    return jnp.dot(a.astype(BF16), b.astype(BF16), preferred_element_type=F32)


def _dot_nt(a, b):
    return lax.dot_general(a.astype(BF16), b.astype(BF16), (((1,), (1,)), ((), ())), preferred_element_type=F32)


def _dot_tn(a, b):
    return lax.dot_general(a.astype(BF16), b.astype(BF16), (((0,), (0,)), ((), ())), preferred_element_type=F32)


def _split2(x):
    hi = x.astype(BF16)
    lo = (x - hi.astype(F32)).astype(BF16)
    return hi, lo


def _split3(x):
    hi = x.astype(BF16)
    r1 = x - hi.astype(F32)
    mid = r1.astype(BF16)
    lo = (r1 - mid.astype(F32)).astype(BF16)
    return hi, mid, lo


def _dot_exact_rhs(x, e):
    hi, lo = _split2(x)
    eb = e.astype(BF16)
    return (jnp.dot(hi, eb, preferred_element_type=F32) + jnp.dot(lo, eb, preferred_element_type=F32))


def _dot_exact_lhs(e, x):
    eb = e.astype(BF16)
    return sum(jnp.dot(eb, p, preferred_element_type=F32) for p in _split3(x))


def _dot_nt_exact_lhs(e, x):
    eb = e.astype(BF16)
    nt = (((1,), (1,)), ((), ()))
    return sum(lax.dot_general(eb, p, nt, preferred_element_type=F32) for p in _split3(x))


def _cumsum_rows(x):
    n = x.shape[0]
    r = lax.broadcasted_iota(jnp.int32, (n, n), 0)
    c = lax.broadcasted_iota(jnp.int32, (n, n), 1)
    return _dot_exact_lhs(jnp.where(r >= c, 1.0, 0.0), x)


def _rows_of(x, n_rows):
    r = lax.broadcasted_iota(jnp.int32, (n_rows, x.shape[1]), 0)
    c = lax.broadcasted_iota(jnp.int32, (n_rows, x.shape[1]), 1)
    return _dot_nt_exact_lhs(jnp.where(r == c, 1.0, 0.0), x)


def _pad_rows(x, n):
    if x.shape[0] == n:
        return x
    return jnp.concatenate([x, jnp.zeros((n - x.shape[0],) + x.shape[1:], x.dtype)], axis=0)


def _silu(x):
    return x * jax.nn.sigmoid(x)


def _tri_masks(n):
    r = lax.broadcasted_iota(jnp.int32, (n, n), 0)
    c = lax.broadcasted_iota(jnp.int32, (n, n), 1)
    return r >= c, r > c, r == c


def _ada_body(c_ref, w_ref, b_ref, o_ref):
    cond = _silu(c_ref[...])
    o_ref[...] = _dot(cond, w_ref[...]) + b_ref[...]


def ada_mod(c_all, w_ada, b_ada):
    depth, d, n = w_ada.shape
    rows = c_all.shape[0]
    tn = 1024
    return pl.pallas_call(
        _ada_body,
        grid=(depth, n // tn),
        in_specs=[pl.BlockSpec((rows, d), lambda l, j: (0, 0)),
                  pl.BlockSpec((None, d, tn), lambda l, j: (l, 0, j)),
                  pl.BlockSpec((None, 1, tn), lambda l, j: (l, 0, j))],
        out_specs=pl.BlockSpec((None, rows, tn), lambda l, j: (l, 0, j)),
        out_shape=jax.ShapeDtypeStruct((depth, rows, n), F32),
        compiler_params=_cparams("arbitrary", "arbitrary"),
        name="ada_mod",
    )(c_all, w_ada, b_ada.reshape(depth, 1, n))


def _mod_specs(layer, geom, tiles_axis, col_map, width):
    bp, tp, bs, _ = geom
    tiles_per_seq = tp // ROW_TILE

    def prompt_map(*idx):
        return (layer, bs + jnp.minimum(idx[tiles_axis] // tiles_per_seq, bp - 1), 0, col_map(*idx))

    def sample_map(*idx):
        return (layer, 0, 0, col_map(*idx))

    return [pl.BlockSpec((None, 1, N_MOD, width), prompt_map), pl.BlockSpec((None, bs, N_MOD, width), sample_map)]


def _norm_body(x_ref, mp_ref, ms_ref, gain_ref, o_ref, *, modulate, shift_idx, scale_idx, n_prompt_tiles, ts):
    i = pl.program_id(0)
    gain = gain_ref[...]

    def compute(shape3, mod):
        x = x_ref[...]
        if shape3 is not None:
            x = x.reshape(shape3)
        var = jnp.mean(x * x, axis=-1, keepdims=True)
        y = x * lax.rsqrt(var + EPS) * gain
        if modulate:
            y = y * (1.0 + mod(scale_idx)) + mod(shift_idx)
        o_ref[...] = y.reshape(o_ref.shape).astype(o_ref.dtype)

    @pl.when(i < n_prompt_tiles)
    def _():
        compute(None, lambda k: mp_ref[0, k:k + 1, :])

    @pl.when(i >= n_prompt_tiles)
    def _():
        rt, d = x_ref.shape
        compute((rt // ts, ts, d), lambda k: ms_ref[:, k:k + 1, :])


def rows_norm(x, mod4, gain, geom, *, mod_layer=0, shift_idx=0, scale_idx=0, modulate=True):
    bp, tp, bs, ts = geom
    mrows, d = x.shape
    row_spec = pl.BlockSpec((ROW_TILE, d), lambda i: (i, 0))
    body = functools.partial(_norm_body, modulate=modulate, shift_idx=shift_idx, scale_idx=scale_idx,
                             n_prompt_tiles=bp * tp // ROW_TILE, ts=ts)
    return pl.pallas_call(
        body,
        grid=(mrows // ROW_TILE,),
        in_specs=[row_spec] + _mod_specs(mod_layer, geom, 0, lambda i: 0, d)
        + [pl.BlockSpec((1, d), lambda i: (0, 0))],
        out_specs=row_spec,
        out_shape=jax.ShapeDtypeStruct((mrows, d), BF16 if modulate else F32),
        compiler_params=_cparams("arbitrary"),
        name="rows_norm",
    )(x, mod4, mod4, gain.reshape(1, d))


def _mm_body(a_ref, w_ref, o_ref, wb_ref):
    @pl.when(pl.program_id(1) == 0)
    def _():
        wb_ref[...] = w_ref[...].astype(BF16)

    o_ref[...] = jnp.dot(a_ref[...], wb_ref[...], preferred_element_type=F32).astype(o_ref.dtype)


def _mm_swiglu_body(a_ref, wa_ref, wg_ref, o_ref, wb_ref):
    tn = o_ref.shape[1]

    @pl.when(pl.program_id(1) == 0)
    def _():
        wb_ref[:, :tn] = wa_ref[...].astype(BF16)
        wb_ref[:, tn:] = wg_ref[...].astype(BF16)

    u = jnp.dot(a_ref[...], wb_ref[...], preferred_element_type=F32)
    o_ref[...] = (_silu(u[:, :tn]) * u[:, tn:]).astype(o_ref.dtype)


def _pick(n, cands):
    for c in cands:
        if n % c == 0:
            return c
    raise ValueError(f"no tile for {n} in {cands}")


def matmul(a, w, *, layer=None, n_cols=None, tm_cands=(1056, 768, 512, 256), tn=512, out_dtype=F32, name="matmul"):
    mrows, k = a.shape
    n = w.shape[-1] if n_cols is None else n_cols
    assert n % tn == 0
    tm = _pick(mrows, tm_cands)
    if w.ndim == 3:
        w_spec = pl.BlockSpec((None, k, tn), lambda j, i: (layer, 0, j))
    else:
        w_spec = pl.BlockSpec((k, tn), lambda j, i: (0, j))
    return pl.pallas_call(
        _mm_body,
        grid=(n // tn, mrows // tm),
        in_specs=[pl.BlockSpec((tm, k), lambda j, i: (i, 0)), w_spec],
        out_specs=pl.BlockSpec((tm, tn), lambda j, i: (i, j)),
        out_shape=jax.ShapeDtypeStruct((mrows, n), out_dtype),
        scratch_shapes=[pltpu.VMEM((k, tn), BF16)],
        compiler_params=_cparams("arbitrary", "arbitrary"),
        name=name,
    )(a, w)


def matmul_swiglu(a, w_in, layer, *, tn=512, tm_cands=(1056, 768, 512, 256)):
    mrows, k = a.shape
    f = w_in.shape[-1] // 2
    tm = _pick(mrows, tm_cands)
    nt = f // tn
    return pl.pallas_call(
        _mm_swiglu_body,
        grid=(nt, mrows // tm),
        in_specs=[pl.BlockSpec((tm, k), lambda j, i: (i, 0)),
                  pl.BlockSpec((None, k, tn), lambda j, i: (layer, 0, j)),
                  pl.BlockSpec((None, k, tn), lambda j, i: (layer, 0, j + nt))],
        out_specs=pl.BlockSpec((tm, tn), lambda j, i: (i, j)),
        out_shape=jax.ShapeDtypeStruct((mrows, f), BF16),
        scratch_shapes=[pltpu.VMEM((k, 2 * tn), BF16)],
        compiler_params=_cparams("arbitrary", "arbitrary"),
        name="ffn_in_swiglu",
    )(a, w_in, w_in)


def _mm_shift_body(a_ref, w_ref, wx_ref, o_ref, wb_ref, *, shift):
    @pl.when(pl.program_id(1) == 0)
    def _():
        tn = o_ref.shape[1]
        full = jnp.concatenate([w_ref[...], wx_ref[...]], axis=1)
        wb_ref[...] = full[:, shift:shift + tn].astype(BF16)

    o_ref[...] = jnp.dot(a_ref[...], wb_ref[...], preferred_element_type=F32)


def matmul_shift(a, w, layer, col0, n, *, tn=1024, tm_cands=(1056, 768, 512, 256), name="matmul_shift"):
    mrows, k = a.shape
    tm = _pick(mrows, tm_cands)
    shift = col0 % LANES
    base = col0 - shift
    assert base % tn == 0 and n % tn == 0 and shift > 0
    return pl.pallas_call(
        functools.partial(_mm_shift_body, shift=shift),
        grid=(n // tn, mrows // tm),
        in_specs=[pl.BlockSpec((tm, k), lambda j, i: (i, 0)),
                  pl.BlockSpec((None, k, tn), lambda j, i: (layer, 0, base // tn + j)),
                  pl.BlockSpec((None, k, LANES), lambda j, i: (layer, 0, (base + (j + 1) * tn) // LANES))],
        out_specs=pl.BlockSpec((tm, tn), lambda j, i: (i, j)),
        out_shape=jax.ShapeDtypeStruct((mrows, n), F32),
        scratch_shapes=[pltpu.VMEM((k, tn), BF16)],
        compiler_params=_cparams("arbitrary", "arbitrary"),
        name=name,
    )(a, w, w)


def _mm_cols_body(a_ref, w_ref, o_ref, wb_ref, *, n_valid):
    @pl.when(pl.program_id(0) == 0)
    def _():
        w = w_ref[...]
        if n_valid < LANES:
            lane = lax.broadcasted_iota(jnp.int32, w.shape, 1)
            w = jnp.where(lane < n_valid, w, 0.0)
        wb_ref[...] = w.astype(BF16)

    o_ref[...] = jnp.dot(a_ref[...], wb_ref[...], preferred_element_type=F32)


def matmul_cols128(a, w, layer, col0, *, tm_cands=(1056, 768, 512, 256), name="matmul_cols128"):
    mrows, k = a.shape
    tm = _pick(mrows, tm_cands)
    cb = col0 // LANES
    n_valid = min(LANES, w.shape[-1] - cb * LANES)
    return pl.pallas_call(
        functools.partial(_mm_cols_body, n_valid=n_valid),
        grid=(mrows // tm,),
        in_specs=[pl.BlockSpec((tm, k), lambda i: (i, 0)),
                  pl.BlockSpec((None, k, LANES), lambda i: (layer, 0, cb))],
        out_specs=pl.BlockSpec((tm, LANES), lambda i: (i, 0)),
        out_shape=jax.ShapeDtypeStruct((mrows, LANES), F32),
        scratch_shapes=[pltpu.VMEM((k, LANES), BF16)],
        compiler_params=_cparams("arbitrary"),
        name=name,
    )(a, w)


def _gated_residual(x_ref, acc, gp_ref, gs_ref, o_ref, *, prompt, coef, gate_idx, ts):
    if prompt:
        o_ref[...] = x_ref[...] + coef * gp_ref[0, gate_idx:gate_idx + 1, :] * acc
    else:
        rt, tn = acc.shape
        g = gs_ref[:, gate_idx:gate_idx + 1, :]
        o = x_ref[...].reshape(rt // ts, ts, tn) + coef * g * acc.reshape(rt // ts, ts, tn)
        o_ref[...] = o.reshape(rt, tn)


def _mm_pair_body(p1_ref, p2_ref, s1_ref, s2_ref, w_ref, x_ref, gp_ref, gs_ref, o_ref, wb_ref,
                  *, n_prompt_tiles, coef, gate_idx, ts):
    i = pl.program_id(1)
    k1 = p1_ref.shape[1]

    @pl.when(i == 0)
    def _():
        wb_ref[...] = w_ref[...].astype(BF16)

    def go(r1, r2, prompt):
        acc = (jnp.dot(r1[...].astype(BF16), wb_ref[:k1, :], preferred_element_type=F32)
               + jnp.dot(r2[...].astype(BF16), wb_ref[k1:, :], preferred_element_type=F32))
        _gated_residual(x_ref, acc, gp_ref, gs_ref, o_ref, prompt=prompt, coef=coef, gate_idx=gate_idx, ts=ts)

    @pl.when(i < n_prompt_tiles)
    def _():
        go(p1_ref, p2_ref, True)

    @pl.when(i >= n_prompt_tiles)
    def _():
        go(s1_ref, s2_ref, False)


def matmul_pair_residual(p1, p2, s1, s2, w, x, mod4, geom, *, gate_layer, gate_idx, coef, tn=1024,
                         name="matmul_pair"):
    mp, k1 = p1.shape
    k2 = p2.shape[1]
    n = w.shape[-1]
    npt = mp // ROW_TILE
    assert s1.shape[0] == ROW_TILE and x.shape == (mp + ROW_TILE, n)

    pmap = lambda j, i: (jnp.minimum(i, npt - 1), 0)
    smap = lambda j, i: (0, 0)
    tile = pl.BlockSpec((ROW_TILE, tn), lambda j, i: (i, j))
    body = functools.partial(_mm_pair_body, n_prompt_tiles=npt, coef=coef, gate_idx=gate_idx, ts=geom[3])
    return pl.pallas_call(
        body,
        grid=(n // tn, npt + 1),
        in_specs=[pl.BlockSpec((ROW_TILE, k1), pmap), pl.BlockSpec((ROW_TILE, k2), pmap),
                  pl.BlockSpec((ROW_TILE, k1), smap), pl.BlockSpec((ROW_TILE, k2), smap),
                  pl.BlockSpec((k1 + k2, tn), lambda j, i: (0, j)), tile]
        + _mod_specs(gate_layer, geom, 1, lambda j, i: j, tn),
        out_specs=tile,
        out_shape=jax.ShapeDtypeStruct((mp + ROW_TILE, n), F32),
        scratch_shapes=[pltpu.VMEM((k1 + k2, tn), BF16)],
        compiler_params=_cparams("arbitrary", "arbitrary"),
        name=name,
    )(p1, p2, s1, s2, w, x, mod4, mod4)


def _mm_res_body(a_ref, w_ref, x_ref, gp_ref, gs_ref, o_ref, wb_ref, *, n_prompt_tiles, coef, gate_idx, ts):
    i = pl.program_id(1)

    @pl.when(i == 0)
    def _():
        wb_ref[...] = w_ref[...].astype(BF16)

    acc = jnp.dot(a_ref[...], wb_ref[...], preferred_element_type=F32)

    @pl.when(i < n_prompt_tiles)
    def _():
        _gated_residual(x_ref, acc, gp_ref, gs_ref, o_ref, prompt=True, coef=coef, gate_idx=gate_idx, ts=ts)

    @pl.when(i >= n_prompt_tiles)
    def _():
        _gated_residual(x_ref, acc, gp_ref, gs_ref, o_ref, prompt=False, coef=coef, gate_idx=gate_idx, ts=ts)


def matmul_residual(a, w, layer, x, mod4, geom, *, gate_layer, gate_idx, coef, tn=512, name="matmul_residual"):
    bp, tp, _, ts = geom
    mrows, k = a.shape
    n = w.shape[-1]
    tile = pl.BlockSpec((ROW_TILE, tn), lambda j, i: (i, j))
    body = functools.partial(_mm_res_body, n_prompt_tiles=bp * tp // ROW_TILE, coef=coef, gate_idx=gate_idx, ts=ts)
    return pl.pallas_call(
        body,
        grid=(n // tn, mrows // ROW_TILE),
        in_specs=[pl.BlockSpec((ROW_TILE, k), lambda j, i: (i, 0)),
                  pl.BlockSpec((None, k, tn), lambda j, i: (layer, 0, j)), tile]
        + _mod_specs(gate_layer, geom, 1, lambda j, i: j, tn),
        out_specs=tile,
        out_shape=jax.ShapeDtypeStruct((mrows, n), F32),
        scratch_shapes=[pltpu.VMEM((k, tn), BF16)],
        compiler_params=_cparams("arbitrary", "arbitrary"),
        name=name,
    )(a, w, x, mod4, mod4)


def _conv_chunk(ext_ref, raw, w_ref, cin):
    ext_ref[SUBLANES:SUBLANES + CHUNK, :] = _pad_rows(raw, CHUNK)
    acc = None
    for j in range(CONV_W):
        off = SUBLANES - (CONV_W - 1) + j
        term = ext_ref[off:off + CHUNK, :] * w_ref[j:j + 1, :]
        acc = term if acc is None else acc + term
    return acc


def _conv_init(ext_ref, c0_ref, has_state):
    ext_ref[0:SUBLANES, :] = jnp.zeros((SUBLANES, ext_ref.shape[1]), F32)
    if has_state:
        ext_ref[SUBLANES - (CONV_W - 1):SUBLANES, :] = c0_ref[0]


def _conv_tail(ext_ref, cin):
    return ext_ref[cin + SUBLANES - (CONV_W - 1):cin + SUBLANES, :]


def _conv_advance(ext_ref):
    ext_ref[0:SUBLANES, :] = ext_ref[CHUNK:CHUNK + SUBLANES, :]


def _gla_body(q_ref, k_ref, v_ref, r_ref, g_ref, wg_ref, bg_ref, nrm_ref, s0_ref, o_ref, so_ref, st_ref,
              *, cin, has_state):
    c = pl.program_id(1)
    heads = range(GLA_HEADS)

    @pl.when(c == 0)
    def _():
        for h in heads:
            if has_state:
                st_ref[h] = s0_ref[0, h].T
            else:
                st_ref[h] = jnp.zeros(st_ref.shape[1:], F32)

    q = _pad_rows(q_ref[...], CHUNK) * (GLA_DK ** -0.5)
    k = _pad_rows(k_ref[...], CHUNK)
    v = _pad_rows(v_ref[...], CHUNK)
    glr = _pad_rows(g_ref[...], CHUNK)
    x = _dot(glr, wg_ref[...]) + bg_ref[...]
    logf = jax.nn.log_sigmoid(x) * (1.0 / GLA_TAU)
    if cin < CHUNK:
        rows = lax.broadcasted_iota(jnp.int32, logf.shape, 0)
        logf = jnp.where(rows < cin, logf, 0.0)
    b = _cumsum_rows(logf)
    bmid = b[CHUNK // 2 - 1:CHUNK // 2, :]
    blast = b[CHUNK - 1:CHUNK, :]
    incl, _, _ = _tri_masks(CHUNK)
    qe = q * jnp.exp(b - bmid)
    ke = k * jnp.exp(bmid - b)
    qb = q * jnp.exp(b)
    kw = k * jnp.exp(blast - b)
    sdec = jnp.exp(blast)

    def hk(a, h):
        return a[:, h * GLA_DK:(h + 1) * GLA_DK]

    def hv(a, h):
        return a[:, h * GLA_DV:(h + 1) * GLA_DV]

    sts = [st_ref[h] for h in heads]
    atts = [jnp.where(incl, _dot_nt(hk(qe, h), hk(ke, h)), 0.0) for h in heads]
    os_ = [_dot(atts[h], hv(v, h)) + _dot_nt(hk(qb, h), sts[h]) for h in heads]
    st_news = [sts[h] * hk(sdec, h) + _dot_tn(hv(v, h), hk(kw, h)) for h in heads]
    ys = []
    for h in heads:
        st_ref[h] = st_news[h]
        o = os_[h]
        var = jnp.mean(o * o, axis=-1, keepdims=True)
        ys.append(o * lax.rsqrt(var + EPS) * nrm_ref[...])
    y = jnp.concatenate(ys, axis=1)
    o_ref[0] = y[:cin] * _silu(r_ref[...])

    @pl.when(c == pl.num_programs(1) - 1)
    def _():
        for h in heads:
            so_ref[0, h] = st_news[h].T


def gla_mixer(pm, psm, wg_pad, b_gate, nrm, s0, *, row0, nb, t):
    cin = min(t, CHUNK)
    nc = t // cin
    rb0 = row0 // cin
    has_state = s0 is not None
    if s0 is None:
        s0 = jnp.zeros((1, GLA_HEADS, GLA_DK, GLA_DV), F32)
        s0_map = lambda b, c: (0, 0, 0, 0)
    else:
        s0_map = lambda b, c: (b, 0, 0, 0)

    def rmap(col0, width):
        return lambda b, c: (rb0 + b * nc + c, col0 // width)

    const = lambda b, c: (0, 0)
    body = functools.partial(_gla_body, cin=cin, has_state=has_state)
    return pl.pallas_call(
        body,
        grid=(nb, nc),
        in_specs=[pl.BlockSpec((cin, GLA_QK), rmap(EVA_Q, GLA_QK)),
                  pl.BlockSpec((cin, GLA_QK), rmap(EVA_K, GLA_QK)),
                  pl.BlockSpec((cin, GLA_VW), rmap(EVA_V, GLA_VW)),
                  pl.BlockSpec((cin, GLA_VW), rmap(EVA_R, GLA_VW)),
                  pl.BlockSpec((cin, LANES), lambda b, c: (rb0 + b * nc + c, 0)),
                  pl.BlockSpec((LANES, GLA_QK), const),
                  pl.BlockSpec((1, GLA_QK), const),
                  pl.BlockSpec((1, GLA_DV), const),
                  pl.BlockSpec((1, GLA_HEADS, GLA_DK, GLA_DV), s0_map)],
        out_specs=[pl.BlockSpec((1, cin, GLA_VW), lambda b, c: (b, c, 0)),
                   pl.BlockSpec((1, GLA_HEADS, GLA_DK, GLA_DV), lambda b, c: (b, 0, 0, 0))],
        out_shape=[jax.ShapeDtypeStruct((nb, t, GLA_VW), F32),
                   jax.ShapeDtypeStruct((nb, GLA_HEADS, GLA_DK, GLA_DV), F32)],
        scratch_shapes=[pltpu.VMEM((GLA_HEADS, GLA_DV, GLA_DK), F32)],
        compiler_params=_cparams("arbitrary", "arbitrary"),
        name="gla_mixer",
    )(pm, pm, pm, pm, psm, wg_pad, b_gate.reshape(1, GLA_QK), nrm.reshape(1, GLA_DV), s0)


def _ssd_body(z_ref, x_ref, b_ref, c_ref, sm_ref, cwx_ref, cwb_ref, cwc_ref, cbx_ref, cbb_ref, cbc_ref,
              dtb_ref, alog_ref, dexp_ref, nrm_ref, e_ref, s0_ref, c0x_ref, c0b_ref, c0c_ref,
              y_ref, so_ref, cox_ref, cob_ref, coc_ref,
              st_ref, ex_ref, eb_ref, ec_ref, *, cin, has_state):
    c = pl.program_id(1)

    @pl.when(c == 0)
    def _():
        if has_state:
            st_ref[...] = s0_ref[0].T
        else:
            st_ref[...] = jnp.zeros(st_ref.shape, F32)
        _conv_init(ex_ref, c0x_ref, has_state)
        _conv_init(eb_ref, c0b_ref, has_state)
        _conv_init(ec_ref, c0c_ref, has_state)

    xs = _silu(_conv_chunk(ex_ref, x_ref[...], cwx_ref, cin) + cbx_ref[...])
    bm = _silu(_conv_chunk(eb_ref, b_ref[...], cwb_ref, cin) + cbb_ref[...])
    cm = _silu(_conv_chunk(ec_ref, c_ref[...], cwc_ref, cin) + cbc_ref[...])

    dt = jax.nn.softplus(_pad_rows(sm_ref[...], CHUNK) + dtb_ref[...])
    if cin < CHUNK:
        rows = lax.broadcasted_iota(jnp.int32, dt.shape, 0)
        dt = jnp.where(rows < cin, dt, 0.0)
    a = dt * (-jnp.exp(alog_ref[...]))
    cum = _cumsum_rows(a)
    cum_t = _rows_of(cum, SSD_DT_LANE0 + SSD_HEADS)
    last = cum[CHUNK - 1:CHUNK, :]
    stacked = jnp.concatenate([dt, jnp.exp(cum), jnp.exp(last - cum)], axis=0)
    ex = _dot_exact_rhs(stacked, e_ref[...])
    dt_e, ecum_e, w_e = ex[:CHUNK], ex[CHUNK:2 * CHUNK], ex[2 * CHUNK:]
    xdt = xs * dt_e
    xw = xdt * w_e
    incl, _, _ = _tri_masks(CHUNK)
    lane = lax.broadcasted_iota(jnp.int32, (CHUNK, LANES), 1)
    st = st_ref[...]
    gw = SSD_HPG * SSD_P
    groups = range(SSD_GROUPS)
    pairs = SSD_HPG // 2

    def grp(a, g, w):
        return a[:, g * w:(g + 1) * w]

    cbs = [_dot_nt(grp(cm, g, SSD_STATE), grp(bm, g, SSD_STATE)) for g in groups]
    inters = [_dot(grp(cm, g, SSD_STATE), grp(st, g, gw)) for g in groups]
    st_adds = [_dot_tn(grp(bm, g, SSD_STATE), grp(xw, g, gw)) for g in groups]
    att_pairs = []
    for g in groups:
        for pr in range(pairs):
            atts = []
            for h in (g * SSD_HPG + 2 * pr, g * SSD_HPG + 2 * pr + 1):
                ln = SSD_DT_LANE0 + h
                seg = cum[:, ln:ln + 1] - cum_t[ln:ln + 1, :]
                atts.append(cbs[g] * jnp.exp(jnp.where(incl, seg, NEG_BIG)))
            att_pairs.append(jnp.concatenate(atts, axis=0))
    yys = [_dot(att_pairs[p], xdt[:, p * LANES:(p + 1) * LANES]) for p in range(SSD_GROUPS * pairs)]
    ys = [jnp.where(lane < SSD_P, yy[:CHUNK], yy[CHUNK:]) for yy in yys]
    for g in groups:
        st_ref[:, g * gw:(g + 1) * gw] = grp(st, g, gw) * grp(ecum_e, g, gw)[CHUNK - 1:CHUNK] + st_adds[g]
    y = (jnp.concatenate(ys, axis=1) + jnp.concatenate(inters, axis=1) * ecum_e + xs * dexp_ref[...])
    y = y[:cin] * _silu(z_ref[...])
    var = jnp.mean(y * y, axis=-1, keepdims=True)
    y_ref[0] = y * lax.rsqrt(var + EPS) * nrm_ref[...]

    @pl.when(c == pl.num_programs(1) - 1)
    def _():
        so_ref[0] = st_ref[...].T
        cox_ref[0] = _conv_tail(ex_ref, cin)
        cob_ref[0] = _conv_tail(eb_ref, cin)
        coc_ref[0] = _conv_tail(ec_ref, cin)

    if cin == CHUNK:
        _conv_advance(ex_ref)
        _conv_advance(eb_ref)
        _conv_advance(ec_ref)


def ssd_mixer(pm, psm, conv_w, conv_b, dtb_pad, alog_pad, d_exp, nrm, expand, s0, c0, *, row0, nb, t):
    cin = min(t, CHUNK)
    nc = t // cin
    rb0 = row0 // cin
    has_state = s0 is not None
    hp = SSD_INNER
    if s0 is None:
        s0 = jnp.zeros((1, hp, SSD_STATE), F32)
        c0 = jnp.zeros((1, CONV_W - 1, SSD_CONV_DIM), F32)
        bmap = lambda b, c: 0
    else:
        s0 = s0.reshape(nb, hp, SSD_STATE)
        bmap = lambda b, c: b

    def rmap(col0, width):
        return lambda b, c: (rb0 + b * nc + c, col0 // width)

    const = lambda b, c: (0, 0)
    kconv = CONV_W - 1
    body = functools.partial(_ssd_body, cin=cin, has_state=has_state)
    outs = pl.pallas_call(
        body,
        grid=(nb, nc),
        in_specs=[pl.BlockSpec((cin, hp), rmap(EVB_Z, hp)),
                  pl.BlockSpec((cin, hp), rmap(EVB_X, hp)),
                  pl.BlockSpec((cin, SSD_BC), rmap(EVB_B, SSD_BC)),
                  pl.BlockSpec((cin, SSD_BC), rmap(EVB_C, SSD_BC)),
                  pl.BlockSpec((cin, LANES), lambda b, c: (rb0 + b * nc + c, 0)),
                  pl.BlockSpec((CONV_W, hp), lambda b, c: (0, 0)),
                  pl.BlockSpec((CONV_W, SSD_BC), lambda b, c: (0, hp // SSD_BC)),
                  pl.BlockSpec((CONV_W, SSD_BC), lambda b, c: (0, hp // SSD_BC + 1)),
                  pl.BlockSpec((1, hp), lambda b, c: (0, 0)),
                  pl.BlockSpec((1, SSD_BC), lambda b, c: (0, hp // SSD_BC)),
                  pl.BlockSpec((1, SSD_BC), lambda b, c: (0, hp // SSD_BC + 1)),
                  pl.BlockSpec((1, LANES), const),
                  pl.BlockSpec((1, LANES), const),
                  pl.BlockSpec((1, hp), const),
                  pl.BlockSpec((1, hp), const),
                  pl.BlockSpec((LANES, hp), const),
                  pl.BlockSpec((1, hp, SSD_STATE), lambda b, c: (bmap(b, c), 0, 0)),
                  pl.BlockSpec((1, kconv, hp), lambda b, c: (bmap(b, c), 0, 0)),
                  pl.BlockSpec((1, kconv, SSD_BC), lambda b, c: (bmap(b, c), 0, hp // SSD_BC)),
                  pl.BlockSpec((1, kconv, SSD_BC), lambda b, c: (bmap(b, c), 0, hp // SSD_BC + 1))],
        out_specs=[pl.BlockSpec((1, cin, hp), lambda b, c: (b, c, 0)),
                   pl.BlockSpec((1, hp, SSD_STATE), lambda b, c: (b, 0, 0)),
                   pl.BlockSpec((1, kconv, hp), lambda b, c: (b, 0, 0)),
                   pl.BlockSpec((1, kconv, SSD_BC), lambda b, c: (b, 0, 0)),
                   pl.BlockSpec((1, kconv, SSD_BC), lambda b, c: (b, 0, 0))],
        out_shape=[jax.ShapeDtypeStruct((nb, t, hp), F32),
                   jax.ShapeDtypeStruct((nb, hp, SSD_STATE), F32),
                   jax.ShapeDtypeStruct((nb, kconv, hp), F32),
                   jax.ShapeDtypeStruct((nb, kconv, SSD_BC), F32),
                   jax.ShapeDtypeStruct((nb, kconv, SSD_BC), F32)],
        scratch_shapes=[pltpu.VMEM((SSD_STATE, hp), F32),
                        pltpu.VMEM((CHUNK + SUBLANES, hp), F32),
                        pltpu.VMEM((CHUNK + SUBLANES, SSD_BC), F32),
                        pltpu.VMEM((CHUNK + SUBLANES, SSD_BC), F32)],
        compiler_params=_cparams("arbitrary", "arbitrary"),
        name="ssd_mixer",
    )(pm, pm, pm, pm, psm, conv_w, conv_w, conv_w, conv_b, conv_b, conv_b, dtb_pad, alog_pad, d_exp,
      nrm.reshape(1, hp), expand, s0, c0, c0, c0)
    y, s_new, cx, cb_, cc = outs
    conv_s = jnp.concatenate([cx, cb_, cc], axis=-1)
    return y, s_new.reshape(nb, SSD_HEADS, SSD_P, SSD_STATE), conv_s


def _l2norm_heads(x, n_heads, width, scale):
    outs = []
    for e in range(n_heads):
        xe = x[:, e * width:(e + 1) * width]
        ss = jnp.sum(xe * xe, axis=-1, keepdims=True)
        outs.append(xe * (lax.rsqrt(ss + EPS) * scale))
    return outs


def _unit_lower_inverse(a_mats, eye):
    def mm(a, b):
        return jnp.dot(a, b, preferred_element_type=F32)

    ts = [eye - a for a in a_mats]
    pbs = [(-a).astype(BF16) for a in a_mats]
    for _ in range(int(math.log2(CHUNK)) - 1):
        ps = [mm(pb, pb) for pb in pbs]
        pbs = [p.astype(BF16) for p in ps]
        ts = [t + mm(t.astype(BF16), pb) for t, pb in zip(ts, pbs)]
    outs = []
    splits = [(_split2(a), _split2(t)) for a, t in zip(a_mats, ts)]
    ats = [mm(ah, th) + mm(ah, tl) + mm(al, th) for (ah, al), (th, tl) in splits]
    for t, at, (_, (th, _)) in zip(ts, ats, splits):
        resid = eye - t - at
        outs.append(t + mm(th, resid.astype(BF16)))
    return outs


def _gdn_body(q_ref, k_ref, v_ref, z_ref, sm_ref, cwq_ref, cwk_ref, cwv_ref, alog_ref, dtb_ref, nrm_ref,
              s0_ref, c0q_ref, c0k_ref, c0v_ref,
              o_ref, so_ref, coq_ref, cok_ref, cov_ref,
              s_ref, eq_ref, ek_ref, ev_ref, *, cin, has_state):
    c = pl.program_id(1)
    n = GDN_PACK * CHUNK

    @pl.when(c == 0)
    def _():
        if has_state:
            s_ref[...] = s0_ref[0]
        else:
            s_ref[...] = jnp.zeros(s_ref.shape, F32)
        _conv_init(eq_ref, c0q_ref, has_state)
        _conv_init(ek_ref, c0k_ref, has_state)
        _conv_init(ev_ref, c0v_ref, has_state)

    qc = _silu(_conv_chunk(eq_ref, q_ref[...], cwq_ref, cin))
    kc = _silu(_conv_chunk(ek_ref, k_ref[...], cwk_ref, cin))
    vc = _silu(_conv_chunk(ev_ref, v_ref[...], cwv_ref, cin))
    qs = _l2norm_heads(qc, GDN_QK_HEADS, GDN_DK, GDN_DK ** -0.5)
    ks = _l2norm_heads(kc, GDN_QK_HEADS, GDN_DK, 1.0)

    sm = _pad_rows(sm_ref[...], CHUNK)
    beta = jax.nn.sigmoid(sm)
    gl = -jnp.exp(alog_ref[...]) * jax.nn.softplus(sm + dtb_ref[...])
    if cin < CHUNK:
        rows = lax.broadcasted_iota(jnp.int32, sm.shape, 0)
        beta = jnp.where(rows < cin, beta, 0.0)
        gl = jnp.where(rows < cin, gl, 0.0)
    cum = _cumsum_rows(gl)
    last_row = cum[CHUNK - 1:CHUNK, :]

    rr = lax.broadcasted_iota(jnp.int32, (n, n), 0)
    cc = lax.broadcasted_iota(jnp.int32, (n, n), 1)
    same = (rr // CHUNK) == (cc // CHUNK)
    incl = same & (rr >= cc)
    strict = same & (rr > cc)
    eye = jnp.where(rr == cc, 1.0, 0.0)
    head_of_row = lax.broadcasted_iota(jnp.int32, (n, LANES), 0) // CHUNK
    lane = lax.broadcasted_iota(jnp.int32, (n, LANES), 1)
    bd_mask = (lax.broadcasted_iota(jnp.int32, (n, GDN_PACK * GDN_DK), 0) // CHUNK
               == lax.broadcasted_iota(jnp.int32, (n, GDN_PACK * GDN_DK), 1) // GDN_DK)
    srow = lax.broadcasted_iota(jnp.int32, (GDN_PACK * GDN_DK, LANES), 0) // GDN_DK
    slane = lax.broadcasted_iota(jnp.int32, (GDN_PACK * GDN_DK, LANES), 1)

    def block_diag(x):
        return jnp.where(bd_mask, jnp.concatenate([x] * GDN_PACK, axis=1), 0.0)

    def stack(parts):
        return jnp.concatenate(parts, axis=0)

    cum_rep = stack([cum] * GDN_PACK)
    beta_rep = stack([beta] * GDN_PACK)
    last_rep = jnp.broadcast_to(last_row, (n, LANES))
    cum_t = _rows_of(cum, 2 * GDN_V_HEADS)
    packs = range(GDN_V_HEADS // GDN_PACK)

    pre = []
    for j in packs:
        h0 = j * GDN_PACK
        heads = range(h0, h0 + GDN_PACK)
        k_st = stack([ks[i // GDN_REP] for i in heads])
        q_st = stack([qs[i // GDN_REP] for i in heads])
        v_st = stack([vc[:, i * GDN_DV:(i + 1) * GDN_DV] for i in heads])
        g_sel = lane == GDN_V_HEADS + h0 + head_of_row
        col = jnp.sum(jnp.where(g_sel, cum_rep, 0.0), axis=1, keepdims=True)
        rowv = jnp.concatenate([cum_t[GDN_V_HEADS + i:GDN_V_HEADS + i + 1, :] for i in heads], axis=1)
        bcol = jnp.sum(jnp.where(lane == h0 + head_of_row, beta_rep, 0.0), axis=1, keepdims=True)
        lastc = jnp.sum(jnp.where(g_sel, last_rep, 0.0), axis=1, keepdims=True)
        dec = jnp.exp(jnp.where(incl, col - rowv, NEG_BIG))
        ecol = jnp.exp(col)
        s_st = s_ref[h0:h0 + GDN_PACK].reshape(GDN_PACK * GDN_DK, GDN_DV)
        pre.append(dict(k=k_st, q=q_st, v=v_st, col=col, bcol=bcol, lastc=lastc, dec=dec, ecol=ecol, s=s_st))
    kks = [_dot_nt(p["k"], p["k"]) for p in pre]
    qks = [_dot_nt(p["q"], p["k"]) for p in pre]
    a_mats = [jnp.where(strict, p["bcol"] * kk * p["dec"], 0.0) for p, kk in zip(pre, kks)]
    t_invs = _unit_lower_inverse(a_mats, eye)
    rhss = [jnp.concatenate([p["v"] * p["bcol"], p["k"] * (p["bcol"] * p["ecol"])], axis=1) for p in pre]
    sols = [_dot(t, r) for t, r in zip(t_invs, rhss)]
    deltas = [sol[:, :GDN_DV] - _dot(block_diag(sol[:, GDN_DV:]), p["s"]) for sol, p in zip(sols, pre)]
    o_sts = [_dot(block_diag(p["q"] * p["ecol"]), p["s"]) + _dot(jnp.where(incl, qk * p["dec"], 0.0), dl)
             for p, qk, dl in zip(pre, qks, deltas)]
    outs = []
    for j, (p, dl, o_st) in enumerate(zip(pre, deltas, o_sts)):
        h0 = j * GDN_PACK
        kw_bd = block_diag(p["k"] * jnp.exp(p["lastc"] - p["col"]))
        s_dec = jnp.exp(jnp.sum(jnp.where(slane == GDN_V_HEADS + h0 + srow,
                                          jnp.broadcast_to(last_row, srow.shape), 0.0), axis=1, keepdims=True))
        s_new = p["s"] * s_dec + _dot_tn(kw_bd, dl)
        s_ref[h0:h0 + GDN_PACK] = s_new.reshape(GDN_PACK, GDN_DK, GDN_DV)
        var = jnp.mean(o_st * o_st, axis=-1, keepdims=True)
        y_st = o_st * lax.rsqrt(var + EPS) * nrm_ref[...]
        outs += [y_st[i * CHUNK:(i + 1) * CHUNK] for i in range(GDN_PACK)]
    o = jnp.concatenate(outs, axis=1)
    o_ref[0] = o[:cin] * _silu(z_ref[...])

    @pl.when(c == pl.num_programs(1) - 1)
    def _():
        so_ref[0] = s_ref[...]
        coq_ref[0] = _conv_tail(eq_ref, cin)
        cok_ref[0] = _conv_tail(ek_ref, cin)
        cov_ref[0] = _conv_tail(ev_ref, cin)

    if cin == CHUNK:
        _conv_advance(eq_ref)
        _conv_advance(ek_ref)
        _conv_advance(ev_ref)


def gdn_mixer(pm, psm, conv_w, alog_pad, dtb_pad, nrm, s0, c0, *, row0, nb, t):
    cin = min(t, CHUNK)
    nc = t // cin
    rb0 = row0 // cin
    has_state = s0 is not None
    kconv = CONV_W - 1
    if s0 is None:
        s0 = jnp.zeros((1, GDN_V_HEADS, GDN_DK, GDN_DV), F32)
        c0 = jnp.zeros((1, kconv, GDN_CONV_DIM), F32)
        bmap = lambda b: 0
    else:
        bmap = lambda b: b

    def rmap(col0, width):
        return lambda b, c: (rb0 + b * nc + c, col0 // width)

    const = lambda b, c: (0, 0)
    body = functools.partial(_gdn_body, cin=cin, has_state=has_state)
    outs = pl.pallas_call(
        body,
        grid=(nb, nc),
        in_specs=[pl.BlockSpec((cin, GDN_QK), rmap(ODA_Q, GDN_QK)),
                  pl.BlockSpec((cin, GDN_QK), rmap(ODA_K, GDN_QK)),
                  pl.BlockSpec((cin, GDN_VW), rmap(ODA_V, GDN_VW)),
                  pl.BlockSpec((cin, GDN_VW), rmap(ODA_Z, GDN_VW)),
                  pl.BlockSpec((cin, LANES), lambda b, c: (rb0 + b * nc + c, 0)),
                  pl.BlockSpec((CONV_W, GDN_QK), lambda b, c: (0, 0)),
                  pl.BlockSpec((CONV_W, GDN_QK), lambda b, c: (0, 1)),
                  pl.BlockSpec((CONV_W, GDN_VW), lambda b, c: (0, 1)),
                  pl.BlockSpec((1, LANES), const),
                  pl.BlockSpec((1, LANES), const),
                  pl.BlockSpec((1, GDN_DV), const),
                  pl.BlockSpec((1, GDN_V_HEADS, GDN_DK, GDN_DV), lambda b, c: (bmap(b), 0, 0, 0)),
                  pl.BlockSpec((1, kconv, GDN_QK), lambda b, c: (bmap(b), 0, 0)),
                  pl.BlockSpec((1, kconv, GDN_QK), lambda b, c: (bmap(b), 0, 1)),
                  pl.BlockSpec((1, kconv, GDN_VW), lambda b, c: (bmap(b), 0, 1))],
        out_specs=[pl.BlockSpec((1, cin, GDN_VW), lambda b, c: (b, c, 0)),
                   pl.BlockSpec((1, GDN_V_HEADS, GDN_DK, GDN_DV), lambda b, c: (b, 0, 0, 0)),
                   pl.BlockSpec((1, kconv, GDN_QK), lambda b, c: (b, 0, 0)),
                   pl.BlockSpec((1, kconv, GDN_QK), lambda b, c: (b, 0, 0)),
                   pl.BlockSpec((1, kconv, GDN_VW), lambda b, c: (b, 0, 0))],
        out_shape=[jax.ShapeDtypeStruct((nb, t, GDN_VW), F32),
                   jax.ShapeDtypeStruct((nb, GDN_V_HEADS, GDN_DK, GDN_DV), F32),
                   jax.ShapeDtypeStruct((nb, kconv, GDN_QK), F32),
                   jax.ShapeDtypeStruct((nb, kconv, GDN_QK), F32),
                   jax.ShapeDtypeStruct((nb, kconv, GDN_VW), F32)],
        scratch_shapes=[pltpu.VMEM((GDN_V_HEADS, GDN_DK, GDN_DV), F32),
                        pltpu.VMEM((CHUNK + SUBLANES, GDN_QK), F32),
                        pltpu.VMEM((CHUNK + SUBLANES, GDN_QK), F32),
                        pltpu.VMEM((CHUNK + SUBLANES, GDN_VW), F32)],
        compiler_params=_cparams("arbitrary", "arbitrary"),
        name="gdn_mixer",
    )(pm, pm, pm, pm, psm, conv_w, conv_w, conv_w, alog_pad, dtb_pad, nrm.reshape(1, GDN_DV),
      s0, c0, c0, c0)
    o, s_new, cq, ck, cv = outs
    return o, s_new, jnp.concatenate([cq, ck, cv], axis=-1)


def _residue_perm(dil, inverse=False):
    n = ROW_TILE // dil
    i = lax.broadcasted_iota(jnp.int32, (ROW_TILE, ROW_TILE), 0)
    j = lax.broadcasted_iota(jnp.int32, (ROW_TILE, ROW_TILE), 1)
    if inverse:
        i, j = j, i
    return jnp.where(j == (i % n) * dil + i // n, 1.0, 0.0).astype(BF16)


def _rope_body(q_ref, k_ref, v_ref, inv_ref, qo_ref, ko_ref, *rest, n_prompt_tiles, tiles_per_seq, ts):
    i = pl.program_id(0)
    rt = q_ref.shape[0]
    r = lax.broadcasted_iota(jnp.int32, (rt, LANES), 0)
    pos_prompt = (i % tiles_per_seq) * rt + r
    pos_sample = PAST_LEN + r % ts
    pos = jnp.where(i < n_prompt_tiles, pos_prompt, pos_sample).astype(F32)
    ang = pos * inv_ref[...]
    cos = jnp.cos(ang)
    sin = jnp.sin(ang)
    lane = lax.broadcasted_iota(jnp.int32, (rt, LANES), 1)
    sin = jnp.where(lane < LANES // 2, -sin, sin)
    for src, dst in ((q_ref, qo_ref), (k_ref, ko_ref)):
        for h in range(DSA_W // DSA_HD):
            x = src[:, h * DSA_HD:(h + 1) * DSA_HD]
            dst[:, h * DSA_HD:(h + 1) * DSA_HD] = x * cos + pltpu.roll(x, DSA_HD // 2, 1) * sin
    for g, (_, dil) in enumerate(DSA_GROUPS):
        perm = None if dil == 1 else _residue_perm(dil)
        for src, dst in zip((qo_ref, ko_ref, v_ref), rest[3 * g:3 * g + 3]):
            xb = src[:, g * DSA_GW:(g + 1) * DSA_GW].astype(BF16)
            if perm is not None:
                xb = jnp.dot(perm, xb, preferred_element_type=F32).astype(BF16)
            dst[...] = xb.reshape(dst.shape)


def rope_qk(pm, geom):
    bp, tp, bs, ts = geom
    mrows = pm.shape[0]
    half = DSA_HD // 2
    inv = ROPE_THETA ** (-jnp.arange(half, dtype=F32) / half)
    inv2 = jnp.concatenate([inv, inv]).reshape(1, DSA_HD)
    body = functools.partial(_rope_body, n_prompt_tiles=bp * tp // ROW_TILE, tiles_per_seq=tp // ROW_TILE, ts=ts)
    spec_o = pl.BlockSpec((ROW_TILE, DSA_W), lambda i: (i, 0))
    out_specs = [spec_o, spec_o]
    out_shape = [jax.ShapeDtypeStruct((mrows, DSA_W), F32)] * 2
    for _, dil in DSA_GROUPS:
        n = ROW_TILE // dil
        out_specs += [pl.BlockSpec((dil, n, DSA_GW), lambda i: (0, i, 0))] * 3
        out_shape += [jax.ShapeDtypeStruct((dil, mrows // dil, DSA_GW), BF16)] * 3
    outs = pl.pallas_call(
        body,
        grid=(mrows // ROW_TILE,),
        in_specs=[pl.BlockSpec((ROW_TILE, DSA_W), lambda i: (i, ODB_DQ // DSA_W)),
                  pl.BlockSpec((ROW_TILE, DSA_W), lambda i: (i, ODB_DK // DSA_W)),
                  pl.BlockSpec((ROW_TILE, DSA_W), lambda i: (i, ODB_DV // DSA_W)),
                  pl.BlockSpec((1, DSA_HD), lambda i: (0, 0))],
        out_specs=out_specs,
        out_shape=out_shape,
        compiler_params=_cparams("arbitrary"),
        name="rope_qk",
    )(pm, pm, pm, inv2)
    return outs[0], outs[1], [tuple(outs[2 + 3 * g:5 + 3 * g]) for g in range(len(DSA_GROUPS))]


def _lse_lanes(lses):
    rows = lses[0].shape[0]
    lane = lax.broadcasted_iota(jnp.int32, (rows, LANES), 1)
    out = jnp.zeros((rows, LANES), F32)
    for h, v in enumerate(lses):
        out = jnp.where(lane == h, v, out)
    return out


def _dsa_prompt_body(q_ref, kp_ref, kc_ref, vp_ref, vc_ref, o_ref, l_ref, *, jmax):
    n = pl.program_id(2)
    blk = DSA_BLOCK
    i_idx = lax.broadcasted_iota(jnp.int32, (blk, 2 * blk), 0)
    m_idx = lax.broadcasted_iota(jnp.int32, (blk, 2 * blk), 1)
    dist = i_idx + blk - m_idx
    valid = (dist >= 0) & (dist <= jmax) & ((m_idx >= blk) | (n > 0))
    scale = DSA_HD ** -0.5
    lses = []
    for h in range(DSA_HEADS):
        sl = slice(h * DSA_HD, (h + 1) * DSA_HD)
        kk = jnp.concatenate([kp_ref[:, sl], kc_ref[:, sl]], axis=0)
        vv = jnp.concatenate([vp_ref[:, sl], vc_ref[:, sl]], axis=0)
        s = _dot_nt(q_ref[:, sl], kk) * scale
        s = jnp.where(valid, s, NEG_BIG)
        mx = jnp.max(s, axis=-1, keepdims=True)
        p = jnp.exp(s - mx)
        den = jnp.sum(p, axis=-1, keepdims=True)
        o_ref[:, sl] = (_dot(p, vv) / den).astype(o_ref.dtype)
        lses.append(mx + jnp.log(den))
    l_ref[...] = _lse_lanes(lses)


def dsa_prompt(qkv, g, geom):
    bp, tp, _, _ = geom
    window, dil = DSA_GROUPS[g]
    ls = tp // dil
    nblk = ls // DSA_BLOCK
    q, k, v = qkv
    mu = q.shape[1]

    cur = lambda b, r, n: (r, b * nblk + n, 0)
    prev = lambda b, r, n: (r, b * nblk + jnp.maximum(n - 1, 0), 0)
    blk = (None, DSA_BLOCK, DSA_GW)
    return pl.pallas_call(
        functools.partial(_dsa_prompt_body, jmax=window // dil),
        grid=(bp, dil, nblk),
        in_specs=[pl.BlockSpec(blk, cur), pl.BlockSpec(blk, prev), pl.BlockSpec(blk, cur),
                  pl.BlockSpec(blk, prev), pl.BlockSpec(blk, cur)],
        out_specs=[pl.BlockSpec(blk, cur), pl.BlockSpec((None, DSA_BLOCK, LANES), cur)],
        out_shape=[jax.ShapeDtypeStruct((dil, mu, DSA_GW), BF16),
                   jax.ShapeDtypeStruct((dil, mu, LANES), F32)],
        compiler_params=_cparams("arbitrary", "arbitrary", "arbitrary"),
        name=f"dsa_prompt_w{window}",
    )(q, k, k, v, v)


def _dsa_sample_body(q_ref, kn_ref, vn_ref, cache_ref, o_ref, l_ref, m_scr, d_scr, acc_scr, *, window, dil):
    j = pl.program_id(1)
    nj = pl.num_programs(1)
    ts = q_ref.shape[0]
    nh = DSA_HEADS
    scale = DSA_HD ** -0.5

    def heads_to_rows(ref):
        return jnp.concatenate([ref[:, h * DSA_HD:(h + 1) * DSA_HD] for h in range(nh)], axis=0)

    @pl.when(j == 0)
    def _():
        m_scr[...] = jnp.full(m_scr.shape, NEG_BIG, F32)
        d_scr[...] = jnp.zeros(d_scr.shape, F32)
        acc_scr[...] = jnp.zeros(acc_scr.shape, F32)

    def update(s, vals, ok):
        s = jnp.where(ok, s, NEG_BIG)
        m_old = m_scr[...]
        m_new = jnp.maximum(m_old, jnp.max(s, axis=-1, keepdims=True))
        alpha = jnp.exp(m_old - m_new)
        p = jnp.where(ok, jnp.exp(s - m_new[:, 0:1]), 0.0)
        d_scr[...] = alpha * d_scr[...] + jnp.sum(p, axis=-1, keepdims=True)
        acc_scr[...] = alpha * acc_scr[...] + _dot(p, vals)
        m_scr[...] = m_new

    q_rows = heads_to_rows(q_ref) * scale
    ub, n_res = cache_ref.shape[1], cache_ref.shape[2] // (2 * nh)
    lb = ub * n_res
    x4 = cache_ref[0].reshape(lb, 2, nh, DSA_HD)
    k_all = x4[:, 0].reshape(lb * nh, DSA_HD)
    v_all = x4[:, 1].reshape(lb * nh, DSA_HD)
    r = lax.broadcasted_iota(jnp.int32, (nh * ts, lb * nh), 0)
    c = lax.broadcasted_iota(jnp.int32, (nh * ts, lb * nh), 1)
    pos = c // nh
    dist = window + r % ts - ((j * ub + pos // n_res) * dil + pos % n_res)
    ok = (r // ts == c % nh) & ((dist & (dil - 1)) == 0) & (dist <= window)
    update(_dot_nt(q_rows, k_all), v_all, ok)

    @pl.when(j == nj - 1)
    def _():
        r2 = lax.broadcasted_iota(jnp.int32, (nh * ts, nh * ts), 0)
        c2 = lax.broadcasted_iota(jnp.int32, (nh * ts, nh * ts), 1)
        d2 = r2 % ts - c2 % ts
        ok2 = (r2 // ts == c2 // ts) & (d2 >= 0) & ((d2 & (dil - 1)) == 0)
        update(_dot_nt(q_rows, heads_to_rows(kn_ref)), heads_to_rows(vn_ref), ok2)
        o_rows = acc_scr[...] / d_scr[...]
        lse = m_scr[...] + jnp.log(d_scr[...])
        for h in range(nh):
            o_ref[:, h * DSA_HD:(h + 1) * DSA_HD] = o_rows[h * ts:(h + 1) * ts]
        l_ref[...] = _lse_lanes([lse[h * ts:(h + 1) * ts, 0:1] for h in range(nh)])


def dsa_sample(rq, rk, pm, cache, g, geom):
    bp, tp, bs, ts = geom
    window, dil = DSA_GROUPS[g]
    lcache = cache.shape[1]
    assert lcache == window and window % dil == 0, "cache must hold exactly the window"
    n_res = min(dil, ts)
    strides = lcache // dil
    ub = min(strides, 512 // n_res)
    rb0 = bp * tp // ts
    kv_rows = 2 * DSA_HEADS
    cache4 = cache.reshape(bs, strides, dil * kv_rows, DSA_HD)
    blk = (ts, DSA_GW)
    body = functools.partial(_dsa_sample_body, window=window, dil=dil)
    return pl.pallas_call(
        body,
        grid=(bs, strides // ub),
        in_specs=[pl.BlockSpec(blk, lambda b, j: (rb0 + b, g)),
                  pl.BlockSpec(blk, lambda b, j: (rb0 + b, g)),
                  pl.BlockSpec(blk, lambda b, j: (rb0 + b, ODB_DV // DSA_GW + g)),
                  pl.BlockSpec((1, ub, n_res * kv_rows, DSA_HD), lambda b, j: (b, j, 0, 0))],
        out_specs=[pl.BlockSpec(blk, lambda b, j: (b, 0)), pl.BlockSpec((ts, LANES), lambda b, j: (b, 0))],
        out_shape=[jax.ShapeDtypeStruct((bs * ts, DSA_GW), F32),
                   jax.ShapeDtypeStruct((bs * ts, LANES), F32)],
        scratch_shapes=[pltpu.VMEM((DSA_HEADS * ts, DSA_HD), F32)] * 3,
        compiler_params=_cparams("arbitrary", "arbitrary"),
        name=f"dsa_sample_w{window}",
    )(rq, rk, pm, cache4)


def _dsa_merge_body(o0, o1, o2, l0, l1, l2, e_ref, out_ref, *, dils):
    os_, ls_ = [], []
    for o_ref, l_ref, dil in zip((o0, o1, o2), (l0, l1, l2), dils):
        o = o_ref[...].reshape(ROW_TILE, DSA_GW)
        l = l_ref[...].reshape(ROW_TILE, LANES)
        if dil > 1:
            pinv = _residue_perm(dil, inverse=True)
            o = jnp.dot(pinv, o.astype(BF16), preferred_element_type=F32)
            l = _dot_exact_lhs(pinv, l)
        os_.append(o.astype(F32))
        ls_.append(l)
    mx = jnp.maximum(jnp.maximum(ls_[0], ls_[1]), ls_[2])
    ws = [jnp.exp(l - mx) for l in ls_]
    tot = ws[0] + ws[1] + ws[2]
    acc = None
    for w, o in zip(ws, os_):
        term = _dot_exact_rhs(w / tot, e_ref[...]) * o
        acc = term if acc is None else acc + term
    out_ref[...] = acc


def dsa_merge(os_, ls_, dils, n_tiles):
    in_specs = []
    for width in (DSA_GW, LANES):
        for dil in dils:
            if dil:
                in_specs.append(pl.BlockSpec((dil, ROW_TILE // dil, width), lambda i: (0, i, 0)))
            else:
                in_specs.append(pl.BlockSpec((ROW_TILE, width), lambda i: (i, 0)))
    in_specs.append(pl.BlockSpec((LANES, DSA_GW), lambda i: (0, 0)))
    expand = jnp.asarray((np.arange(LANES)[:, None] == (np.arange(DSA_GW)[None, :] // DSA_HD)).astype(np.float32))
    return pl.pallas_call(
        functools.partial(_dsa_merge_body, dils=tuple(max(d, 1) for d in dils)),
        grid=(n_tiles,),
        in_specs=in_specs,
        out_specs=pl.BlockSpec((ROW_TILE, DSA_GW), lambda i: (i, 0)),
        out_shape=jax.ShapeDtypeStruct((n_tiles * ROW_TILE, DSA_GW), F32),
        compiler_params=_cparams("arbitrary"),
        name="dsa_merge",
    )(*os_, *ls_, expand)


def _pad_lanes(vec, offset=0):
    return jnp.zeros((1, LANES), F32).at[0, offset:offset + vec.shape[0]].set(vec)


def kernel(x_prompt, x_sample, c_prompt, c_sample, state_gla, state_ssd, state_ssd_conv, state_gdn, state_gdn_conv, cache_dsa_w128, cache_dsa_w512, cache_dsa_w2048, ln_ffn1, ln_mix, ln_ffn2, w_ada, b_ada, ffn1_w_in, ffn1_w_out, ffn2_w_in, ffn2_w_out, even_w_in, gla_w_gate2, gla_b_gate, gla_norm, ssd_conv_w, ssd_conv_b, ssd_dt_bias, ssd_A_log, ssd_D, ssd_norm, even_w_out, odd_w_in, gdn_conv_w, gdn_A_log, gdn_dt_bias, gdn_norm, odd_w_out, final_norm):
    bp, tp, d = x_prompt.shape
    bs, ts, _ = x_sample.shape
    depth = w_ada.shape[0]
    geom = (bp, tp, bs, ts)
    assert tp % ROW_TILE == 0 and bs * ts == ROW_TILE and ts == SUBLANES
    mp = bp * tp
    caches = (cache_dsa_w128, cache_dsa_w512, cache_dsa_w2048)

    x = jnp.concatenate([x_prompt.reshape(mp, d), x_sample.reshape(bs * ts, d)], axis=0)
    n_pad = (-(bs + bp)) % SUBLANES
    c_all = jnp.concatenate([c_sample, c_prompt, jnp.zeros((n_pad, d), F32)], axis=0)
    mod4 = ada_mod(c_all, w_ada, b_ada).reshape(depth, c_all.shape[0], N_MOD, d)
    expand = jnp.asarray((np.arange(LANES)[:, None] - SSD_DT_LANE0
                          == (np.arange(SSD_INNER)[None, :] // SSD_P)).astype(np.float32))

    new ={k: ([], []) for k in ("gla", "ssd", "ssd_conv", "gdn", "gdn_conv", "dsa0", "dsa1", "dsa2")}

    h = rows_norm(x, mod4, ln_ffn1[0], geom, mod_layer=0, shift_idx=0, scale_idx=1)
    for l in range(depth):
        act = matmul_swiglu(h, ffn1_w_in, l)
        x = matmul_residual(act, ffn1_w_out, l, x, mod4, geom, gate_layer=l, gate_idx=2, coef=0.5, name="ffn_out")
        h = rows_norm(x, mod4, ln_mix[l], geom, mod_layer=l, shift_idx=3, scale_idx=4)
        i = l // 2
        if l % 2 == 0:
            assert even_w_in.shape[-1] == EVEN_IN
            pma = matmul(h, even_w_in, layer=i, n_cols=EVA_W, tn=1024, name="even_in_a")
            pmb = matmul_shift(h, even_w_in, i, EVB_COL, EVB_W, name="even_in_b")
            p_glr = matmul_cols128(h, even_w_in, i, EV_GLR_COL, name="even_in_glr")
            p_dt = matmul_cols128(h, even_w_in, i, EV_DT_COL, name="even_in_dt")
            wg_pad = jnp.zeros((LANES, GLA_QK), F32).at[:GLA_RANK].set(gla_w_gate2[i])
            dtb = _pad_lanes(ssd_dt_bias[i], SSD_DT_LANE0)
            alog = _pad_lanes(ssd_A_log[i], SSD_DT_LANE0)
            d_exp = jnp.repeat(ssd_D[i], SSD_P).reshape(1, SSD_INNER)
            mix = []
            for grp, (row0, nb, t, s_gla, s_ssd, s_conv) in enumerate((
                    (0, bp, tp, None, None, None),
                    (mp, bs, ts, state_gla[i], state_ssd[i], state_ssd_conv[i]))):
                o_gla, gla_new = gla_mixer(pma, p_glr, wg_pad, gla_b_gate[i], gla_norm[i], s_gla,
                                           row0=row0, nb=nb, t=t)
                y_ssd, ssd_new, conv_new = ssd_mixer(pmb, p_dt, ssd_conv_w[i], ssd_conv_b[i].reshape(1, -1), dtb,
                                                     alog, d_exp, ssd_norm[i], expand, s_ssd, s_conv,
                                                     row0=row0, nb=nb, t=t)
                new["gla"][grp].append(gla_new)
                new["ssd"][grp].append(ssd_new)
                new["ssd_conv"][grp].append(conv_new)
                mix += [o_gla.reshape(nb * t, GLA_VW), y_ssd.reshape(nb * t, SSD_INNER)]
            x = matmul_pair_residual(*mix, even_w_out[i], x, mod4, geom, gate_layer=l, gate_idx=5, coef=1.0,
                                     name="even_out")
        else:
            assert odd_w_in.shape[-1] == ODD_IN
            pma = matmul(h, odd_w_in, layer=i, n_cols=ODA_W, tn=1024, name="odd_in_a")
            pmb = matmul_shift(h, odd_w_in, i, ODB_COL, ODB_W, name="odd_in_b")
            p_ba = matmul_cols128(h, odd_w_in, i, OD_BA_COL, name="odd_in_ba")
            rq, rk, qkv = rope_qk(pmb, geom)
            alog = _pad_lanes(gdn_A_log[i], GDN_V_HEADS)
            dtb = _pad_lanes(gdn_dt_bias[i], GDN_V_HEADS)
            mix = []
            for grp, (row0, nb, t, s_gdn, s_conv) in enumerate((
                    (0, bp, tp, None, None), (mp, bs, ts, state_gdn[i], state_gdn_conv[i]))):
                o_gdn, gdn_new, conv_new = gdn_mixer(pma, p_ba, gdn_conv_w[i], alog, dtb, gdn_norm[i],
                                                     s_gdn, s_conv, row0=row0, nb=nb, t=t)
                new["gdn"][grp].append(gdn_new)
                new["gdn_conv"][grp].append(conv_new)
                os_, ls_ = [], []
                for g, (window, dil) in enumerate(DSA_GROUPS):
                    vcol = ODB_DV + g * DSA_GW
                    if grp == 0:
                        o_g, l_g = dsa_prompt(qkv[g], g, geom)
                        keep = min(window, tp)
                        kg = jnp.stack([rk[(b + 1) * tp - keep:(b + 1) * tp, g * DSA_GW:(g + 1) * DSA_GW]
                                        for b in range(bp)])
                        vg = jnp.stack([pmb[(b + 1) * tp - keep:(b + 1) * tp, vcol:vcol + DSA_GW]
                                        for b in range(bp)])
                    else:
                        o_g, l_g = dsa_sample(rq, rk, pmb, caches[g][i], g, geom)
                        keep = ts
                        kg = rk[mp:, g * DSA_GW:(g + 1) * DSA_GW]
                        vg = pmb[mp:, vcol:vcol + DSA_GW]
                    rows = jnp.stack([kg.reshape(nb, keep, DSA_HEADS, DSA_HD),
                                      vg.reshape(nb, keep, DSA_HEADS, DSA_HD)], axis=2)
                    new["dsa%d" % g][grp].append(rows)
                    os_.append(o_g)
                    ls_.append(l_g)
                if grp == 0:
                    o_dsa = dsa_merge(os_, ls_, [dil for _, dil in DSA_GROUPS], mp // ROW_TILE)
                else:
                    o_dsa = dsa_merge(os_, ls_, [0] * len(DSA_GROUPS), 1)
                mix += [o_gdn.reshape(nb * t, GDN_VW), o_dsa]
            x = matmul_pair_residual(*mix, odd_w_out[i], x, mod4, geom, gate_layer=l, gate_idx=5, coef=1.0,
                                     name="odd_out")
        h = rows_norm(x, mod4, ln_ffn2[l], geom, mod_layer=l, shift_idx=6, scale_idx=7)
        act = matmul_swiglu(h, ffn2_w_in, l)
        x = matmul_residual(act, ffn2_w_out, l, x, mod4, geom, gate_layer=l, gate_idx=8, coef=0.5, name="ffn_out")
        if l + 1 < depth:
            h = rows_norm(x, mod4, ln_ffn1[l + 1], geom, mod_layer=l + 1, shift_idx=0, scale_idx=1)
        else:
            y = rows_norm(x, mod4, final_norm, geom, modulate=False)

    y_prompt = y[:mp].reshape(bp, tp, d)
    y_sample = y[mp:].reshape(bs, ts, d)
    outs = [y_prompt, y_sample]
    for name in ("gla", "ssd", "ssd_conv", "gdn", "gdn_conv", "dsa0", "dsa1", "dsa2"):
        outs.append(jnp.stack(new[name][0]))
        outs.append(jnp.stack(new[name][1]))
    return tuple(outs)
```

```python
import functools

import numpy as np
import jax
import jax.numpy as jnp
from jax import lax
from jax.experimental import pallas as pl
from jax.experimental.pallas import tpu as pltpu

F32 = jnp.float32
BF16 = jnp.bfloat16

D_MODEL = 2048
D_FF = 5632
N_MOD = 9
EPS = 1e-6
CONV_W = 4
ROPE_THETA = 10000.0
PAST_LEN = 8192

GLA_HEADS, GLA_DK, GLA_DV, GLA_RANK, GLA_TAU = 4, 128, 256, 16, 16.0
GLA_QK, GLA_VW = GLA_HEADS * GLA_DK, GLA_HEADS * GLA_DV
SSD_HEADS, SSD_P, SSD_STATE, SSD_GROUPS = 32, 64, 128, 4
SSD_HPG = SSD_HEADS // SSD_GROUPS
SSD_INNER = SSD_HEADS * SSD_P
SSD_BC = SSD_GROUPS * SSD_STATE
SSD_CONV_DIM = SSD_INNER + 2 * SSD_BC
GDN_QK_HEADS, GDN_V_HEADS, GDN_DK, GDN_DV = 8, 16, 128, 128
GDN_QK, GDN_VW = GDN_QK_HEADS * GDN_DK, GDN_V_HEADS * GDN_DV
GDN_CONV_DIM = 2 * GDN_QK + GDN_VW
GDN_REP = GDN_V_HEADS // GDN_QK_HEADS
DSA_GROUPS = ((128, 1), (512, 4), (2048, 16))
DSA_HEADS, DSA_HD, DSA_BLOCK = 8, 128, 128
DSA_GW = DSA_HEADS * DSA_HD
DSA_W = len(DSA_GROUPS) * DSA_GW

LANES = 128
SUBLANES = 8
MXU_DIM = 256
VMEM_LIMIT_BYTES = 56 * 1024 * 1024

ROW_TILE = 256
RESIDUAL_TILE = 512
CHUNK = 64
GDN_PACK = MXU_DIM // CHUNK
NEG_BIG = -1e30

EVA_Q, EVA_K, EVA_V, EVA_R = 0, GLA_QK, 2 * GLA_QK, 2 * GLA_QK + GLA_VW
EVA_W = 2 * GLA_QK + 2 * GLA_VW
EV_GLR_COL = EVA_W
EVB_COL = EVA_W + GLA_RANK
EVB_Z, EVB_X, EVB_B, EVB_C = 0, SSD_INNER, 2 * SSD_INNER, 2 * SSD_INNER + SSD_BC
EVB_W = SSD_INNER + SSD_CONV_DIM
EV_DT_COL = EVB_COL + EVB_W
EVEN_IN = EV_DT_COL + SSD_HEADS
SSD_DT_LANE0 = EV_DT_COL % LANES
ODA_Q, ODA_K, ODA_V, ODA_Z = 0, GDN_QK, 2 * GDN_QK, 2 * GDN_QK + GDN_VW
ODA_W = 2 * GDN_QK + 2 * GDN_VW
OD_BA_COL = ODA_W
ODB_COL = ODA_W + 2 * GDN_V_HEADS
ODB_DQ, ODB_DK, ODB_DV = 0, DSA_W, 2 * DSA_W
ODB_W = 3 * DSA_W
ODD_IN = ODB_COL + ODB_W


def _cparams(*sem):
    return pltpu.CompilerParams(dimension_semantics=sem, vmem_limit_bytes=VMEM_LIMIT_BYTES)


def _dot(a, b):
    return jnp.dot(a.astype(BF16), b.astype(BF16), preferred_element_type=F32)


def _dot_nt(a, b):
    return lax.dot_general(a.astype(BF16), b.astype(BF16), (((1,), (1,)), ((), ())), preferred_element_type=F32)


def _dot_tn(a, b):
    return lax.dot_general(a.astype(BF16), b.astype(BF16), (((0,), (0,)), ((), ())), preferred_element_type=F32)


def _split2(x):
    hi = x.astype(BF16)
    lo = (x - hi.astype(F32)).astype(BF16)
    return hi, lo


def _split3(x):
    hi = x.astype(BF16)
    r1 = x - hi.astype(F32)
    mid = r1.astype(BF16)
    lo = (r1 - mid.astype(F32)).astype(BF16)
    return hi, mid, lo


def _dot_exact_rhs(x, e):
    hi, lo = _split2(x)
    eb = e.astype(BF16)
    return (jnp.dot(hi, eb, preferred_element_type=F32) + jnp.dot(lo, eb, preferred_element_type=F32))


def _dot_exact_lhs(e, x):
    eb = e.astype(BF16)
    return sum(jnp.dot(eb, p, preferred_element_type=F32) for p in _split3(x))


def _dot_nt_exact_lhs(e, x):
    eb = e.astype(BF16)
    nt = (((1,), (1,)), ((), ()))
    return sum(lax.dot_general(eb, p, nt, preferred_element_type=F32) for p in _split3(x))


def _cumsum_rows(x):
    n = x.shape[0]
    r = lax.broadcasted_iota(jnp.int32, (n, n), 0)
    c = lax.broadcasted_iota(jnp.int32, (n, n), 1)
    return _dot_exact_lhs(jnp.where(r >= c, 1.0, 0.0), x)


def _rows_of(x, n_rows):
    r = lax.broadcasted_iota(jnp.int32, (n_rows, x.shape[1]), 0)
    c = lax.broadcasted_iota(jnp.int32, (n_rows, x.shape[1]), 1)
    return _dot_nt_exact_lhs(jnp.where(r == c, 1.0, 0.0), x)


def _pad_rows(x, n):
    if x.shape[0] == n:
        return x
    return jnp.concatenate([x, jnp.zeros((n - x.shape[0],) + x.shape[1:], x.dtype)], axis=0)


def _silu(x):
    return x * jax.nn.sigmoid(x)


def _tri_masks(n):
    r = lax.broadcasted_iota(jnp.int32, (n, n), 0)
    c = lax.broadcasted_iota(jnp.int32, (n, n), 1)
    return r >= c, r > c, r == c


def _ada_body(c_ref, w_ref, b_ref, o_ref):
    cond = _silu(c_ref[...])
    o_ref[...] = _dot(cond, w_ref[...]) + b_ref[...]


def ada_mod(c_all, w_ada, b_ada):
    depth, d, n = w_ada.shape
    rows = c_all.shape[0]
    tn = 1024
    return pl.pallas_call(
        _ada_body,
        grid=(depth, n // tn),
        in_specs=[pl.BlockSpec((rows, d), lambda l, j: (0, 0)),
                  pl.BlockSpec((None, d, tn), lambda l, j: (l, 0, j)),
                  pl.BlockSpec((None, 1, tn), lambda l, j: (l, 0, j))],
        out_specs=pl.BlockSpec((None, rows, tn), lambda l, j: (l, 0, j)),
        out_shape=jax.ShapeDtypeStruct((depth, rows, n), F32),
        compiler_params=_cparams("arbitrary", "arbitrary"),
        name="ada_mod",
    )(c_all, w_ada, b_ada.reshape(depth, 1, n))


def _mod_specs(layer, geom, tiles_axis, col_map, width, row_tile=ROW_TILE):
    bp, tp, bs, _ = geom
    tiles_per_seq = tp // row_tile

    def prompt_map(*idx):
        return (layer, bs + jnp.minimum(idx[tiles_axis] // tiles_per_seq, bp - 1), 0, col_map(*idx))

    def sample_map(*idx):
        return (layer, 0, 0, col_map(*idx))

    return [pl.BlockSpec((None, 1, N_MOD, width), prompt_map), pl.BlockSpec((None, bs, N_MOD, width), sample_map)]


def _norm_body(x_ref, mp_ref, ms_ref, gain_ref, o_ref, *, modulate, shift_idx, scale_idx, n_prompt_tiles, ts):
    i = pl.program_id(0)
    gain = gain_ref[...]

    def compute(shape3, mod):
        x = x_ref[...]
        if shape3 is not None:
            x = x.reshape(shape3)
        var = jnp.mean(x * x, axis=-1, keepdims=True)
        y = x * lax.rsqrt(var + EPS) * gain
        if modulate:
            y = y * (1.0 + mod(scale_idx)) + mod(shift_idx)
        o_ref[...] = y.reshape(o_ref.shape).astype(o_ref.dtype)

    @pl.when(i < n_prompt_tiles)
    def _():
        compute(None, lambda k: mp_ref[0, k:k + 1, :])

    @pl.when(i >= n_prompt_tiles)
    def _():
        rt, d = x_ref.shape
        compute((rt // ts, ts, d), lambda k: ms_ref[:, k:k + 1, :])


def rows_norm(x, mod4, gain, geom, *, mod_layer=0, shift_idx=0, scale_idx=0, modulate=True):
    bp, tp, bs, ts = geom
    mrows, d = x.shape
    row_spec = pl.BlockSpec((ROW_TILE, d), lambda i: (i, 0))
    body = functools.partial(_norm_body, modulate=modulate, shift_idx=shift_idx, scale_idx=scale_idx,
                             n_prompt_tiles=bp * tp // ROW_TILE, ts=ts)
    return pl.pallas_call(
        body,
        grid=(mrows // ROW_TILE,),
        in_specs=[row_spec] + _mod_specs(mod_layer, geom, 0, lambda i: 0, d)
        + [pl.BlockSpec((1, d), lambda i: (0, 0))],
        out_specs=row_spec,
        out_shape=jax.ShapeDtypeStruct((mrows, d), BF16 if modulate else F32),
        compiler_params=_cparams("arbitrary"),
        name="rows_norm",
    )(x, mod4, mod4, gain.reshape(1, d))


def _mm_body(a_ref, w_ref, o_ref, wb_ref):
    @pl.when(pl.program_id(1) == 0)
    def _():
        wb_ref[...] = w_ref[...].astype(BF16)

    o_ref[...] = jnp.dot(a_ref[...], wb_ref[...], preferred_element_type=F32).astype(o_ref.dtype)


def _mm_swiglu_body(a_ref, wa_ref, wg_ref, o_ref, wb_ref):
    tn = o_ref.shape[1]

    @pl.when(pl.program_id(1) == 0)
    def _():
        wb_ref[:, :tn] = wa_ref[...].astype(BF16)
        wb_ref[:, tn:] = wg_ref[...].astype(BF16)

    u = jnp.dot(a_ref[...], wb_ref[...], preferred_element_type=F32)
    o_ref[...] = (_silu(u[:, :tn]) * u[:, tn:]).astype(o_ref.dtype)


def _pick(n, cands):
    for c in cands:
        if n % c == 0:
            return c
    raise ValueError(f"no tile for {n} in {cands}")


def matmul(a, w, *, layer=None, n_cols=None, tm_cands=(1056, 768, 512, 256), tn=512, out_dtype=F32, name="matmul"):
    mrows, k = a.shape
    n = w.shape[-1] if n_cols is None else n_cols
    assert n % tn == 0
    tm = _pick(mrows, tm_cands)
    if w.ndim == 3:
        w_spec = pl.BlockSpec((None, k, tn), lambda j, i: (layer, 0, j))
    else:
        w_spec = pl.BlockSpec((k, tn), lambda j, i: (0, j))
    return pl.pallas_call(
        _mm_body,
        grid=(n // tn, mrows // tm),
        in_specs=[pl.BlockSpec((tm, k), lambda j, i: (i, 0)), w_spec],
        out_specs=pl.BlockSpec((tm, tn), lambda j, i: (i, j)),
        out_shape=jax.ShapeDtypeStruct((mrows, n), out_dtype),
        scratch_shapes=[pltpu.VMEM((k, tn), BF16)],
        compiler_params=_cparams("arbitrary", "arbitrary"),
        name=name,
    )(a, w)


def matmul_swiglu(a, w_in, layer, *, tn=512, tm_cands=(2112, 1056, 768, 512, 256)):
    mrows, k = a.shape
    f = w_in.shape[-1] // 2
    tm = _pick(mrows, tm_cands)
    nt = f // tn
    return pl.pallas_call(
        _mm_swiglu_body,
        grid=(nt, mrows // tm),
        in_specs=[pl.BlockSpec((tm, k), lambda j, i: (i, 0)),
                  pl.BlockSpec((None, k, tn), lambda j, i: (layer, 0, j)),
                  pl.BlockSpec((None, k, tn), lambda j, i: (layer, 0, j + nt))],
        out_specs=pl.BlockSpec((tm, tn), lambda j, i: (i, j)),
        out_shape=jax.ShapeDtypeStruct((mrows, f), BF16),
        scratch_shapes=[pltpu.VMEM((k, 2 * tn), BF16)],
        compiler_params=_cparams("arbitrary", "arbitrary"),
        name="ffn_in_swiglu",
    )(a, w_in, w_in)


def _mm_shift_body(a_ref, w_ref, wx_ref, o_ref, wb_ref, *, shift):
    @pl.when(pl.program_id(1) == 0)
    def _():
        tn = o_ref.shape[1]
        full = jnp.concatenate([w_ref[...], wx_ref[...]], axis=1)
        wb_ref[...] = full[:, shift:shift + tn].astype(BF16)

    o_ref[...] = jnp.dot(a_ref[...], wb_ref[...], preferred_element_type=F32)


def matmul_shift(a, w, layer, col0, n, *, tn=1024, tm_cands=(1056, 768, 512, 256), name="matmul_shift"):
    mrows, k = a.shape
    tm = _pick(mrows, tm_cands)
    shift = col0 % LANES
    base = col0 - shift
    assert base % tn == 0 and n % tn == 0 and shift > 0
    return pl.pallas_call(
        functools.partial(_mm_shift_body, shift=shift),
        grid=(n // tn, mrows // tm),
        in_specs=[pl.BlockSpec((tm, k), lambda j, i: (i, 0)),
                  pl.BlockSpec((None, k, tn), lambda j, i: (layer, 0, base // tn + j)),
                  pl.BlockSpec((None, k, LANES), lambda j, i: (layer, 0, (base + (j + 1) * tn) // LANES))],
        out_specs=pl.BlockSpec((tm, tn), lambda j, i: (i, j)),
        out_shape=jax.ShapeDtypeStruct((mrows, n), F32),
        scratch_shapes=[pltpu.VMEM((k, tn), BF16)],
        compiler_params=_cparams("arbitrary", "arbitrary"),
        name=name,
    )(a, w, w)


def _mm_cols_body(a_ref, w_ref, o_ref, wb_ref, *, n_valid):
    @pl.when(pl.program_id(0) == 0)
    def _():
        w = w_ref[...]
        if n_valid < LANES:
            lane = lax.broadcasted_iota(jnp.int32, w.shape, 1)
            w = jnp.where(lane < n_valid, w, 0.0)
        wb_ref[...] = w.astype(BF16)

    o_ref[...] = jnp.dot(a_ref[...], wb_ref[...], preferred_element_type=F32)


def matmul_cols128(a, w, layer, col0, *, tm_cands=(1056, 768, 512, 256), name="matmul_cols128"):
    mrows, k = a.shape
    tm = _pick(mrows, tm_cands)
    cb = col0 // LANES
    n_valid = min(LANES, w.shape[-1] - cb * LANES)
    return pl.pallas_call(
        functools.partial(_mm_cols_body, n_valid=n_valid),
        grid=(mrows // tm,),
        in_specs=[pl.BlockSpec((tm, k), lambda i: (i, 0)),
                  pl.BlockSpec((None, k, LANES), lambda i: (layer, 0, cb))],
        out_specs=pl.BlockSpec((tm, LANES), lambda i: (i, 0)),
        out_shape=jax.ShapeDtypeStruct((mrows, LANES), F32),
        scratch_shapes=[pltpu.VMEM((k, LANES), BF16)],
        compiler_params=_cparams("arbitrary"),
        name=name,
    )(a, w)


def _residual_tile(geom):
    return _pick(geom[1], (RESIDUAL_TILE, ROW_TILE))


def _gated_residual(x_ref, acc, gp_ref, gs_ref, o_ref, *, prompt, coef, gate_idx, ts):
    if prompt:
        o_ref[...] = x_ref[...] + coef * gp_ref[0, gate_idx:gate_idx + 1, :] * acc
    else:
        tn = acc.shape[1]
        g = gs_ref[:, gate_idx:gate_idx + 1, :]
        o = (x_ref[:ROW_TILE, :].reshape(ROW_TILE // ts, ts, tn)
             + coef * g * acc[:ROW_TILE].reshape(ROW_TILE // ts, ts, tn))
        o_ref[:ROW_TILE, :] = o.reshape(ROW_TILE, tn)


def _mm_pair_body(p1_ref, p2_ref, s1_ref, s2_ref, w_ref, x_ref, gp_ref, gs_ref, o_ref, wb_ref,
                  *, n_prompt_tiles, coef, gate_idx, ts):
    i = pl.program_id(1)
    k1 = p1_ref.shape[1]

    @pl.when(i == 0)
    def _():
        wb_ref[...] = w_ref[...].astype(BF16)

    def go(r1, r2, prompt):
        acc = (jnp.dot(r1[...].astype(BF16), wb_ref[:k1, :], preferred_element_type=F32)
               + jnp.dot(r2[...].astype(BF16), wb_ref[k1:, :], preferred_element_type=F32))
        _gated_residual(x_ref, acc, gp_ref, gs_ref, o_ref, prompt=prompt, coef=coef, gate_idx=gate_idx, ts=ts)

    @pl.when(i < n_prompt_tiles)
    def _():
        go(p1_ref, p2_ref, True)

    @pl.when(i >= n_prompt_tiles)
    def _():
        go(s1_ref, s2_ref, False)


def matmul_pair_residual(p1, p2, s1, s2, w, x, mod4, geom, *, gate_layer, gate_idx, coef, tn=512,
                         name="matmul_pair"):
    mp, k1 = p1.shape
    k2 = p2.shape[1]
    n = w.shape[-1]
    tm = _residual_tile(geom)
    npt = mp // tm
    assert s1.shape[0] == ROW_TILE and x.shape == (mp + ROW_TILE, n)

    pmap = lambda j, i: (jnp.minimum(i, npt - 1), 0)
    smap = lambda j, i: (0, 0)
    tile = pl.BlockSpec((tm, tn), lambda j, i: (i, j))
    body = functools.partial(_mm_pair_body, n_prompt_tiles=npt, coef=coef, gate_idx=gate_idx, ts=geom[3])
    return pl.pallas_call(
        body,
        grid=(n // tn, npt + 1),
        in_specs=[pl.BlockSpec((tm, k1), pmap), pl.BlockSpec((tm, k2), pmap),
                  pl.BlockSpec((ROW_TILE, k1), smap), pl.BlockSpec((ROW_TILE, k2), smap),
                  pl.BlockSpec((k1 + k2, tn), lambda j, i: (0, j)), tile]
        + _mod_specs(gate_layer, geom, 1, lambda j, i: j, tn, tm),
        out_specs=tile,
        out_shape=jax.ShapeDtypeStruct((mp + ROW_TILE, n), F32),
        scratch_shapes=[pltpu.VMEM((k1 + k2, tn), BF16)],
        compiler_params=_cparams("arbitrary", "arbitrary"),
        name=name,
    )(p1, p2, s1, s2, w, x, mod4, mod4)


def _mm_res_body(a_ref, w_ref, x_ref, gp_ref, gs_ref, o_ref, wb_ref, *, n_prompt_tiles, coef, gate_idx, ts):
    i = pl.program_id(1)

    @pl.when(i == 0)
    def _():
        wb_ref[...] = w_ref[...].astype(BF16)

    acc = jnp.dot(a_ref[...], wb_ref[...], preferred_element_type=F32)

    @pl.when(i < n_prompt_tiles)
    def _():
        _gated_residual(x_ref, acc, gp_ref, gs_ref, o_ref, prompt=True, coef=coef, gate_idx=gate_idx, ts=ts)

    @pl.when(i >= n_prompt_tiles)
    def _():
        _gated_residual(x_ref, acc, gp_ref, gs_ref, o_ref, prompt=False, coef=coef, gate_idx=gate_idx, ts=ts)


def matmul_residual(a, w, layer, x, mod4, geom, *, gate_layer, gate_idx, coef, tn=512, name="matmul_residual"):
    bp, tp, _, ts = geom
    mrows, k = a.shape
    n = w.shape[-1]
    tm = _residual_tile(geom)
    npt = bp * tp // tm
    tile = pl.BlockSpec((tm, tn), lambda j, i: (i, j))
    body = functools.partial(_mm_res_body, n_prompt_tiles=npt, coef=coef, gate_idx=gate_idx, ts=ts)
    return pl.pallas_call(
        body,
        grid=(n // tn, npt + 1),
        in_specs=[pl.BlockSpec((tm, k), lambda j, i: (i, 0)),
                  pl.BlockSpec((None, k, tn), lambda j, i: (layer, 0, j)), tile]
        + _mod_specs(gate_layer, geom, 1, lambda j, i: j, tn, tm),
        out_specs=tile,
        out_shape=jax.ShapeDtypeStruct((mrows, n), F32),
        scratch_shapes=[pltpu.VMEM((k, tn), BF16)],
        compiler_params=_cparams("arbitrary", "arbitrary"),
        name=name,
    )(a, w, x, mod4, mod4)


def _conv_chunk(ext_ref, raw, w_ref, cin):
    ext_ref[SUBLANES:SUBLANES + CHUNK, :] = _pad_rows(raw, CHUNK)
    acc = None
    for j in range(CONV_W):
        off = SUBLANES - (CONV_W - 1) + j
        term = ext_ref[off:off + CHUNK, :] * w_ref[j:j + 1, :]
        acc = term if acc is None else acc + term
    return acc


def _conv_init(ext_ref, c0_ref, has_state):
    ext_ref[0:SUBLANES, :] = jnp.zeros((SUBLANES, ext_ref.shape[1]), F32)
    if has_state:
        ext_ref[SUBLANES - (CONV_W - 1):SUBLANES, :] = c0_ref[0]


def _conv_tail(ext_ref, cin):
    return ext_ref[cin + SUBLANES - (CONV_W - 1):cin + SUBLANES, :]


def _conv_advance(ext_ref):
    ext_ref[0:SUBLANES, :] = ext_ref[CHUNK:CHUNK + SUBLANES, :]


def _gla_body(q_ref, k_ref, v_ref, r_ref, g_ref, wg_ref, bg_ref, nrm_ref, s0_ref, o_ref, so_ref, st_ref,
              *, cin, has_state):
    c = pl.program_id(1)
    heads = range(GLA_HEADS)

    @pl.when(c == 0)
    def _():
        for h in heads:
            if has_state:
                st_ref[h] = s0_ref[0, h].T
            else:
                st_ref[h] = jnp.zeros(st_ref.shape[1:], F32)

    q = _pad_rows(q_ref[...], CHUNK) * (GLA_DK ** -0.5)
    k = _pad_rows(k_ref[...], CHUNK)
    v = _pad_rows(v_ref[...], CHUNK)
    glr = _pad_rows(g_ref[...], CHUNK)
    x = _dot(glr, wg_ref[...]) + bg_ref[...]
    logf = jax.nn.log_sigmoid(x) * (1.0 / GLA_TAU)
    if cin < CHUNK:
        rows = lax.broadcasted_iota(jnp.int32, logf.shape, 0)
        logf = jnp.where(rows < cin, logf, 0.0)
    b = _cumsum_rows(logf)
    bmid = b[CHUNK // 2 - 1:CHUNK // 2, :]
    blast = b[CHUNK - 1:CHUNK, :]
    incl, _, _ = _tri_masks(CHUNK)
    qe = q * jnp.exp(b - bmid)
    ke = k * jnp.exp(bmid - b)
    qb = q * jnp.exp(b)
    kw = k * jnp.exp(blast - b)
    sdec = jnp.exp(blast)

    def hk(a, h):
        return a[:, h * GLA_DK:(h + 1) * GLA_DK]

    def hv(a, h):
        return a[:, h * GLA_DV:(h + 1) * GLA_DV]

    sts = [st_ref[h] for h in heads]
    atts = [jnp.where(incl, _dot_nt(hk(qe, h), hk(ke, h)), 0.0) for h in heads]
    os_ = [_dot(atts[h], hv(v, h)) + _dot_nt(hk(qb, h), sts[h]) for h in heads]
    st_news = [sts[h] * hk(sdec, h) + _dot_tn(hv(v, h), hk(kw, h)) for h in heads]
    ys = []
    for h in heads:
        st_ref[h] = st_news[h]
        o = os_[h]
        var = jnp.mean(o * o, axis=-1, keepdims=True)
        ys.append(o * lax.rsqrt(var + EPS) * nrm_ref[...])
    y = jnp.concatenate(ys, axis=1)
    o_ref[0] = y[:cin] * _silu(r_ref[...])

    @pl.when(c == pl.num_programs(1) - 1)
    def _():
        for h in heads:
            so_ref[0, h] = st_news[h].T


def gla_mixer(pm, psm, wg_pad, b_gate, nrm, s0, *, row0, nb, t):
    cin = min(t, CHUNK)
    nc = t // cin
    rb0 = row0 // cin
    has_state = s0 is not None
    if s0 is None:
        s0 = jnp.zeros((1, GLA_HEADS, GLA_DK, GLA_DV), F32)
        s0_map = lambda b, c: (0, 0, 0, 0)
    else:
        s0_map = lambda b, c: (b, 0, 0, 0)

    def rmap(col0, width):
        return lambda b, c: (rb0 + b * nc + c, col0 // width)

    const = lambda b, c: (0, 0)
    body = functools.partial(_gla_body, cin=cin, has_state=has_state)
    return pl.pallas_call(
        body,
        grid=(nb, nc),
        in_specs=[pl.BlockSpec((cin, GLA_QK), rmap(EVA_Q, GLA_QK)),
                  pl.BlockSpec((cin, GLA_QK), rmap(EVA_K, GLA_QK)),
                  pl.BlockSpec((cin, GLA_VW), rmap(EVA_V, GLA_VW)),
                  pl.BlockSpec((cin, GLA_VW), rmap(EVA_R, GLA_VW)),
                  pl.BlockSpec((cin, LANES), lambda b, c: (rb0 + b * nc + c, 0)),
                  pl.BlockSpec((LANES, GLA_QK), const),
                  pl.BlockSpec((1, GLA_QK), const),
                  pl.BlockSpec((1, GLA_DV), const),
                  pl.BlockSpec((1, GLA_HEADS, GLA_DK, GLA_DV), s0_map)],
        out_specs=[pl.BlockSpec((1, cin, GLA_VW), lambda b, c: (b, c, 0)),
                   pl.BlockSpec((1, GLA_HEADS, GLA_DK, GLA_DV), lambda b, c: (b, 0, 0, 0))],
        out_shape=[jax.ShapeDtypeStruct((nb, t, GLA_VW), F32),
                   jax.ShapeDtypeStruct((nb, GLA_HEADS, GLA_DK, GLA_DV), F32)],
        scratch_shapes=[pltpu.VMEM((GLA_HEADS, GLA_DV, GLA_DK), F32)],
        compiler_params=_cparams("arbitrary", "arbitrary"),
        name="gla_mixer",
    )(pm, pm, pm, pm, psm, wg_pad, b_gate.reshape(1, GLA_QK), nrm.reshape(1, GLA_DV), s0)


def _ssd_body(z_ref, x_ref, b_ref, c_ref, sm_ref, cwx_ref, cwb_ref, cwc_ref, cbx_ref, cbb_ref, cbc_ref,
              dtb_ref, alog_ref, dexp_ref, nrm_ref, e_ref, s0_ref, c0x_ref, c0b_ref, c0c_ref,
              y_ref, so_ref, cox_ref, cob_ref, coc_ref,
              st_ref, ex_ref, eb_ref, ec_ref, *, cin, has_state):
    c = pl.program_id(1)

    @pl.when(c == 0)
    def _():
        if has_state:
            st_ref[...] = s0_ref[0].T
        else:
            st_ref[...] = jnp.zeros(st_ref.shape, F32)
        _conv_init(ex_ref, c0x_ref, has_state)
        _conv_init(eb_ref, c0b_ref, has_state)
        _conv_init(ec_ref, c0c_ref, has_state)

    xs = _silu(_conv_chunk(ex_ref, x_ref[...], cwx_ref, cin) + cbx_ref[...])
    bm = _silu(_conv_chunk(eb_ref, b_ref[...], cwb_ref, cin) + cbb_ref[...])
    cm = _silu(_conv_chunk(ec_ref, c_ref[...], cwc_ref, cin) + cbc_ref[...])

    dt = jax.nn.softplus(_pad_rows(sm_ref[...], CHUNK) + dtb_ref[...])
    if cin < CHUNK:
        rows = lax.broadcasted_iota(jnp.int32, dt.shape, 0)
        dt = jnp.where(rows < cin, dt, 0.0)
    a = dt * (-jnp.exp(alog_ref[...]))
    cum = _cumsum_rows(a)
    cum_t = _rows_of(cum, SSD_DT_LANE0 + SSD_HEADS)
    last = cum[CHUNK - 1:CHUNK, :]
    stacked = jnp.concatenate([dt, jnp.exp(cum), jnp.exp(last - cum)], axis=0)
    ex = _dot_exact_rhs(stacked, e_ref[...])
    dt_e, ecum_e, w_e = ex[:CHUNK], ex[CHUNK:2 * CHUNK], ex[2 * CHUNK:]
    xdt = xs * dt_e
    xw = xdt * w_e
    incl, _, _ = _tri_masks(CHUNK)
    lane = lax.broadcasted_iota(jnp.int32, (CHUNK, LANES), 1)
    st = st_ref[...]
    gw = SSD_HPG * SSD_P
    groups = range(SSD_GROUPS)
    pairs = SSD_HPG // 2

    def grp(a, g, w):
        return a[:, g * w:(g + 1) * w]

    cbs = [_dot_nt(grp(cm, g, SSD_STATE), grp(bm, g, SSD_STATE)) for g in groups]
    inters = [_dot(grp(cm, g, SSD_STATE), grp(st, g, gw)) for g in groups]
    st_adds = [_dot_tn(grp(bm, g, SSD_STATE), grp(xw, g, gw)) for g in groups]
    att_pairs = []
    for g in groups:
        for pr in range(pairs):
            atts = []
            for h in (g * SSD_HPG + 2 * pr, g * SSD_HPG + 2 * pr + 1):
                ln = SSD_DT_LANE0 + h
                seg = cum[:, ln:ln + 1] - cum_t[ln:ln + 1, :]
                atts.append(cbs[g] * jnp.exp(jnp.where(incl, seg, NEG_BIG)))
            att_pairs.append(jnp.concatenate(atts, axis=0))
    yys = [_dot(att_pairs[p], xdt[:, p * LANES:(p + 1) * LANES]) for p in range(SSD_GROUPS * pairs)]
    ys = [jnp.where(lane < SSD_P, yy[:CHUNK], yy[CHUNK:]) for yy in yys]
    for g in groups:
        st_ref[:, g * gw:(g + 1) * gw] = grp(st, g, gw) * grp(ecum_e, g, gw)[CHUNK - 1:CHUNK] + st_adds[g]
    y = (jnp.concatenate(ys, axis=1) + jnp.concatenate(inters, axis=1) * ecum_e + xs * dexp_ref[...])
    y = y[:cin] * _silu(z_ref[...])
    var = jnp.mean(y * y, axis=-1, keepdims=True)
    y_ref[0] = y * lax.rsqrt(var + EPS) * nrm_ref[...]

    @pl.when(c == pl.num_programs(1) - 1)
    def _():
        so_ref[0] = st_ref[...].T
        cox_ref[0] = _conv_tail(ex_ref, cin)
        cob_ref[0] = _conv_tail(eb_ref, cin)
        coc_ref[0] = _conv_tail(ec_ref, cin)

    if cin == CHUNK:
        _conv_advance(ex_ref)
        _conv_advance(eb_ref)
        _conv_advance(ec_ref)


def ssd_mixer(pm, psm, conv_w, conv_b, dtb_pad, alog_pad, d_exp, nrm, expand, s0, c0, *, row0, nb, t):
    cin = min(t, CHUNK)
    nc = t // cin
    rb0 = row0 // cin
    has_state = s0 is not None
    hp = SSD_INNER
    if s0 is None:
        s0 = jnp.zeros((1, hp, SSD_STATE), F32)
        c0 = jnp.zeros((1, CONV_W - 1, SSD_CONV_DIM), F32)
        bmap = lambda b, c: 0
    else:
        s0 = s0.reshape(nb, hp, SSD_STATE)
        bmap = lambda b, c: b

    def rmap(col0, width):
        return lambda b, c: (rb0 + b * nc + c, col0 // width)

    const = lambda b, c: (0, 0)
    kconv = CONV_W - 1
    body = functools.partial(_ssd_body, cin=cin, has_state=has_state)
    outs = pl.pallas_call(
        body,
        grid=(nb, nc),
        in_specs=[pl.BlockSpec((cin, hp), rmap(EVB_Z, hp)),
                  pl.BlockSpec((cin, hp), rmap(EVB_X, hp)),
                  pl.BlockSpec((cin, SSD_BC), rmap(EVB_B, SSD_BC)),
                  pl.BlockSpec((cin, SSD_BC), rmap(EVB_C, SSD_BC)),
                  pl.BlockSpec((cin, LANES), lambda b, c: (rb0 + b * nc + c, 0)),
                  pl.BlockSpec((CONV_W, hp), lambda b, c: (0, 0)),
                  pl.BlockSpec((CONV_W, SSD_BC), lambda b, c: (0, hp // SSD_BC)),
                  pl.BlockSpec((CONV_W, SSD_BC), lambda b, c: (0, hp // SSD_BC + 1)),
                  pl.BlockSpec((1, hp), lambda b, c: (0, 0)),
                  pl.BlockSpec((1, SSD_BC), lambda b, c: (0, hp // SSD_BC)),
                  pl.BlockSpec((1, SSD_BC), lambda b, c: (0, hp // SSD_BC + 1)),
                  pl.BlockSpec((1, LANES), const),
                  pl.BlockSpec((1, LANES), const),
                  pl.BlockSpec((1, hp), const),
                  pl.BlockSpec((1, hp), const),
                  pl.BlockSpec((LANES, hp), const),
                  pl.BlockSpec((1, hp, SSD_STATE), lambda b, c: (bmap(b, c), 0, 0)),
                  pl.BlockSpec((1, kconv, hp), lambda b, c: (bmap(b, c), 0, 0)),
                  pl.BlockSpec((1, kconv, SSD_BC), lambda b, c: (bmap(b, c), 0, hp // SSD_BC)),
                  pl.BlockSpec((1, kconv, SSD_BC), lambda b, c: (bmap(b, c), 0, hp // SSD_BC + 1))],
        out_specs=[pl.BlockSpec((1, cin, hp), lambda b, c: (b, c, 0)),
                   pl.BlockSpec((1, hp, SSD_STATE), lambda b, c: (b, 0, 0)),
                   pl.BlockSpec((1, kconv, hp), lambda b, c: (b, 0, 0)),
                   pl.BlockSpec((1, kconv, SSD_BC), lambda b, c: (b, 0, 0)),
                   pl.BlockSpec((1, kconv, SSD_BC), lambda b, c: (b, 0, 0))],
        out_shape=[jax.ShapeDtypeStruct((nb, t, hp), F32),
                   jax.ShapeDtypeStruct((nb, hp, SSD_STATE), F32),
                   jax.ShapeDtypeStruct((nb, kconv, hp), F32),
                   jax.ShapeDtypeStruct((nb, kconv, SSD_BC), F32),
                   jax.ShapeDtypeStruct((nb, kconv, SSD_BC), F32)],
        scratch_shapes=[pltpu.VMEM((SSD_STATE, hp), F32),
                        pltpu.VMEM((CHUNK + SUBLANES, hp), F32),
                        pltpu.VMEM((CHUNK + SUBLANES, SSD_BC), F32),
                        pltpu.VMEM((CHUNK + SUBLANES, SSD_BC), F32)],
        compiler_params=_cparams("arbitrary", "arbitrary"),
        name="ssd_mixer",
    )(pm, pm, pm, pm, psm, conv_w, conv_w, conv_w, conv_b, conv_b, conv_b, dtb_pad, alog_pad, d_exp,
      nrm.reshape(1, hp), expand, s0, c0, c0, c0)
    y, s_new, cx, cb_, cc = outs
    conv_s = jnp.concatenate([cx, cb_, cc], axis=-1)
    return y, s_new.reshape(nb, SSD_HEADS, SSD_P, SSD_STATE), conv_s


def _l2norm_heads(x, n_heads, width, scale):
    outs = []
    for e in range(n_heads):
        xe = x[:, e * width:(e + 1) * width]
        ss = jnp.sum(xe * xe, axis=-1, keepdims=True)
        outs.append(xe * (lax.rsqrt(ss + EPS) * scale))
    return outs


def _unit_lower_inverse(a_mats, eye, order):
    def mm(a, b):
        return jnp.dot(a, b, preferred_element_type=F32)

    ts = [eye - a for a in a_mats]
    pbs = [(-a).astype(BF16) for a in a_mats]
    for _ in range(max(order - 1, 1).bit_length() - 1):
        ps = [mm(pb, pb) for pb in pbs]
        pbs = [p.astype(BF16) for p in ps]
        ts = [t + mm(t.astype(BF16), pb) for t, pb in zip(ts, pbs)]
    outs = []
    splits = [(_split2(a), _split2(t)) for a, t in zip(a_mats, ts)]
    ats = [mm(ah, th) + mm(ah, tl) + mm(al, th) for (ah, al), (th, tl) in splits]
    for t, at, (_, (th, _)) in zip(ts, ats, splits):
        resid = eye - t - at
        outs.append(t + mm(th, resid.astype(BF16)))
    return outs


def _gdn_body(q_ref, k_ref, v_ref, z_ref, sm_ref, cwq_ref, cwk_ref, cwv_ref, alog_ref, dtb_ref, nrm_ref,
              s0_ref, c0q_ref, c0k_ref, c0v_ref,
              o_ref, so_ref, coq_ref, cok_ref, cov_ref,
              s_ref, eq_ref, ek_ref, ev_ref, *, cin, has_state):
    c = pl.program_id(1)
    n = GDN_PACK * CHUNK

    @pl.when(c == 0)
    def _():
        if has_state:
            s_ref[...] = s0_ref[0]
        else:
            s_ref[...] = jnp.zeros(s_ref.shape, F32)
        _conv_init(eq_ref, c0q_ref, has_state)
        _conv_init(ek_ref, c0k_ref, has_state)
        _conv_init(ev_ref, c0v_ref, has_state)

    qc = _silu(_conv_chunk(eq_ref, q_ref[...], cwq_ref, cin))
    kc = _silu(_conv_chunk(ek_ref, k_ref[...], cwk_ref, cin))
    vc = _silu(_conv_chunk(ev_ref, v_ref[...], cwv_ref, cin))
    qs = _l2norm_heads(qc, GDN_QK_HEADS, GDN_DK, GDN_DK ** -0.5)
    ks = _l2norm_heads(kc, GDN_QK_HEADS, GDN_DK, 1.0)

    sm = _pad_rows(sm_ref[...], CHUNK)
    beta = jax.nn.sigmoid(sm)
    gl = -jnp.exp(alog_ref[...]) * jax.nn.softplus(sm + dtb_ref[...])
    if cin < CHUNK:
        rows = lax.broadcasted_iota(jnp.int32, sm.shape, 0)
        beta = jnp.where(rows < cin, beta, 0.0)
        gl = jnp.where(rows < cin, gl, 0.0)
    cum = _cumsum_rows(gl)
    last_row = cum[CHUNK - 1:CHUNK, :]

    rr = lax.broadcasted_iota(jnp.int32, (n, n), 0)
    cc = lax.broadcasted_iota(jnp.int32, (n, n), 1)
    same = (rr // CHUNK) == (cc // CHUNK)
    incl = same & (rr >= cc)
    strict = same & (rr > cc)
    eye = jnp.where(rr == cc, 1.0, 0.0)
    head_of_row = lax.broadcasted_iota(jnp.int32, (n, LANES), 0) // CHUNK
    lane = lax.broadcasted_iota(jnp.int32, (n, LANES), 1)
    bd_mask = (lax.broadcasted_iota(jnp.int32, (n, GDN_PACK * GDN_DK), 0) // CHUNK
               == lax.broadcasted_iota(jnp.int32, (n, GDN_PACK * GDN_DK), 1) // GDN_DK)
    srow = lax.broadcasted_iota(jnp.int32, (GDN_PACK * GDN_DK, LANES), 0) // GDN_DK
    slane = lax.broadcasted_iota(jnp.int32, (GDN_PACK * GDN_DK, LANES), 1)

    def block_diag(x):
        return jnp.where(bd_mask, jnp.concatenate([x] * GDN_PACK, axis=1), 0.0)

    def stack(parts):
        return jnp.concatenate(parts, axis=0)

    cum_rep = stack([cum] * GDN_PACK)
    beta_rep = stack([beta] * GDN_PACK)
    last_rep = jnp.broadcast_to(last_row, (n, LANES))
    cum_t = _rows_of(cum, 2 * GDN_V_HEADS)
    packs = range(GDN_V_HEADS // GDN_PACK)

    pre = []
    for j in packs:
        h0 = j * GDN_PACK
        heads = range(h0, h0 + GDN_PACK)
        k_st = stack([ks[i // GDN_REP] for i in heads])
        q_st = stack([qs[i // GDN_REP] for i in heads])
        v_st = stack([vc[:, i * GDN_DV:(i + 1) * GDN_DV] for i in heads])
        g_sel = lane == GDN_V_HEADS + h0 + head_of_row
        col = jnp.sum(jnp.where(g_sel, cum_rep, 0.0), axis=1, keepdims=True)
        rowv = jnp.concatenate([cum_t[GDN_V_HEADS + i:GDN_V_HEADS + i + 1, :] for i in heads], axis=1)
        bcol = jnp.sum(jnp.where(lane == h0 + head_of_row, beta_rep, 0.0), axis=1, keepdims=True)
        lastc = jnp.sum(jnp.where(g_sel, last_rep, 0.0), axis=1, keepdims=True)
        dec = jnp.exp(jnp.where(incl, col - rowv, NEG_BIG))
        ecol = jnp.exp(col)
        s_st = s_ref[h0:h0 + GDN_PACK].reshape(GDN_PACK * GDN_DK, GDN_DV)
        pre.append(dict(k=k_st, q=q_st, v=v_st, col=col, bcol=bcol, lastc=lastc, dec=dec, ecol=ecol, s=s_st))
    kks = [_dot_nt(p["k"], p["k"]) for p in pre]
    qks = [_dot_nt(p["q"], p["k"]) for p in pre]
    a_mats = [jnp.where(strict, p["bcol"] * kk * p["dec"], 0.0) for p, kk in zip(pre, kks)]
    t_invs = _unit_lower_inverse(a_mats, eye, cin)
    rhss = [jnp.concatenate([p["v"] * p["bcol"], p["k"] * (p["bcol"] * p["ecol"])], axis=1) for p in pre]
    sols = [_dot(t, r) for t, r in zip(t_invs, rhss)]
    deltas = [sol[:, :GDN_DV] - _dot(block_diag(sol[:, GDN_DV:]), p["s"]) for sol, p in zip(sols, pre)]
    o_sts = [_dot(block_diag(p["q"] * p["ecol"]), p["s"]) + _dot(jnp.where(incl, qk * p["dec"], 0.0), dl)
             for p, qk, dl in zip(pre, qks, deltas)]
    outs = []
    for j, (p, dl, o_st) in enumerate(zip(pre, deltas, o_sts)):
        h0 = j * GDN_PACK
        kw_bd = block_diag(p["k"] * jnp.exp(p["lastc"] - p["col"]))
        s_dec = jnp.exp(jnp.sum(jnp.where(slane == GDN_V_HEADS + h0 + srow,
                                          jnp.broadcast_to(last_row, srow.shape), 0.0), axis=1, keepdims=True))
        s_new = p["s"] * s_dec + _dot_tn(kw_bd, dl)
        s_ref[h0:h0 + GDN_PACK] = s_new.reshape(GDN_PACK, GDN_DK, GDN_DV)
        var = jnp.mean(o_st * o_st, axis=-1, keepdims=True)
        y_st = o_st * lax.rsqrt(var + EPS) * nrm_ref[...]
        outs += [y_st[i * CHUNK:(i + 1) * CHUNK] for i in range(GDN_PACK)]
    o = jnp.concatenate(outs, axis=1)
    o_ref[0] = o[:cin] * _silu(z_ref[...])

    @pl.when(c == pl.num_programs(1) - 1)
    def _():
        so_ref[0] = s_ref[...]
        coq_ref[0] = _conv_tail(eq_ref, cin)
        cok_ref[0] = _conv_tail(ek_ref, cin)
        cov_ref[0] = _conv_tail(ev_ref, cin)

    if cin == CHUNK:
        _conv_advance(eq_ref)
        _conv_advance(ek_ref)
        _conv_advance(ev_ref)


def gdn_mixer(pm, psm, conv_w, alog_pad, dtb_pad, nrm, s0, c0, *, row0, nb, t):
    cin = min(t, CHUNK)
    nc = t // cin
    rb0 = row0 // cin
    has_state = s0 is not None
    kconv = CONV_W - 1
    if s0 is None:
        s0 = jnp.zeros((1, GDN_V_HEADS, GDN_DK, GDN_DV), F32)
        c0 = jnp.zeros((1, kconv, GDN_CONV_DIM), F32)
        bmap = lambda b: 0
    else:
        bmap = lambda b: b

    def rmap(col0, width):
        return lambda b, c: (rb0 + b * nc + c, col0 // width)

    const = lambda b, c: (0, 0)
    body = functools.partial(_gdn_body, cin=cin, has_state=has_state)
    outs = pl.pallas_call(
        body,
        grid=(nb, nc),
        in_specs=[pl.BlockSpec((cin, GDN_QK), rmap(ODA_Q, GDN_QK)),
                  pl.BlockSpec((cin, GDN_QK), rmap(ODA_K, GDN_QK)),
                  pl.BlockSpec((cin, GDN_VW), rmap(ODA_V, GDN_VW)),
                  pl.BlockSpec((cin, GDN_VW), rmap(ODA_Z, GDN_VW)),
                  pl.BlockSpec((cin, LANES), lambda b, c: (rb0 + b * nc + c, 0)),
                  pl.BlockSpec((CONV_W, GDN_QK), lambda b, c: (0, 0)),
                  pl.BlockSpec((CONV_W, GDN_QK), lambda b, c: (0, 1)),
                  pl.BlockSpec((CONV_W, GDN_VW), lambda b, c: (0, 1)),
                  pl.BlockSpec((1, LANES), const),
                  pl.BlockSpec((1, LANES), const),
                  pl.BlockSpec((1, GDN_DV), const),
                  pl.BlockSpec((1, GDN_V_HEADS, GDN_DK, GDN_DV), lambda b, c: (bmap(b), 0, 0, 0)),
                  pl.BlockSpec((1, kconv, GDN_QK), lambda b, c: (bmap(b), 0, 0)),
                  pl.BlockSpec((1, kconv, GDN_QK), lambda b, c: (bmap(b), 0, 1)),
                  pl.BlockSpec((1, kconv, GDN_VW), lambda b, c: (bmap(b), 0, 1))],
        out_specs=[pl.BlockSpec((1, cin, GDN_VW), lambda b, c: (b, c, 0)),
                   pl.BlockSpec((1, GDN_V_HEADS, GDN_DK, GDN_DV), lambda b, c: (b, 0, 0, 0)),
                   pl.BlockSpec((1, kconv, GDN_QK), lambda b, c: (b, 0, 0)),
                   pl.BlockSpec((1, kconv, GDN_QK), lambda b, c: (b, 0, 0)),
                   pl.BlockSpec((1, kconv, GDN_VW), lambda b, c: (b, 0, 0))],
        out_shape=[jax.ShapeDtypeStruct((nb, t, GDN_VW), F32),
                   jax.ShapeDtypeStruct((nb, GDN_V_HEADS, GDN_DK, GDN_DV), F32),
                   jax.ShapeDtypeStruct((nb, kconv, GDN_QK), F32),
                   jax.ShapeDtypeStruct((nb, kconv, GDN_QK), F32),
                   jax.ShapeDtypeStruct((nb, kconv, GDN_VW), F32)],
        scratch_shapes=[pltpu.VMEM((GDN_V_HEADS, GDN_DK, GDN_DV), F32),
                        pltpu.VMEM((CHUNK + SUBLANES, GDN_QK), F32),
                        pltpu.VMEM((CHUNK + SUBLANES, GDN_QK), F32),
                        pltpu.VMEM((CHUNK + SUBLANES, GDN_VW), F32)],
        compiler_params=_cparams("arbitrary", "arbitrary"),
        name="gdn_mixer",
    )(pm, pm, pm, pm, psm, conv_w, conv_w, conv_w, alog_pad, dtb_pad, nrm.reshape(1, GDN_DV),
      s0, c0, c0, c0)
    o, s_new, cq, ck, cv = outs
    return o, s_new, jnp.concatenate([cq, ck, cv], axis=-1)


def _residue_perm(dil, inverse=False):
    n = ROW_TILE // dil
    i = lax.broadcasted_iota(jnp.int32, (ROW_TILE, ROW_TILE), 0)
    j = lax.broadcasted_iota(jnp.int32, (ROW_TILE, ROW_TILE), 1)
    if inverse:
        i, j = j, i
    return jnp.where(j == (i % n) * dil + i // n, 1.0, 0.0).astype(BF16)


def _rope_body(q_ref, k_ref, v_ref, inv_ref, qo_ref, ko_ref, *rest, n_prompt_tiles, tiles_per_seq, ts):
    i = pl.program_id(0)
    rt = q_ref.shape[0]
    r = lax.broadcasted_iota(jnp.int32, (rt, LANES), 0)
    pos_prompt = (i % tiles_per_seq) * rt + r
    pos_sample = PAST_LEN + r % ts
    pos = jnp.where(i < n_prompt_tiles, pos_prompt, pos_sample).astype(F32)
    ang = pos * inv_ref[...]
    cos = jnp.cos(ang)
    sin = jnp.sin(ang)
    lane = lax.broadcasted_iota(jnp.int32, (rt, LANES), 1)
    sin = jnp.where(lane < LANES // 2, -sin, sin)
    for src, dst in ((q_ref, qo_ref), (k_ref, ko_ref)):
        for h in range(DSA_W // DSA_HD):
            x = src[:, h * DSA_HD:(h + 1) * DSA_HD]
            dst[:, h * DSA_HD:(h + 1) * DSA_HD] = x * cos + pltpu.roll(x, DSA_HD // 2, 1) * sin
    for g, (_, dil) in enumerate(DSA_GROUPS):
        perm = None if dil == 1 else _residue_perm(dil)
        for src, dst in zip((qo_ref, ko_ref, v_ref), rest[3 * g:3 * g + 3]):
            xb = src[:, g * DSA_GW:(g + 1) * DSA_GW].astype(BF16)
            if perm is not None:
                xb = jnp.dot(perm, xb, preferred_element_type=F32).astype(BF16)
            dst[...] = xb.reshape(dst.shape)


def rope_qk(pm, geom):
    bp, tp, bs, ts = geom
    mrows = pm.shape[0]
    half = DSA_HD // 2
    inv = ROPE_THETA ** (-jnp.arange(half, dtype=F32) / half)
    inv2 = jnp.concatenate([inv, inv]).reshape(1, DSA_HD)
    body = functools.partial(_rope_body, n_prompt_tiles=bp * tp // ROW_TILE, tiles_per_seq=tp // ROW_TILE, ts=ts)
    spec_o = pl.BlockSpec((ROW_TILE, DSA_W), lambda i: (i, 0))
    out_specs = [spec_o, spec_o]
    out_shape = [jax.ShapeDtypeStruct((mrows, DSA_W), F32)] * 2
    for _, dil in DSA_GROUPS:
        n = ROW_TILE // dil
        out_specs += [pl.BlockSpec((dil, n, DSA_GW), lambda i: (0, i, 0))] * 3
        out_shape += [jax.ShapeDtypeStruct((dil, mrows // dil, DSA_GW), BF16)] * 3
    outs = pl.pallas_call(
        body,
        grid=(mrows // ROW_TILE,),
        in_specs=[pl.BlockSpec((ROW_TILE, DSA_W), lambda i: (i, ODB_DQ // DSA_W)),
                  pl.BlockSpec((ROW_TILE, DSA_W), lambda i: (i, ODB_DK // DSA_W)),
                  pl.BlockSpec((ROW_TILE, DSA_W), lambda i: (i, ODB_DV // DSA_W)),
                  pl.BlockSpec((1, DSA_HD), lambda i: (0, 0))],
        out_specs=out_specs,
        out_shape=out_shape,
        compiler_params=_cparams("arbitrary"),
        name="rope_qk",
    )(pm, pm, pm, inv2)
    return outs[0], outs[1], [tuple(outs[2 + 3 * g:5 + 3 * g]) for g in range(len(DSA_GROUPS))]


def _lse_lanes(lses):
    rows = lses[0].shape[0]
    lane = lax.broadcasted_iota(jnp.int32, (rows, LANES), 1)
    out = jnp.zeros((rows, LANES), F32)
    for h, v in enumerate(lses):
        out = jnp.where(lane == h, v, out)
    return out


def _dsa_prompt_body(q_ref, kp_ref, kc_ref, vp_ref, vc_ref, o_ref, l_ref, *, jmax):
    n = pl.program_id(2)
    blk = DSA_BLOCK
    i_idx = lax.broadcasted_iota(jnp.int32, (blk, 2 * blk), 0)
    m_idx = lax.broadcasted_iota(jnp.int32, (blk, 2 * blk), 1)
    dist = i_idx + blk - m_idx
    valid = (dist >= 0) & (dist <= jmax) & ((m_idx >= blk) | (n > 0))
    scale = DSA_HD ** -0.5
    lses = []
    for h in range(DSA_HEADS):
        sl = slice(h * DSA_HD, (h + 1) * DSA_HD)
        kk = jnp.concatenate([kp_ref[:, sl], kc_ref[:, sl]], axis=0)
        vv = jnp.concatenate([vp_ref[:, sl], vc_ref[:, sl]], axis=0)
        s = _dot_nt(q_ref[:, sl], kk) * scale
        s = jnp.where(valid, s, NEG_BIG)
        mx = jnp.max(s, axis=-1, keepdims=True)
        p = jnp.exp(s - mx)
        den = jnp.sum(p, axis=-1, keepdims=True)
        o_ref[:, sl] = (_dot(p, vv) / den).astype(o_ref.dtype)
        lses.append(mx + jnp.log(den))
    l_ref[...] = _lse_lanes(lses)


def dsa_prompt(qkv, g, geom):
    bp, tp, _, _ = geom
    window, dil = DSA_GROUPS[g]
    ls = tp // dil
    nblk = ls // DSA_BLOCK
    q, k, v = qkv
    mu = bp * ls

    cur = lambda b, r, n: (r, b * nblk + n, 0)
    prev = lambda b, r, n: (r, b * nblk + jnp.maximum(n - 1, 0), 0)
    blk = (None, DSA_BLOCK, DSA_GW)
    return pl.pallas_call(
        functools.partial(_dsa_prompt_body, jmax=window // dil),
        grid=(bp, dil, nblk),
        in_specs=[pl.BlockSpec(blk, cur), pl.BlockSpec(blk, prev), pl.BlockSpec(blk, cur),
                  pl.BlockSpec(blk, prev), pl.BlockSpec(blk, cur)],
        out_specs=[pl.BlockSpec(blk, cur), pl.BlockSpec((None, DSA_BLOCK, LANES), cur)],
        out_shape=[jax.ShapeDtypeStruct((dil, mu, DSA_GW), BF16),
                   jax.ShapeDtypeStruct((dil, mu, LANES), F32)],
        compiler_params=_cparams("arbitrary", "arbitrary", "arbitrary"),
        name=f"dsa_prompt_w{window}",
    )(q, k, k, v, v)


def _dsa_sample_body(q_ref, kn_ref, vn_ref, cache_ref, o_ref, l_ref, m_scr, d_scr, acc_scr, *, window, dil):
    j = pl.program_id(1)
    nj = pl.num_programs(1)
    ts = q_ref.shape[0]
    nh = DSA_HEADS
    scale = DSA_HD ** -0.5

    def heads_to_rows(ref):
        return jnp.concatenate([ref[:, h * DSA_HD:(h + 1) * DSA_HD] for h in range(nh)], axis=0)

    @pl.when(j == 0)
    def _():
        m_scr[...] = jnp.full(m_scr.shape, NEG_BIG, F32)
        d_scr[...] = jnp.zeros(d_scr.shape, F32)
        acc_scr[...] = jnp.zeros(acc_scr.shape, F32)

    def update(s, vals, ok):
        s = jnp.where(ok, s, NEG_BIG)
        m_old = m_scr[...]
        m_new = jnp.maximum(m_old, jnp.max(s, axis=-1, keepdims=True))
        alpha = jnp.exp(m_old - m_new)
        p = jnp.where(ok, jnp.exp(s - m_new[:, 0:1]), 0.0)
        d_scr[...] = alpha * d_scr[...] + jnp.sum(p, axis=-1, keepdims=True)
        acc_scr[...] = alpha * acc_scr[...] + _dot(p, vals)
        m_scr[...] = m_new

    q_rows = heads_to_rows(q_ref) * scale
    ub, n_res = cache_ref.shape[1], cache_ref.shape[2] // (2 * nh)
    lb = ub * n_res
    x4 = cache_ref[0].reshape(lb, 2, nh, DSA_HD)
    k_all = x4[:, 0].reshape(lb * nh, DSA_HD)
    v_all = x4[:, 1].reshape(lb * nh, DSA_HD)
    r = lax.broadcasted_iota(jnp.int32, (nh * ts, lb * nh), 0)
    c = lax.broadcasted_iota(jnp.int32, (nh * ts, lb * nh), 1)
    pos = c // nh
    dist = window + r % ts - ((j * ub + pos // n_res) * dil + pos % n_res)
    ok = (r // ts == c % nh) & ((dist & (dil - 1)) == 0) & (dist <= window)
    update(_dot_nt(q_rows, k_all), v_all, ok)

    @pl.when(j == nj - 1)
    def _():
        r2 = lax.broadcasted_iota(jnp.int32, (nh * ts, nh * ts), 0)
        c2 = lax.broadcasted_iota(jnp.int32, (nh * ts, nh * ts), 1)
        d2 = r2 % ts - c2 % ts
        ok2 = (r2 // ts == c2 // ts) & (d2 >= 0) & ((d2 & (dil - 1)) == 0)
        update(_dot_nt(q_rows, heads_to_rows(kn_ref)), heads_to_rows(vn_ref), ok2)
        o_rows = acc_scr[...] / d_scr[...]
        lse = m_scr[...] + jnp.log(d_scr[...])
        for h in range(nh):
            o_ref[:, h * DSA_HD:(h + 1) * DSA_HD] = o_rows[h * ts:(h + 1) * ts]
        l_ref[...] = _lse_lanes([lse[h * ts:(h + 1) * ts, 0:1] for h in range(nh)])


def dsa_sample(rq, rk, pm, cache, g, geom):
    bp, tp, bs, ts = geom
    window, dil = DSA_GROUPS[g]
    lcache = cache.shape[1]
    assert lcache == window and window % dil == 0, "cache must hold exactly the window"
    n_res = min(dil, ts)
    strides = lcache // dil
    ub = min(strides, 512 // n_res)
    rb0 = bp * tp // ts
    kv_rows = 2 * DSA_HEADS
    cache4 = cache.reshape(bs, strides, dil * kv_rows, DSA_HD)
    blk = (ts, DSA_GW)
    body = functools.partial(_dsa_sample_body, window=window, dil=dil)
    return pl.pallas_call(
        body,
        grid=(bs, strides // ub),
        in_specs=[pl.BlockSpec(blk, lambda b, j: (rb0 + b, g)),
                  pl.BlockSpec(blk, lambda b, j: (rb0 + b, g)),
                  pl.BlockSpec(blk, lambda b, j: (rb0 + b, ODB_DV // DSA_GW + g)),
                  pl.BlockSpec((1, ub, n_res * kv_rows, DSA_HD), lambda b, j: (b, j, 0, 0))],
        out_specs=[pl.BlockSpec(blk, lambda b, j: (b, 0)), pl.BlockSpec((ts, LANES), lambda b, j: (b, 0))],
        out_shape=[jax.ShapeDtypeStruct((bs * ts, DSA_GW), F32),
                   jax.ShapeDtypeStruct((bs * ts, LANES), F32)],
        scratch_shapes=[pltpu.VMEM((DSA_HEADS * ts, DSA_HD), F32)] * 3,
        compiler_params=_cparams("arbitrary", "arbitrary"),
        name=f"dsa_sample_w{window}",
    )(rq, rk, pm, cache4)


def _dsa_merge_body(o0, o1, o2, l0, l1, l2, e_ref, out_ref, *, dils):
    os_, ls_ = [], []
    for o_ref, l_ref, dil in zip((o0, o1, o2), (l0, l1, l2), dils):
        o = o_ref[...].reshape(ROW_TILE, DSA_GW)
        l = l_ref[...].reshape(ROW_TILE, LANES)
        if dil > 1:
            pinv = _residue_perm(dil, inverse=True)
            o = jnp.dot(pinv, o.astype(BF16), preferred_element_type=F32)
            l = _dot_exact_lhs(pinv, l)
        os_.append(o.astype(F32))
        ls_.append(l)
    mx = jnp.maximum(jnp.maximum(ls_[0], ls_[1]), ls_[2])
    ws = [jnp.exp(l - mx) for l in ls_]
    tot = ws[0] + ws[1] + ws[2]
    acc = None
    for w, o in zip(ws, os_):
        term = _dot_exact_rhs(w / tot, e_ref[...]) * o
        acc = term if acc is None else acc + term
    out_ref[...] = acc


def dsa_merge(os_, ls_, dils, n_tiles):
    in_specs = []
    for width in (DSA_GW, LANES):
        for dil in dils:
            if dil:
                in_specs.append(pl.BlockSpec((dil, ROW_TILE // dil, width), lambda i: (0, i, 0)))
            else:
                in_specs.append(pl.BlockSpec((ROW_TILE, width), lambda i: (i, 0)))
    in_specs.append(pl.BlockSpec((LANES, DSA_GW), lambda i: (0, 0)))
    expand = jnp.asarray((np.arange(LANES)[:, None] == (np.arange(DSA_GW)[None, :] // DSA_HD)).astype(np.float32))
    return pl.pallas_call(
        functools.partial(_dsa_merge_body, dils=tuple(max(d, 1) for d in dils)),
        grid=(n_tiles,),
        in_specs=in_specs,
        out_specs=pl.BlockSpec((ROW_TILE, DSA_GW), lambda i: (i, 0)),
        out_shape=jax.ShapeDtypeStruct((n_tiles * ROW_TILE, DSA_GW), F32),
        compiler_params=_cparams("arbitrary"),
        name="dsa_merge",
    )(*os_, *ls_, expand)


def _pad_lanes(vec, offset=0):
    return jnp.zeros((1, LANES), F32).at[0, offset:offset + vec.shape[0]].set(vec)


def kernel(x_prompt, x_sample, c_prompt, c_sample, state_gla, state_ssd, state_ssd_conv, state_gdn, state_gdn_conv, cache_dsa_w128, cache_dsa_w512, cache_dsa_w2048, ln_ffn1, ln_mix, ln_ffn2, w_ada, b_ada, ffn1_w_in, ffn1_w_out, ffn2_w_in, ffn2_w_out, even_w_in, gla_w_gate2, gla_b_gate, gla_norm, ssd_conv_w, ssd_conv_b, ssd_dt_bias, ssd_A_log, ssd_D, ssd_norm, even_w_out, odd_w_in, gdn_conv_w, gdn_A_log, gdn_dt_bias, gdn_norm, odd_w_out, final_norm):
    bp, tp, d = x_prompt.shape
    bs, ts, _ = x_sample.shape
    depth = w_ada.shape[0]
    geom = (bp, tp, bs, ts)
    assert tp % ROW_TILE == 0 and bs * ts == ROW_TILE and ts == SUBLANES
    mp = bp * tp
    caches = (cache_dsa_w128, cache_dsa_w512, cache_dsa_w2048)

    x = jnp.concatenate([x_prompt.reshape(mp, d), x_sample.reshape(bs * ts, d)], axis=0)
    n_pad = (-(bs + bp)) % SUBLANES
    c_all = jnp.concatenate([c_sample, c_prompt, jnp.zeros((n_pad, d), F32)], axis=0)
    mod4 = ada_mod(c_all, w_ada, b_ada).reshape(depth, c_all.shape[0], N_MOD, d)
    expand = jnp.asarray((np.arange(LANES)[:, None] - SSD_DT_LANE0
                          == (np.arange(SSD_INNER)[None, :] // SSD_P)).astype(np.float32))

    new ={k: ([], []) for k in ("gla", "ssd", "ssd_conv", "gdn", "gdn_conv", "dsa0", "dsa1", "dsa2")}

    h = rows_norm(x, mod4, ln_ffn1[0], geom, mod_layer=0, shift_idx=0, scale_idx=1)
    for l in range(depth):
        act = matmul_swiglu(h, ffn1_w_in, l)
        x = matmul_residual(act, ffn1_w_out, l, x, mod4, geom, gate_layer=l, gate_idx=2, coef=0.5, name="ffn_out")
        h = rows_norm(x, mod4, ln_mix[l], geom, mod_layer=l, shift_idx=3, scale_idx=4)
        i = l // 2
        if l % 2 == 0:
            assert even_w_in.shape[-1] == EVEN_IN
            pma = matmul(h, even_w_in, layer=i, n_cols=EVA_W, tn=1024, name="even_in_a")
            pmb = matmul_shift(h, even_w_in, i, EVB_COL, EVB_W, name="even_in_b")
            p_glr = matmul_cols128(h, even_w_in, i, EV_GLR_COL, name="even_in_glr")
            p_dt = matmul_cols128(h, even_w_in, i, EV_DT_COL, name="even_in_dt")
            wg_pad = jnp.zeros((LANES, GLA_QK), F32).at[:GLA_RANK].set(gla_w_gate2[i])
            dtb = _pad_lanes(ssd_dt_bias[i], SSD_DT_LANE0)
            alog = _pad_lanes(ssd_A_log[i], SSD_DT_LANE0)
            d_exp = jnp.repeat(ssd_D[i], SSD_P).reshape(1, SSD_INNER)
            mix = []
            for grp, (row0, nb, t, s_gla, s_ssd, s_conv) in enumerate((
                    (0, bp, tp, None, None, None),
                    (mp, bs, ts, state_gla[i], state_ssd[i], state_ssd_conv[i]))):
                o_gla, gla_new = gla_mixer(pma, p_glr, wg_pad, gla_b_gate[i], gla_norm[i], s_gla,
                                           row0=row0, nb=nb, t=t)
                y_ssd, ssd_new, conv_new = ssd_mixer(pmb, p_dt, ssd_conv_w[i], ssd_conv_b[i].reshape(1, -1), dtb,
                                                     alog, d_exp, ssd_norm[i], expand, s_ssd, s_conv,
                                                     row0=row0, nb=nb, t=t)
                new["gla"][grp].append(gla_new)
                new["ssd"][grp].append(ssd_new)
                new["ssd_conv"][grp].append(conv_new)
                mix += [o_gla.reshape(nb * t, GLA_VW), y_ssd.reshape(nb * t, SSD_INNER)]
            x = matmul_pair_residual(*mix, even_w_out[i], x, mod4, geom, gate_layer=l, gate_idx=5, coef=1.0,
                                     name="even_out")
        else:
            assert odd_w_in.shape[-1] == ODD_IN
            pma = matmul(h, odd_w_in, layer=i, n_cols=ODA_W, tn=1024, name="odd_in_a")
            pmb = matmul_shift(h, odd_w_in, i, ODB_COL, ODB_W, name="odd_in_b")
            p_ba = matmul_cols128(h, odd_w_in, i, OD_BA_COL, name="odd_in_ba")
            rq, rk, qkv = rope_qk(pmb, geom)
            alog = _pad_lanes(gdn_A_log[i], GDN_V_HEADS)
            dtb = _pad_lanes(gdn_dt_bias[i], GDN_V_HEADS)
            mix = []
            for grp, (row0, nb, t, s_gdn, s_conv) in enumerate((
                    (0, bp, tp, None, None), (mp, bs, ts, state_gdn[i], state_gdn_conv[i]))):
                o_gdn, gdn_new, conv_new = gdn_mixer(pma, p_ba, gdn_conv_w[i], alog, dtb, gdn_norm[i],
                                                     s_gdn, s_conv, row0=row0, nb=nb, t=t)
                new["gdn"][grp].append(gdn_new)
                new["gdn_conv"][grp].append(conv_new)
                os_, ls_ = [], []
                for g, (window, dil) in enumerate(DSA_GROUPS):
                    vcol = ODB_DV + g * DSA_GW
                    if grp == 0:
                        o_g, l_g = dsa_prompt(qkv[g], g, geom)
                        keep = min(window, tp)
                        kg = jnp.stack([rk[(b + 1) * tp - keep:(b + 1) * tp, g * DSA_GW:(g + 1) * DSA_GW]
                                        for b in range(bp)])
                        vg = jnp.stack([pmb[(b + 1) * tp - keep:(b + 1) * tp, vcol:vcol + DSA_GW]
                                        for b in range(bp)])
                    else:
                        o_g, l_g = dsa_sample(rq, rk, pmb, caches[g][i], g, geom)
                        keep = ts
                        kg = rk[mp:, g * DSA_GW:(g + 1) * DSA_GW]
                        vg = pmb[mp:, vcol:vcol + DSA_GW]
                    rows = jnp.stack([kg.reshape(nb, keep, DSA_HEADS, DSA_HD),
                                      vg.reshape(nb, keep, DSA_HEADS, DSA_HD)], axis=2)
                    new["dsa%d" % g][grp].append(rows)
                    os_.append(o_g)
                    ls_.append(l_g)
                if grp == 0:
                    o_dsa = dsa_merge(os_, ls_, [dil for _, dil in DSA_GROUPS], mp // ROW_TILE)
                else:
                    o_dsa = dsa_merge(os_, ls_, [0] * len(DSA_GROUPS), 1)
                mix += [o_gdn.reshape(nb * t, GDN_VW), o_dsa]
            x = matmul_pair_residual(*mix, odd_w_out[i], x, mod4, geom, gate_layer=l, gate_idx=5, coef=1.0,
                                     name="odd_out")
        h = rows_norm(x, mod4, ln_ffn2[l], geom, mod_layer=l, shift_idx=6, scale_idx=7)
        act = matmul_swiglu(h, ffn2_w_in, l)
        x = matmul_residual(act, ffn2_w_out, l, x, mod4, geom, gate_layer=l, gate_idx=8, coef=0.5, name="ffn_out")
        if l + 1 < depth:
            h = rows_norm(x, mod4, ln_ffn1[l + 1], geom, mod_layer=l + 1, shift_idx=0, scale_idx=1)
        else:
            y = rows_norm(x, mod4, final_norm, geom, modulate=False)

    y_prompt = y[:mp].reshape(bp, tp, d)
    y_sample = y[mp:].reshape(bs, ts, d)
    outs = [y_prompt, y_sample]
    for name in ("gla", "ssd", "ssd_conv", "gdn", "gdn_conv", "dsa0", "dsa1", "dsa2"):
        outs.append(jnp.stack(new[name][0]))
        outs.append(jnp.stack(new[name][1]))
    return tuple(outs)
```

```python
import functools

import numpy as np
import jax
import jax.numpy as jnp
from jax import lax
from jax.experimental import pallas as pl
from jax.experimental.pallas import tpu as pltpu

F32 = jnp.float32
BF16 = jnp.bfloat16

D_MODEL = 2048
D_FF = 5632
N_MOD = 9
EPS = 1e-6
CONV_W = 4
ROPE_THETA = 10000.0
PAST_LEN = 8192

GLA_HEADS, GLA_DK, GLA_DV, GLA_RANK, GLA_TAU = 4, 128, 256, 16, 16.0
GLA_QK, GLA_VW = GLA_HEADS * GLA_DK, GLA_HEADS * GLA_DV
SSD_HEADS, SSD_P, SSD_STATE, SSD_GROUPS = 32, 64, 128, 4
SSD_HPG = SSD_HEADS // SSD_GROUPS
SSD_INNER = SSD_HEADS * SSD_P
SSD_BC = SSD_GROUPS * SSD_STATE
SSD_CONV_DIM = SSD_INNER + 2 * SSD_BC
GDN_QK_HEADS, GDN_V_HEADS, GDN_DK, GDN_DV = 8, 16, 128, 128
GDN_QK, GDN_VW = GDN_QK_HEADS * GDN_DK, GDN_V_HEADS * GDN_DV
GDN_CONV_DIM = 2 * GDN_QK + GDN_VW
GDN_REP = GDN_V_HEADS // GDN_QK_HEADS
DSA_GROUPS = ((128, 1), (512, 4), (2048, 16))
DSA_HEADS, DSA_HD, DSA_BLOCK = 8, 128, 128
DSA_GW = DSA_HEADS * DSA_HD
DSA_W = len(DSA_GROUPS) * DSA_GW

LANES = 128
SUBLANES = 8
MXU_DIM = 256
VMEM_LIMIT_BYTES = 56 * 1024 * 1024

ROW_TILE = 256
RESIDUAL_TILE = 512
CHUNK = 64
GDN_PACK = MXU_DIM // CHUNK
NEG_BIG = -1e30

EVA_Q, EVA_K, EVA_V, EVA_R = 0, GLA_QK, 2 * GLA_QK, 2 * GLA_QK + GLA_VW
EVA_W = 2 * GLA_QK + 2 * GLA_VW
EV_GLR_COL = EVA_W
EVB_COL = EVA_W + GLA_RANK
EVB_Z, EVB_X, EVB_B, EVB_C = 0, SSD_INNER, 2 * SSD_INNER, 2 * SSD_INNER + SSD_BC
EVB_W = SSD_INNER + SSD_CONV_DIM
EV_DT_COL = EVB_COL + EVB_W
EVEN_IN = EV_DT_COL + SSD_HEADS
SSD_DT_LANE0 = EV_DT_COL % LANES
ODA_Q, ODA_K, ODA_V, ODA_Z = 0, GDN_QK, 2 * GDN_QK, 2 * GDN_QK + GDN_VW
ODA_W = 2 * GDN_QK + 2 * GDN_VW
OD_BA_COL = ODA_W
ODB_COL = ODA_W + 2 * GDN_V_HEADS
ODB_DQ, ODB_DK, ODB_DV = 0, DSA_W, 2 * DSA_W
ODB_W = 3 * DSA_W
ODD_IN = ODB_COL + ODB_W


def _cparams(*sem):
    return pltpu.CompilerParams(dimension_semantics=sem, vmem_limit_bytes=VMEM_LIMIT_BYTES)


def _dot(a, b):
    return jnp.dot(a.astype(BF16), b.astype(BF16), preferred_element_type=F32)


def _dot_nt(a, b):
    return lax.dot_general(a.astype(BF16), b.astype(BF16), (((1,), (1,)), ((), ())), preferred_element_type=F32)


def _dot_tn(a, b):
    return lax.dot_general(a.astype(BF16), b.astype(BF16), (((0,), (0,)), ((), ())), preferred_element_type=F32)


def _split2(x):
    hi = x.astype(BF16)
    lo = (x - hi.astype(F32)).astype(BF16)
    return hi, lo


def _split3(x):
    hi = x.astype(BF16)
    r1 = x - hi.astype(F32)
    mid = r1.astype(BF16)
    lo = (r1 - mid.astype(F32)).astype(BF16)
    return hi, mid, lo


def _dot_exact_rhs(x, e):
    hi, lo = _split2(x)
    eb = e.astype(BF16)
    return (jnp.dot(hi, eb, preferred_element_type=F32) + jnp.dot(lo, eb, preferred_element_type=F32))


def _dot_exact_lhs(e, x):
    eb = e.astype(BF16)
    return sum(jnp.dot(eb, p, preferred_element_type=F32) for p in _split3(x))


def _dot_nt_exact_lhs(e, x):
    eb = e.astype(BF16)
    nt = (((1,), (1,)), ((), ()))
    return sum(lax.dot_general(eb, p, nt, preferred_element_type=F32) for p in _split3(x))


def _cumsum_rows(x):
    n = x.shape[0]
    r = lax.broadcasted_iota(jnp.int32, (n, n), 0)
    c = lax.broadcasted_iota(jnp.int32, (n, n), 1)
    return _dot_exact_lhs(jnp.where(r >= c, 1.0, 0.0), x)


def _rows_of(x, n_rows):
    r = lax.broadcasted_iota(jnp.int32, (n_rows, x.shape[1]), 0)
    c = lax.broadcasted_iota(jnp.int32, (n_rows, x.shape[1]), 1)
    return _dot_nt_exact_lhs(jnp.where(r == c, 1.0, 0.0), x)


def _pad_rows(x, n):
    if x.shape[0] == n:
        return x
    return jnp.concatenate([x, jnp.zeros((n - x.shape[0],) + x.shape[1:], x.dtype)], axis=0)


def _silu(x):
    return x * jax.nn.sigmoid(x)


def _tri_masks(n):
    r = lax.broadcasted_iota(jnp.int32, (n, n), 0)
    c = lax.broadcasted_iota(jnp.int32, (n, n), 1)
    return r >= c, r > c, r == c


def _ada_body(c_ref, w_ref, b_ref, o_ref):
    cond = _silu(c_ref[...])
    o_ref[...] = _dot(cond, w_ref[...]) + b_ref[...]


def ada_mod(c_all, w_ada, b_ada):
    depth, d, n = w_ada.shape
    rows = c_all.shape[0]
    tn = 1024
    return pl.pallas_call(
        _ada_body,
        grid=(depth, n // tn),
        in_specs=[pl.BlockSpec((rows, d), lambda l, j: (0, 0)),
                  pl.BlockSpec((None, d, tn), lambda l, j: (l, 0, j)),
                  pl.BlockSpec((None, 1, tn), lambda l, j: (l, 0, j))],
        out_specs=pl.BlockSpec((None, rows, tn), lambda l, j: (l, 0, j)),
        out_shape=jax.ShapeDtypeStruct((depth, rows, n), F32),
        compiler_params=_cparams("arbitrary", "arbitrary"),
        name="ada_mod",
    )(c_all, w_ada, b_ada.reshape(depth, 1, n))


def _mod_specs(layer, geom, tiles_axis, col_map, width, row_tile=ROW_TILE):
    bp, tp, bs, _ = geom
    tiles_per_seq = tp // row_tile

    def prompt_map(*idx):
        return (layer, bs + jnp.minimum(idx[tiles_axis] // tiles_per_seq, bp - 1), 0, col_map(*idx))

    def sample_map(*idx):
        return (layer, 0, 0, col_map(*idx))

    return [pl.BlockSpec((None, 1, N_MOD, width), prompt_map), pl.BlockSpec((None, bs, N_MOD, width), sample_map)]


def _norm_body(x_ref, mp_ref, ms_ref, gain_ref, o_ref, *, modulate, shift_idx, scale_idx, n_prompt_tiles, ts):
    i = pl.program_id(0)
    gain = gain_ref[...]

    def compute(shape3, mod):
        x = x_ref[...]
        if shape3 is not None:
            x = x.reshape(shape3)
        var = jnp.mean(x * x, axis=-1, keepdims=True)
        y = x * lax.rsqrt(var + EPS) * gain
        if modulate:
            y = y * (1.0 + mod(scale_idx)) + mod(shift_idx)
        o_ref[...] = y.reshape(o_ref.shape).astype(o_ref.dtype)

    @pl.when(i < n_prompt_tiles)
    def _():
        compute(None, lambda k: mp_ref[0, k:k + 1, :])

    @pl.when(i >= n_prompt_tiles)
    def _():
        rt, d = x_ref.shape
        compute((rt // ts, ts, d), lambda k: ms_ref[:, k:k + 1, :])


def rows_norm(x, mod4, gain, geom, *, mod_layer=0, shift_idx=0, scale_idx=0, modulate=True):
    bp, tp, bs, ts = geom
    mrows, d = x.shape
    row_spec = pl.BlockSpec((ROW_TILE, d), lambda i: (i, 0))
    body = functools.partial(_norm_body, modulate=modulate, shift_idx=shift_idx, scale_idx=scale_idx,
                             n_prompt_tiles=bp * tp // ROW_TILE, ts=ts)
    return pl.pallas_call(
        body,
        grid=(mrows // ROW_TILE,),
        in_specs=[row_spec] + _mod_specs(mod_layer, geom, 0, lambda i: 0, d)
        + [pl.BlockSpec((1, d), lambda i: (0, 0))],
        out_specs=row_spec,
        out_shape=jax.ShapeDtypeStruct((mrows, d), BF16 if modulate else F32),
        compiler_params=_cparams("arbitrary"),
        name="rows_norm",
    )(x, mod4, mod4, gain.reshape(1, d))


def _mm_body(a_ref, w_ref, o_ref, wb_ref):
    @pl.when(pl.program_id(1) == 0)
    def _():
        wb_ref[...] = w_ref[...].astype(BF16)

    o_ref[...] = jnp.dot(a_ref[...], wb_ref[...], preferred_element_type=F32).astype(o_ref.dtype)


def _mm_swiglu_body(a_ref, wa_ref, wg_ref, o_ref, wb_ref):
    tn = o_ref.shape[1]

    @pl.when(pl.program_id(1) == 0)
    def _():
        wb_ref[:, :tn] = wa_ref[...].astype(BF16)
        wb_ref[:, tn:] = wg_ref[...].astype(BF16)

    u = jnp.dot(a_ref[...], wb_ref[...], preferred_element_type=F32)
    o_ref[...] = (_silu(u[:, :tn]) * u[:, tn:]).astype(o_ref.dtype)


def _pick(n, cands):
    for c in cands:
        if n % c == 0:
            return c
    raise ValueError(f"no tile for {n} in {cands}")


def matmul(a, w, *, layer=None, n_cols=None, tm_cands=(1056, 768, 512, 256), tn=512, out_dtype=F32, name="matmul"):
    mrows, k = a.shape
    n = w.shape[-1] if n_cols is None else n_cols
    assert n % tn == 0
    tm = _pick(mrows, tm_cands)
    if w.ndim == 3:
        w_spec = pl.BlockSpec((None, k, tn), lambda j, i: (layer, 0, j))
    else:
        w_spec = pl.BlockSpec((k, tn), lambda j, i: (0, j))
    return pl.pallas_call(
        _mm_body,
        grid=(n // tn, mrows // tm),
        in_specs=[pl.BlockSpec((tm, k), lambda j, i: (i, 0)), w_spec],
        out_specs=pl.BlockSpec((tm, tn), lambda j, i: (i, j)),
        out_shape=jax.ShapeDtypeStruct((mrows, n), out_dtype),
        scratch_shapes=[pltpu.VMEM((k, tn), BF16)],
        compiler_params=_cparams("arbitrary", "arbitrary"),
        name=name,
    )(a, w)


def matmul_swiglu(a, w_in, layer, *, tn=512, tm_cands=(2112, 1056, 768, 512, 256)):
    mrows, k = a.shape
    f = w_in.shape[-1] // 2
    tm = _pick(mrows, tm_cands)
    nt = f // tn
    return pl.pallas_call(
        _mm_swiglu_body,
        grid=(nt, mrows // tm),
        in_specs=[pl.BlockSpec((tm, k), lambda j, i: (i, 0)),
                  pl.BlockSpec((None, k, tn), lambda j, i: (layer, 0, j)),
                  pl.BlockSpec((None, k, tn), lambda j, i: (layer, 0, j + nt))],
        out_specs=pl.BlockSpec((tm, tn), lambda j, i: (i, j)),
        out_shape=jax.ShapeDtypeStruct((mrows, f), BF16),
        scratch_shapes=[pltpu.VMEM((k, 2 * tn), BF16)],
        compiler_params=_cparams("arbitrary", "arbitrary"),
        name="ffn_in_swiglu",
    )(a, w_in, w_in)


def _mm_shift_body(a_ref, w_ref, wx_ref, o_ref, wb_ref, *, shift):
    @pl.when(pl.program_id(1) == 0)
    def _():
        tn = o_ref.shape[1]
        full = jnp.concatenate([w_ref[...], wx_ref[...]], axis=1)
        wb_ref[...] = full[:, shift:shift + tn].astype(BF16)

    o_ref[...] = jnp.dot(a_ref[...], wb_ref[...], preferred_element_type=F32)


def matmul_shift(a, w, layer, col0, n, *, tn=1024, tm_cands=(1056, 768, 512, 256), name="matmul_shift"):
    mrows, k = a.shape
    tm = _pick(mrows, tm_cands)
    shift = col0 % LANES
    base = col0 - shift
    assert base % tn == 0 and n % tn == 0 and shift > 0
    return pl.pallas_call(
        functools.partial(_mm_shift_body, shift=shift),
        grid=(n // tn, mrows // tm),
        in_specs=[pl.BlockSpec((tm, k), lambda j, i: (i, 0)),
                  pl.BlockSpec((None, k, tn), lambda j, i: (layer, 0, base // tn + j)),
                  pl.BlockSpec((None, k, LANES), lambda j, i: (layer, 0, (base + (j + 1) * tn) // LANES))],
        out_specs=pl.BlockSpec((tm, tn), lambda j, i: (i, j)),
        out_shape=jax.ShapeDtypeStruct((mrows, n), F32),
        scratch_shapes=[pltpu.VMEM((k, tn), BF16)],
        compiler_params=_cparams("arbitrary", "arbitrary"),
        name=name,
    )(a, w, w)


def _mm_cols_body(a_ref, w_ref, o_ref, wb_ref, *, n_valid):
    @pl.when(pl.program_id(0) == 0)
    def _():
        w = w_ref[...]
        if n_valid < LANES:
            lane = lax.broadcasted_iota(jnp.int32, w.shape, 1)
            w = jnp.where(lane < n_valid, w, 0.0)
        wb_ref[...] = w.astype(BF16)

    o_ref[...] = jnp.dot(a_ref[...], wb_ref[...], preferred_element_type=F32)


def matmul_cols128(a, w, layer, col0, *, tm_cands=(1056, 768, 512, 256), name="matmul_cols128"):
    mrows, k = a.shape
    tm = _pick(mrows, tm_cands)
    cb = col0 // LANES
    n_valid = min(LANES, w.shape[-1] - cb * LANES)
    return pl.pallas_call(
        functools.partial(_mm_cols_body, n_valid=n_valid),
        grid=(mrows // tm,),
        in_specs=[pl.BlockSpec((tm, k), lambda i: (i, 0)),
                  pl.BlockSpec((None, k, LANES), lambda i: (layer, 0, cb))],
        out_specs=pl.BlockSpec((tm, LANES), lambda i: (i, 0)),
        out_shape=jax.ShapeDtypeStruct((mrows, LANES), F32),
        scratch_shapes=[pltpu.VMEM((k, LANES), BF16)],
        compiler_params=_cparams("arbitrary"),
        name=name,
    )(a, w)


def _residual_tile(geom):
    return _pick(geom[1], (RESIDUAL_TILE, ROW_TILE))


def _gated_residual(x_ref, acc, gp_ref, gs_ref, o_ref, *, prompt, coef, gate_idx, ts):
    if prompt:
        o_ref[...] = x_ref[...] + coef * gp_ref[0, gate_idx:gate_idx + 1, :] * acc
    else:
        tn = acc.shape[1]
        g = gs_ref[:, gate_idx:gate_idx + 1, :]
        o = (x_ref[:ROW_TILE, :].reshape(ROW_TILE // ts, ts, tn)
             + coef * g * acc[:ROW_TILE].reshape(ROW_TILE // ts, ts, tn))
        o_ref[:ROW_TILE, :] = o.reshape(ROW_TILE, tn)


def _mm_pair_body(p1_ref, p2_ref, s1_ref, s2_ref, w_ref, x_ref, gp_ref, gs_ref, o_ref, wb_ref,
                  *, n_prompt_tiles, coef, gate_idx, ts):
    i = pl.program_id(1)
    k1 = p1_ref.shape[1]

    @pl.when(i == 0)
    def _():
        wb_ref[...] = w_ref[...].astype(BF16)

    def go(r1, r2, prompt):
        acc = (jnp.dot(r1[...], wb_ref[:k1, :], preferred_element_type=F32)
               + jnp.dot(r2[...], wb_ref[k1:, :], preferred_element_type=F32))
        _gated_residual(x_ref, acc, gp_ref, gs_ref, o_ref, prompt=prompt, coef=coef, gate_idx=gate_idx, ts=ts)

    @pl.when(i < n_prompt_tiles)
    def _():
        go(p1_ref, p2_ref, True)

    @pl.when(i >= n_prompt_tiles)
    def _():
        go(s1_ref, s2_ref, False)


def matmul_pair_residual(p1, p2, s1, s2, w, x, mod4, geom, *, gate_layer, gate_idx, coef, tn=1024,
                         name="matmul_pair"):
    mp, k1 = p1.shape
    k2 = p2.shape[1]
    n = w.shape[-1]
    tm = _residual_tile(geom)
    npt = mp // tm
    assert s1.shape[0] == ROW_TILE and x.shape == (mp + ROW_TILE, n)

    pmap = lambda j, i: (jnp.minimum(i, npt - 1), 0)
    smap = lambda j, i: (0, 0)
    tile = pl.BlockSpec((tm, tn), lambda j, i: (i, j))
    body = functools.partial(_mm_pair_body, n_prompt_tiles=npt, coef=coef, gate_idx=gate_idx, ts=geom[3])
    return pl.pallas_call(
        body,
        grid=(n // tn, npt + 1),
        in_specs=[pl.BlockSpec((tm, k1), pmap), pl.BlockSpec((tm, k2), pmap),
                  pl.BlockSpec((ROW_TILE, k1), smap), pl.BlockSpec((ROW_TILE, k2), smap),
                  pl.BlockSpec((k1 + k2, tn), lambda j, i: (0, j)), tile]
        + _mod_specs(gate_layer, geom, 1, lambda j, i: j, tn, tm),
        out_specs=tile,
        out_shape=jax.ShapeDtypeStruct((mp + ROW_TILE, n), F32),
        scratch_shapes=[pltpu.VMEM((k1 + k2, tn), BF16)],
        compiler_params=_cparams("arbitrary", "arbitrary"),
        name=name,
    )(p1, p2, s1, s2, w, x, mod4, mod4)


def _mm_res_body(a_ref, w_ref, x_ref, gp_ref, gs_ref, o_ref, wb_ref, *, n_prompt_tiles, coef, gate_idx, ts):
    i = pl.program_id(1)

    @pl.when(i == 0)
    def _():
        wb_ref[...] = w_ref[...].astype(BF16)

    acc = jnp.dot(a_ref[...], wb_ref[...], preferred_element_type=F32)

    @pl.when(i < n_prompt_tiles)
    def _():
        _gated_residual(x_ref, acc, gp_ref, gs_ref, o_ref, prompt=True, coef=coef, gate_idx=gate_idx, ts=ts)

    @pl.when(i >= n_prompt_tiles)
    def _():
        _gated_residual(x_ref, acc, gp_ref, gs_ref, o_ref, prompt=False, coef=coef, gate_idx=gate_idx, ts=ts)


def matmul_residual(a, w, layer, x, mod4, geom, *, gate_layer, gate_idx, coef, tn=512, name="matmul_residual"):
    bp, tp, _, ts = geom
    mrows, k = a.shape
    n = w.shape[-1]
    tm = _residual_tile(geom)
    npt = bp * tp // tm
    tile = pl.BlockSpec((tm, tn), lambda j, i: (i, j))
    body = functools.partial(_mm_res_body, n_prompt_tiles=npt, coef=coef, gate_idx=gate_idx, ts=ts)
    return pl.pallas_call(
        body,
        grid=(n // tn, npt + 1),
        in_specs=[pl.BlockSpec((tm, k), lambda j, i: (i, 0)),
                  pl.BlockSpec((None, k, tn), lambda j, i: (layer, 0, j)), tile]
        + _mod_specs(gate_layer, geom, 1, lambda j, i: j, tn, tm),
        out_specs=tile,
        out_shape=jax.ShapeDtypeStruct((mrows, n), F32),
        scratch_shapes=[pltpu.VMEM((k, tn), BF16)],
        compiler_params=_cparams("arbitrary", "arbitrary"),
        name=name,
    )(a, w, x, mod4, mod4)


def _conv_chunk(ext_ref, raw, w_ref, cin):
    ext_ref[SUBLANES:SUBLANES + CHUNK, :] = _pad_rows(raw, CHUNK)
    acc = None
    for j in range(CONV_W):
        off = SUBLANES - (CONV_W - 1) + j
        term = ext_ref[off:off + CHUNK, :] * w_ref[j:j + 1, :]
        acc = term if acc is None else acc + term
    return acc


def _conv_init(ext_ref, c0_ref, has_state):
    ext_ref[0:SUBLANES, :] = jnp.zeros((SUBLANES, ext_ref.shape[1]), F32)
    if has_state:
        ext_ref[SUBLANES - (CONV_W - 1):SUBLANES, :] = c0_ref[0]


def _conv_tail(ext_ref, cin):
    return ext_ref[cin + SUBLANES - (CONV_W - 1):cin + SUBLANES, :]


def _conv_advance(ext_ref):
    ext_ref[0:SUBLANES, :] = ext_ref[CHUNK:CHUNK + SUBLANES, :]


def _gla_body(q_ref, k_ref, v_ref, r_ref, g_ref, wg_ref, bg_ref, nrm_ref, s0_ref, o_ref, so_ref, st_ref,
              *, cin, has_state):
    c = pl.program_id(1)
    heads = range(GLA_HEADS)

    @pl.when(c == 0)
    def _():
        for h in heads:
            if has_state:
                st_ref[h] = s0_ref[0, h].T
            else:
                st_ref[h] = jnp.zeros(st_ref.shape[1:], F32)

    q = _pad_rows(q_ref[...], CHUNK) * (GLA_DK ** -0.5)
    k = _pad_rows(k_ref[...], CHUNK)
    v = _pad_rows(v_ref[...], CHUNK)
    glr = _pad_rows(g_ref[...], CHUNK)
    x = _dot(glr, wg_ref[...]) + bg_ref[...]
    logf = jax.nn.log_sigmoid(x) * (1.0 / GLA_TAU)
    if cin < CHUNK:
        rows = lax.broadcasted_iota(jnp.int32, logf.shape, 0)
        logf = jnp.where(rows < cin, logf, 0.0)
    b = _cumsum_rows(logf)
    bmid = b[CHUNK // 2 - 1:CHUNK // 2, :]
    blast = b[CHUNK - 1:CHUNK, :]
    incl, _, _ = _tri_masks(CHUNK)
    qe = q * jnp.exp(b - bmid)
    ke = k * jnp.exp(bmid - b)
    qb = q * jnp.exp(b)
    kw = k * jnp.exp(blast - b)
    sdec = jnp.exp(blast)

    def hk(a, h):
        return a[:, h * GLA_DK:(h + 1) * GLA_DK]

    def hv(a, h):
        return a[:, h * GLA_DV:(h + 1) * GLA_DV]

    sts = [st_ref[h] for h in heads]
    atts = [jnp.where(incl, _dot_nt(hk(qe, h), hk(ke, h)), 0.0) for h in heads]
    os_ = [_dot(atts[h], hv(v, h)) + _dot_nt(hk(qb, h), sts[h]) for h in heads]
    st_news = [sts[h] * hk(sdec, h) + _dot_tn(hv(v, h), hk(kw, h)) for h in heads]
    ys = []
    for h in heads:
        st_ref[h] = st_news[h]
        o = os_[h]
        var = jnp.mean(o * o, axis=-1, keepdims=True)
        ys.append(o * lax.rsqrt(var + EPS) * nrm_ref[...])
    y = jnp.concatenate(ys, axis=1)
    o_ref[0] = (y[:cin] * _silu(r_ref[...])).astype(o_ref.dtype)

    @pl.when(c == pl.num_programs(1) - 1)
    def _():
        for h in heads:
            so_ref[0, h] = st_news[h].T


def gla_mixer(pm, psm, wg_pad, b_gate, nrm, s0, *, row0, nb, t):
    cin = min(t, CHUNK)
    nc = t // cin
    rb0 = row0 // cin
    has_state = s0 is not None
    if s0 is None:
        s0 = jnp.zeros((1, GLA_HEADS, GLA_DK, GLA_DV), F32)
        s0_map = lambda b, c: (0, 0, 0, 0)
    else:
        s0_map = lambda b, c: (b, 0, 0, 0)

    def rmap(col0, width):
        return lambda b, c: (rb0 + b * nc + c, col0 // width)

    const = lambda b, c: (0, 0)
    body = functools.partial(_gla_body, cin=cin, has_state=has_state)
    return pl.pallas_call(
        body,
        grid=(nb, nc),
        in_specs=[pl.BlockSpec((cin, GLA_QK), rmap(EVA_Q, GLA_QK)),
                  pl.BlockSpec((cin, GLA_QK), rmap(EVA_K, GLA_QK)),
                  pl.BlockSpec((cin, GLA_VW), rmap(EVA_V, GLA_VW)),
                  pl.BlockSpec((cin, GLA_VW), rmap(EVA_R, GLA_VW)),
                  pl.BlockSpec((cin, LANES), lambda b, c: (rb0 + b * nc + c, 0)),
                  pl.BlockSpec((LANES, GLA_QK), const),
                  pl.BlockSpec((1, GLA_QK), const),
                  pl.BlockSpec((1, GLA_DV), const),
                  pl.BlockSpec((1, GLA_HEADS, GLA_DK, GLA_DV), s0_map)],
        out_specs=[pl.BlockSpec((1, cin, GLA_VW), lambda b, c: (b, c, 0)),
                   pl.BlockSpec((1, GLA_HEADS, GLA_DK, GLA_DV), lambda b, c: (b, 0, 0, 0))],
        out_shape=[jax.ShapeDtypeStruct((nb, t, GLA_VW), BF16),
                   jax.ShapeDtypeStruct((nb, GLA_HEADS, GLA_DK, GLA_DV), F32)],
        scratch_shapes=[pltpu.VMEM((GLA_HEADS, GLA_DV, GLA_DK), F32)],
        compiler_params=_cparams("arbitrary", "arbitrary"),
        name="gla_mixer",
    )(pm, pm, pm, pm, psm, wg_pad, b_gate.reshape(1, GLA_QK), nrm.reshape(1, GLA_DV), s0)


def _ssd_body(z_ref, x_ref, b_ref, c_ref, sm_ref, cwx_ref, cwb_ref, cwc_ref, cbx_ref, cbb_ref, cbc_ref,
              dtb_ref, alog_ref, dexp_ref, nrm_ref, e_ref, s0_ref, c0x_ref, c0b_ref, c0c_ref,
              y_ref, so_ref, cox_ref, cob_ref, coc_ref,
              st_ref, ex_ref, eb_ref, ec_ref, *, cin, has_state):
    c = pl.program_id(1)

    @pl.when(c == 0)
    def _():
        if has_state:
            st_ref[...] = s0_ref[0].T
        else:
            st_ref[...] = jnp.zeros(st_ref.shape, F32)
        _conv_init(ex_ref, c0x_ref, has_state)
        _conv_init(eb_ref, c0b_ref, has_state)
        _conv_init(ec_ref, c0c_ref, has_state)

    xs = _silu(_conv_chunk(ex_ref, x_ref[...], cwx_ref, cin) + cbx_ref[...])
    bm = _silu(_conv_chunk(eb_ref, b_ref[...], cwb_ref, cin) + cbb_ref[...])
    cm = _silu(_conv_chunk(ec_ref, c_ref[...], cwc_ref, cin) + cbc_ref[...])

    dt = jax.nn.softplus(_pad_rows(sm_ref[...], CHUNK) + dtb_ref[...])
    if cin < CHUNK:
        rows = lax.broadcasted_iota(jnp.int32, dt.shape, 0)
        dt = jnp.where(rows < cin, dt, 0.0)
    a = dt * (-jnp.exp(alog_ref[...]))
    cum = _cumsum_rows(a)
    cum_t = _rows_of(cum, SSD_DT_LANE0 + SSD_HEADS)
    last = cum[CHUNK - 1:CHUNK, :]
    stacked = jnp.concatenate([dt, jnp.exp(cum), jnp.exp(last - cum)], axis=0)
    ex = _dot_exact_rhs(stacked, e_ref[...])
    dt_e, ecum_e, w_e = ex[:CHUNK], ex[CHUNK:2 * CHUNK], ex[2 * CHUNK:]
    xdt = xs * dt_e
    xw = xdt * w_e
    incl, _, _ = _tri_masks(CHUNK)
    lane = lax.broadcasted_iota(jnp.int32, (CHUNK, LANES), 1)
    st = st_ref[...]
    gw = SSD_HPG * SSD_P
    groups = range(SSD_GROUPS)
    pairs = SSD_HPG // 2

    def grp(a, g, w):
        return a[:, g * w:(g + 1) * w]

    cbs = [_dot_nt(grp(cm, g, SSD_STATE), grp(bm, g, SSD_STATE)) for g in groups]
    inters = [_dot(grp(cm, g, SSD_STATE), grp(st, g, gw)) for g in groups]
    st_adds = [_dot_tn(grp(bm, g, SSD_STATE), grp(xw, g, gw)) for g in groups]
    att_pairs = []
    for g in groups:
        for pr in range(pairs):
            atts = []
            for h in (g * SSD_HPG + 2 * pr, g * SSD_HPG + 2 * pr + 1):
                ln = SSD_DT_LANE0 + h
                seg = cum[:, ln:ln + 1] - cum_t[ln:ln + 1, :]
                atts.append(cbs[g] * jnp.exp(jnp.where(incl, seg, NEG_BIG)))
            att_pairs.append(jnp.concatenate(atts, axis=0))
    yys = [_dot(att_pairs[p], xdt[:, p * LANES:(p + 1) * LANES]) for p in range(SSD_GROUPS * pairs)]
    ys = [jnp.where(lane < SSD_P, yy[:CHUNK], yy[CHUNK:]) for yy in yys]
    for g in groups:
        st_ref[:, g * gw:(g + 1) * gw] = grp(st, g, gw) * grp(ecum_e, g, gw)[CHUNK - 1:CHUNK] + st_adds[g]
    y = (jnp.concatenate(ys, axis=1) + jnp.concatenate(inters, axis=1) * ecum_e + xs * dexp_ref[...])
    y = y[:cin] * _silu(z_ref[...])
    var = jnp.mean(y * y, axis=-1, keepdims=True)
    y_ref[0] = (y * lax.rsqrt(var + EPS) * nrm_ref[...]).astype(y_ref.dtype)

    @pl.when(c == pl.num_programs(1) - 1)
    def _():
        so_ref[0] = st_ref[...].T
        cox_ref[0] = _conv_tail(ex_ref, cin)
        cob_ref[0] = _conv_tail(eb_ref, cin)
        coc_ref[0] = _conv_tail(ec_ref, cin)

    if cin == CHUNK:
        _conv_advance(ex_ref)
        _conv_advance(eb_ref)
        _conv_advance(ec_ref)


def ssd_mixer(pm, psm, conv_w, conv_b, dtb_pad, alog_pad, d_exp, nrm, expand, s0, c0, *, row0, nb, t):
    cin = min(t, CHUNK)
    nc = t // cin
    rb0 = row0 // cin
    has_state = s0 is not None
    hp = SSD_INNER
    if s0 is None:
        s0 = jnp.zeros((1, hp, SSD_STATE), F32)
        c0 = jnp.zeros((1, CONV_W - 1, SSD_CONV_DIM), F32)
        bmap = lambda b, c: 0
    else:
        s0 = s0.reshape(nb, hp, SSD_STATE)
        bmap = lambda b, c: b

    def rmap(col0, width):
        return lambda b, c: (rb0 + b * nc + c, col0 // width)

    const = lambda b, c: (0, 0)
    kconv = CONV_W - 1
    body = functools.partial(_ssd_body, cin=cin, has_state=has_state)
    outs = pl.pallas_call(
        body,
        grid=(nb, nc),
        in_specs=[pl.BlockSpec((cin, hp), rmap(EVB_Z, hp)),
                  pl.BlockSpec((cin, hp), rmap(EVB_X, hp)),
                  pl.BlockSpec((cin, SSD_BC), rmap(EVB_B, SSD_BC)),
                  pl.BlockSpec((cin, SSD_BC), rmap(EVB_C, SSD_BC)),
                  pl.BlockSpec((cin, LANES), lambda b, c: (rb0 + b * nc + c, 0)),
                  pl.BlockSpec((CONV_W, hp), lambda b, c: (0, 0)),
                  pl.BlockSpec((CONV_W, SSD_BC), lambda b, c: (0, hp // SSD_BC)),
                  pl.BlockSpec((CONV_W, SSD_BC), lambda b, c: (0, hp // SSD_BC + 1)),
                  pl.BlockSpec((1, hp), lambda b, c: (0, 0)),
                  pl.BlockSpec((1, SSD_BC), lambda b, c: (0, hp // SSD_BC)),
                  pl.BlockSpec((1, SSD_BC), lambda b, c: (0, hp // SSD_BC + 1)),
                  pl.BlockSpec((1, LANES), const),
                  pl.BlockSpec((1, LANES), const),
                  pl.BlockSpec((1, hp), const),
                  pl.BlockSpec((1, hp), const),
                  pl.BlockSpec((LANES, hp), const),
                  pl.BlockSpec((1, hp, SSD_STATE), lambda b, c: (bmap(b, c), 0, 0)),
                  pl.BlockSpec((1, kconv, hp), lambda b, c: (bmap(b, c), 0, 0)),
                  pl.BlockSpec((1, kconv, SSD_BC), lambda b, c: (bmap(b, c), 0, hp // SSD_BC)),
                  pl.BlockSpec((1, kconv, SSD_BC), lambda b, c: (bmap(b, c), 0, hp // SSD_BC + 1))],
        out_specs=[pl.BlockSpec((1, cin, hp), lambda b, c: (b, c, 0)),
                   pl.BlockSpec((1, hp, SSD_STATE), lambda b, c: (b, 0, 0)),
                   pl.BlockSpec((1, kconv, hp), lambda b, c: (b, 0, 0)),
                   pl.BlockSpec((1, kconv, SSD_BC), lambda b, c: (b, 0, 0)),
                   pl.BlockSpec((1, kconv, SSD_BC), lambda b, c: (b, 0, 0))],
        out_shape=[jax.ShapeDtypeStruct((nb, t, hp), BF16),
                   jax.ShapeDtypeStruct((nb, hp, SSD_STATE), F32),
                   jax.ShapeDtypeStruct((nb, kconv, hp), F32),
                   jax.ShapeDtypeStruct((nb, kconv, SSD_BC), F32),
                   jax.ShapeDtypeStruct((nb, kconv, SSD_BC), F32)],
        scratch_shapes=[pltpu.VMEM((SSD_STATE, hp), F32),
                        pltpu.VMEM((CHUNK + SUBLANES, hp), F32),
                        pltpu.VMEM((CHUNK + SUBLANES, SSD_BC), F32),
                        pltpu.VMEM((CHUNK + SUBLANES, SSD_BC), F32)],
        compiler_params=_cparams("arbitrary", "arbitrary"),
        name="ssd_mixer",
    )(pm, pm, pm, pm, psm, conv_w, conv_w, conv_w, conv_b, conv_b, conv_b, dtb_pad, alog_pad, d_exp,
      nrm.reshape(1, hp), expand, s0, c0, c0, c0)
    y, s_new, cx, cb_, cc = outs
    conv_s = jnp.concatenate([cx, cb_, cc], axis=-1)
    return y, s_new.reshape(nb, SSD_HEADS, SSD_P, SSD_STATE), conv_s


def _l2norm_heads(x, n_heads, width, scale):
    outs = []
    for e in range(n_heads):
        xe = x[:, e * width:(e + 1) * width]
        ss = jnp.sum(xe * xe, axis=-1, keepdims=True)
        outs.append(xe * (lax.rsqrt(ss + EPS) * scale))
    return outs


def _unit_lower_inverse(a_mats, eye, order):
    def mm(a, b):
        return jnp.dot(a, b, preferred_element_type=F32)

    ts = [eye - a for a in a_mats]
    pbs = [(-a).astype(BF16) for a in a_mats]
    for _ in range(max(order - 1, 1).bit_length() - 1):
        ps = [mm(pb, pb) for pb in pbs]
        pbs = [p.astype(BF16) for p in ps]
        ts = [t + mm(t.astype(BF16), pb) for t, pb in zip(ts, pbs)]
    outs = []
    splits = [(_split2(a), _split2(t)) for a, t in zip(a_mats, ts)]
    ats = [mm(ah, th) + mm(ah, tl) + mm(al, th) for (ah, al), (th, tl) in splits]
    for t, at, (_, (th, _)) in zip(ts, ats, splits):
        resid = eye - t - at
        outs.append(t + mm(th, resid.astype(BF16)))
    return outs


def _gdn_body(q_ref, k_ref, v_ref, z_ref, sm_ref, cwq_ref, cwk_ref, cwv_ref, alog_ref, dtb_ref, nrm_ref,
              s0_ref, c0q_ref, c0k_ref, c0v_ref,
              o_ref, so_ref, coq_ref, cok_ref, cov_ref,
              s_ref, eq_ref, ek_ref, ev_ref, *, cin, has_state):
    c = pl.program_id(1)
    n = GDN_PACK * CHUNK

    @pl.when(c == 0)
    def _():
        if has_state:
            s_ref[...] = s0_ref[0]
        else:
            s_ref[...] = jnp.zeros(s_ref.shape, F32)
        _conv_init(eq_ref, c0q_ref, has_state)
        _conv_init(ek_ref, c0k_ref, has_state)
        _conv_init(ev_ref, c0v_ref, has_state)

    qc = _silu(_conv_chunk(eq_ref, q_ref[...], cwq_ref, cin))
    kc = _silu(_conv_chunk(ek_ref, k_ref[...], cwk_ref, cin))
    vc = _silu(_conv_chunk(ev_ref, v_ref[...], cwv_ref, cin))
    qs = _l2norm_heads(qc, GDN_QK_HEADS, GDN_DK, GDN_DK ** -0.5)
    ks = _l2norm_heads(kc, GDN_QK_HEADS, GDN_DK, 1.0)

    sm = _pad_rows(sm_ref[...], CHUNK)
    beta = jax.nn.sigmoid(sm)
    gl = -jnp.exp(alog_ref[...]) * jax.nn.softplus(sm + dtb_ref[...])
    if cin < CHUNK:
        rows = lax.broadcasted_iota(jnp.int32, sm.shape, 0)
        beta = jnp.where(rows < cin, beta, 0.0)
        gl = jnp.where(rows < cin, gl, 0.0)
    cum = _cumsum_rows(gl)
    last_row = cum[CHUNK - 1:CHUNK, :]

    rr = lax.broadcasted_iota(jnp.int32, (n, n), 0)
    cc = lax.broadcasted_iota(jnp.int32, (n, n), 1)
    same = (rr // CHUNK) == (cc // CHUNK)
    incl = same & (rr >= cc)
    strict = same & (rr > cc)
    eye = jnp.where(rr == cc, 1.0, 0.0)
    head_of_row = lax.broadcasted_iota(jnp.int32, (n, LANES), 0) // CHUNK
    lane = lax.broadcasted_iota(jnp.int32, (n, LANES), 1)
    bd_mask = (lax.broadcasted_iota(jnp.int32, (n, GDN_PACK * GDN_DK), 0) // CHUNK
               == lax.broadcasted_iota(jnp.int32, (n, GDN_PACK * GDN_DK), 1) // GDN_DK)
    srow = lax.broadcasted_iota(jnp.int32, (GDN_PACK * GDN_DK, LANES), 0) // GDN_DK
    slane = lax.broadcasted_iota(jnp.int32, (GDN_PACK * GDN_DK, LANES), 1)

    def block_diag(x):
        return jnp.where(bd_mask, jnp.concatenate([x] * GDN_PACK, axis=1), 0.0)

    def stack(parts):
        return jnp.concatenate(parts, axis=0)

    cum_rep = stack([cum] * GDN_PACK)
    beta_rep = stack([beta] * GDN_PACK)
    last_rep = jnp.broadcast_to(last_row, (n, LANES))
    cum_t = _rows_of(cum, 2 * GDN_V_HEADS)
    packs = range(GDN_V_HEADS // GDN_PACK)

    pre = []
    for j in packs:
        h0 = j * GDN_PACK
        heads = range(h0, h0 + GDN_PACK)
        k_st = stack([ks[i // GDN_REP] for i in heads])
        q_st = stack([qs[i // GDN_REP] for i in heads])
        v_st = stack([vc[:, i * GDN_DV:(i + 1) * GDN_DV] for i in heads])
        g_sel = lane == GDN_V_HEADS + h0 + head_of_row
        col = jnp.sum(jnp.where(g_sel, cum_rep, 0.0), axis=1, keepdims=True)
        rowv = jnp.concatenate([cum_t[GDN_V_HEADS + i:GDN_V_HEADS + i + 1, :] for i in heads], axis=1)
        bcol = jnp.sum(jnp.where(lane == h0 + head_of_row, beta_rep, 0.0), axis=1, keepdims=True)
        lastc = jnp.sum(jnp.where(g_sel, last_rep, 0.0), axis=1, keepdims=True)
        dec = jnp.exp(jnp.where(incl, col - rowv, NEG_BIG))
        ecol = jnp.exp(col)
        s_st = s_ref[h0:h0 + GDN_PACK].reshape(GDN_PACK * GDN_DK, GDN_DV)
        pre.append(dict(k=k_st, q=q_st, v=v_st, col=col, bcol=bcol, lastc=lastc, dec=dec, ecol=ecol, s=s_st))
    kks = [_dot_nt(p["k"], p["k"]) for p in pre]
    qks = [_dot_nt(p["q"], p["k"]) for p in pre]
    a_mats = [jnp.where(strict, p["bcol"] * kk * p["dec"], 0.0) for p, kk in zip(pre, kks)]
    t_invs = _unit_lower_inverse(a_mats, eye, cin)
    rhss = [jnp.concatenate([p["v"] * p["bcol"], p["k"] * (p["bcol"] * p["ecol"])], axis=1) for p in pre]
    sols = [_dot(t, r) for t, r in zip(t_invs, rhss)]
    deltas = [sol[:, :GDN_DV] - _dot(block_diag(sol[:, GDN_DV:]), p["s"]) for sol, p in zip(sols, pre)]
    o_sts = [_dot(block_diag(p["q"] * p["ecol"]), p["s"]) + _dot(jnp.where(incl, qk * p["dec"], 0.0), dl)
             for p, qk, dl in zip(pre, qks, deltas)]
    outs = []
    for j, (p, dl, o_st) in enumerate(zip(pre, deltas, o_sts)):
        h0 = j * GDN_PACK
        kw_bd = block_diag(p["k"] * jnp.exp(p["lastc"] - p["col"]))
        s_dec = jnp.exp(jnp.sum(jnp.where(slane == GDN_V_HEADS + h0 + srow,
                                          jnp.broadcast_to(last_row, srow.shape), 0.0), axis=1, keepdims=True))
        s_new = p["s"] * s_dec + _dot_tn(kw_bd, dl)
        s_ref[h0:h0 + GDN_PACK] = s_new.reshape(GDN_PACK, GDN_DK, GDN_DV)
        var = jnp.mean(o_st * o_st, axis=-1, keepdims=True)
        y_st = o_st * lax.rsqrt(var + EPS) * nrm_ref[...]
        outs += [y_st[i * CHUNK:(i + 1) * CHUNK] for i in range(GDN_PACK)]
    o = jnp.concatenate(outs, axis=1)
    o_ref[0] = (o[:cin] * _silu(z_ref[...])).astype(o_ref.dtype)

    @pl.when(c == pl.num_programs(1) - 1)
    def _():
        so_ref[0] = s_ref[...]
        coq_ref[0] = _conv_tail(eq_ref, cin)
        cok_ref[0] = _conv_tail(ek_ref, cin)
        cov_ref[0] = _conv_tail(ev_ref, cin)

    if cin == CHUNK:
        _conv_advance(eq_ref)
        _conv_advance(ek_ref)
        _conv_advance(ev_ref)


def gdn_mixer(pm, psm, conv_w, alog_pad, dtb_pad, nrm, s0, c0, *, row0, nb, t):
    cin = min(t, CHUNK)
    nc = t // cin
    rb0 = row0 // cin
    has_state = s0 is not None
    kconv = CONV_W - 1
    if s0 is None:
        s0 = jnp.zeros((1, GDN_V_HEADS, GDN_DK, GDN_DV), F32)
        c0 = jnp.zeros((1, kconv, GDN_CONV_DIM), F32)
        bmap = lambda b: 0
    else:
        bmap = lambda b: b

    def rmap(col0, width):
        return lambda b, c: (rb0 + b * nc + c, col0 // width)

    const = lambda b, c: (0, 0)
    body = functools.partial(_gdn_body, cin=cin, has_state=has_state)
    outs = pl.pallas_call(
        body,
        grid=(nb, nc),
        in_specs=[pl.BlockSpec((cin, GDN_QK), rmap(ODA_Q, GDN_QK)),
                  pl.BlockSpec((cin, GDN_QK), rmap(ODA_K, GDN_QK)),
                  pl.BlockSpec((cin, GDN_VW), rmap(ODA_V, GDN_VW)),
                  pl.BlockSpec((cin, GDN_VW), rmap(ODA_Z, GDN_VW)),
                  pl.BlockSpec((cin, LANES), lambda b, c: (rb0 + b * nc + c, 0)),
                  pl.BlockSpec((CONV_W, GDN_QK), lambda b, c: (0, 0)),
                  pl.BlockSpec((CONV_W, GDN_QK), lambda b, c: (0, 1)),
                  pl.BlockSpec((CONV_W, GDN_VW), lambda b, c: (0, 1)),
                  pl.BlockSpec((1, LANES), const),
                  pl.BlockSpec((1, LANES), const),
                  pl.BlockSpec((1, GDN_DV), const),
                  pl.BlockSpec((1, GDN_V_HEADS, GDN_DK, GDN_DV), lambda b, c: (bmap(b), 0, 0, 0)),
                  pl.BlockSpec((1, kconv, GDN_QK), lambda b, c: (bmap(b), 0, 0)),
                  pl.BlockSpec((1, kconv, GDN_QK), lambda b, c: (bmap(b), 0, 1)),
                  pl.BlockSpec((1, kconv, GDN_VW), lambda b, c: (bmap(b), 0, 1))],
        out_specs=[pl.BlockSpec((1, cin, GDN_VW), lambda b, c: (b, c, 0)),
                   pl.BlockSpec((1, GDN_V_HEADS, GDN_DK, GDN_DV), lambda b, c: (b, 0, 0, 0)),
                   pl.BlockSpec((1, kconv, GDN_QK), lambda b, c: (b, 0, 0)),
                   pl.BlockSpec((1, kconv, GDN_QK), lambda b, c: (b, 0, 0)),
                   pl.BlockSpec((1, kconv, GDN_VW), lambda b, c: (b, 0, 0))],
        out_shape=[jax.ShapeDtypeStruct((nb, t, GDN_VW), BF16),
                   jax.ShapeDtypeStruct((nb, GDN_V_HEADS, GDN_DK, GDN_DV), F32),
                   jax.ShapeDtypeStruct((nb, kconv, GDN_QK), F32),
                   jax.ShapeDtypeStruct((nb, kconv, GDN_QK), F32),
                   jax.ShapeDtypeStruct((nb, kconv, GDN_VW), F32)],
        scratch_shapes=[pltpu.VMEM((GDN_V_HEADS, GDN_DK, GDN_DV), F32),
                        pltpu.VMEM((CHUNK + SUBLANES, GDN_QK), F32),
                        pltpu.VMEM((CHUNK + SUBLANES, GDN_QK), F32),
                        pltpu.VMEM((CHUNK + SUBLANES, GDN_VW), F32)],
        compiler_params=_cparams("arbitrary", "arbitrary"),
        name="gdn_mixer",
    )(pm, pm, pm, pm, psm, conv_w, conv_w, conv_w, alog_pad, dtb_pad, nrm.reshape(1, GDN_DV),
      s0, c0, c0, c0)
    o, s_new, cq, ck, cv = outs
    return o, s_new, jnp.concatenate([cq, ck, cv], axis=-1)


def _residue_perm(dil, inverse=False):
    n = ROW_TILE // dil
    i = lax.broadcasted_iota(jnp.int32, (ROW_TILE, ROW_TILE), 0)
    j = lax.broadcasted_iota(jnp.int32, (ROW_TILE, ROW_TILE), 1)
    if inverse:
        i, j = j, i
    return jnp.where(j == (i % n) * dil + i // n, 1.0, 0.0).astype(BF16)


def _rope_body(q_ref, k_ref, v_ref, inv_ref, qo_ref, ko_ref, *rest, n_prompt_tiles, tiles_per_seq, ts):
    i = pl.program_id(0)
    rt = q_ref.shape[0]
    r = lax.broadcasted_iota(jnp.int32, (rt, LANES), 0)
    pos_prompt = (i % tiles_per_seq) * rt + r
    pos_sample = PAST_LEN + r % ts
    pos = jnp.where(i < n_prompt_tiles, pos_prompt, pos_sample).astype(F32)
    ang = pos * inv_ref[...]
    cos = jnp.cos(ang)
    sin = jnp.sin(ang)
    lane = lax.broadcasted_iota(jnp.int32, (rt, LANES), 1)
    sin = jnp.where(lane < LANES // 2, -sin, sin)
    for src, dst in ((q_ref, qo_ref), (k_ref, ko_ref)):
        for h in range(DSA_W // DSA_HD):
            x = src[:, h * DSA_HD:(h + 1) * DSA_HD]
            dst[:, h * DSA_HD:(h + 1) * DSA_HD] = x * cos + pltpu.roll(x, DSA_HD // 2, 1) * sin
    for g, (_, dil) in enumerate(DSA_GROUPS):
        perm = None if dil == 1 else _residue_perm(dil)
        for src, dst in zip((qo_ref, ko_ref, v_ref), rest[3 * g:3 * g + 3]):
            xb = src[:, g * DSA_GW:(g + 1) * DSA_GW].astype(BF16)
            if perm is not None:
                xb = jnp.dot(perm, xb, preferred_element_type=F32).astype(BF16)
            dst[...] = xb.reshape(dst.shape)


def rope_qk(pm, geom):
    bp, tp, bs, ts = geom
    mrows = pm.shape[0]
    half = DSA_HD // 2
    inv = ROPE_THETA ** (-jnp.arange(half, dtype=F32) / half)
    inv2 = jnp.concatenate([inv, inv]).reshape(1, DSA_HD)
    body = functools.partial(_rope_body, n_prompt_tiles=bp * tp // ROW_TILE, tiles_per_seq=tp // ROW_TILE, ts=ts)
    spec_o = pl.BlockSpec((ROW_TILE, DSA_W), lambda i: (i, 0))
    out_specs = [spec_o, spec_o]
    out_shape = [jax.ShapeDtypeStruct((mrows, DSA_W), F32)] * 2
    for _, dil in DSA_GROUPS:
        n = ROW_TILE // dil
        out_specs += [pl.BlockSpec((dil, n, DSA_GW), lambda i: (0, i, 0))] * 3
        out_shape += [jax.ShapeDtypeStruct((dil, mrows // dil, DSA_GW), BF16)] * 3
    outs = pl.pallas_call(
        body,
        grid=(mrows // ROW_TILE,),
        in_specs=[pl.BlockSpec((ROW_TILE, DSA_W), lambda i: (i, ODB_DQ // DSA_W)),
                  pl.BlockSpec((ROW_TILE, DSA_W), lambda i: (i, ODB_DK // DSA_W)),
                  pl.BlockSpec((ROW_TILE, DSA_W), lambda i: (i, ODB_DV // DSA_W)),
                  pl.BlockSpec((1, DSA_HD), lambda i: (0, 0))],
        out_specs=out_specs,
        out_shape=out_shape,
        compiler_params=_cparams("arbitrary"),
        name="rope_qk",
    )(pm, pm, pm, inv2)
    return outs[0], outs[1], [tuple(outs[2 + 3 * g:5 + 3 * g]) for g in range(len(DSA_GROUPS))]


def _lse_lanes(lses):
    rows = lses[0].shape[0]
    lane = lax.broadcasted_iota(jnp.int32, (rows, LANES), 1)
    out = jnp.zeros((rows, LANES), F32)
    for h, v in enumerate(lses):
        out = jnp.where(lane == h, v, out)
    return out


def _dsa_prompt_body(q_ref, kp_ref, kc_ref, vp_ref, vc_ref, o_ref, l_ref, *, jmax):
    n = pl.program_id(2)
    blk = DSA_BLOCK
    i_idx = lax.broadcasted_iota(jnp.int32, (blk, 2 * blk), 0)
    m_idx = lax.broadcasted_iota(jnp.int32, (blk, 2 * blk), 1)
    dist = i_idx + blk - m_idx
    valid = (dist >= 0) & (dist <= jmax) & ((m_idx >= blk) | (n > 0))
    scale = DSA_HD ** -0.5
    lses = []
    for h in range(DSA_HEADS):
        sl = slice(h * DSA_HD, (h + 1) * DSA_HD)
        kk = jnp.concatenate([kp_ref[:, sl], kc_ref[:, sl]], axis=0)
        vv = jnp.concatenate([vp_ref[:, sl], vc_ref[:, sl]], axis=0)
        s = _dot_nt(q_ref[:, sl], kk) * scale
        s = jnp.where(valid, s, NEG_BIG)
        mx = jnp.max(s, axis=-1, keepdims=True)
        p = jnp.exp(s - mx)
        den = jnp.sum(p, axis=-1, keepdims=True)
        o_ref[:, sl] = (_dot(p, vv) / den).astype(o_ref.dtype)
        lses.append(mx + jnp.log(den))
    l_ref[...] = _lse_lanes(lses)


def dsa_prompt(qkv, g, geom):
    bp, tp, _, _ = geom
    window, dil = DSA_GROUPS[g]
    ls = tp // dil
    nblk = ls // DSA_BLOCK
    q, k, v = qkv
    mu = bp * ls

    cur = lambda b, r, n: (r, b * nblk + n, 0)
    prev = lambda b, r, n: (r, b * nblk + jnp.maximum(n - 1, 0), 0)
    blk = (None, DSA_BLOCK, DSA_GW)
    return pl.pallas_call(
        functools.partial(_dsa_prompt_body, jmax=window // dil),
        grid=(bp, dil, nblk),
        in_specs=[pl.BlockSpec(blk, cur), pl.BlockSpec(blk, prev), pl.BlockSpec(blk, cur),
                  pl.BlockSpec(blk, prev), pl.BlockSpec(blk, cur)],
        out_specs=[pl.BlockSpec(blk, cur), pl.BlockSpec((None, DSA_BLOCK, LANES), cur)],
        out_shape=[jax.ShapeDtypeStruct((dil, mu, DSA_GW), BF16),
                   jax.ShapeDtypeStruct((dil, mu, LANES), F32)],
        compiler_params=_cparams("arbitrary", "arbitrary", "arbitrary"),
        name=f"dsa_prompt_w{window}",
    )(q, k, k, v, v)


def _dsa_sample_body(q_ref, kn_ref, vn_ref, cache_ref, o_ref, l_ref, m_scr, d_scr, acc_scr, *, window, dil):
    j = pl.program_id(1)
    nj = pl.num_programs(1)
    ts = q_ref.shape[0]
    nh = DSA_HEADS
    scale = DSA_HD ** -0.5

    def heads_to_rows(ref):
        return jnp.concatenate([ref[:, h * DSA_HD:(h + 1) * DSA_HD] for h in range(nh)], axis=0)

    @pl.when(j == 0)
    def _():
        m_scr[...] = jnp.full(m_scr.shape, NEG_BIG, F32)
        d_scr[...] = jnp.zeros(d_scr.shape, F32)
        acc_scr[...] = jnp.zeros(acc_scr.shape, F32)

    def update(s, vals, ok):
        s = jnp.where(ok, s, NEG_BIG)
        m_old = m_scr[...]
        m_new = jnp.maximum(m_old, jnp.max(s, axis=-1, keepdims=True))
        alpha = jnp.exp(m_old - m_new)
        p = jnp.where(ok, jnp.exp(s - m_new[:, 0:1]), 0.0)
        d_scr[...] = alpha * d_scr[...] + jnp.sum(p, axis=-1, keepdims=True)
        acc_scr[...] = alpha * acc_scr[...] + _dot(p, vals)
        m_scr[...] = m_new

    q_rows = heads_to_rows(q_ref) * scale
    ub, n_res = cache_ref.shape[1], cache_ref.shape[2] // (2 * nh)
    lb = ub * n_res
    x4 = cache_ref[0].reshape(lb, 2, nh, DSA_HD)
    k_all = x4[:, 0].reshape(lb * nh, DSA_HD)
    v_all = x4[:, 1].reshape(lb * nh, DSA_HD)
    r = lax.broadcasted_iota(jnp.int32, (nh * ts, lb * nh), 0)
    c = lax.broadcasted_iota(jnp.int32, (nh * ts, lb * nh), 1)
    pos = c // nh
    dist = window + r % ts - ((j * ub + pos // n_res) * dil + pos % n_res)
    ok = (r // ts == c % nh) & ((dist & (dil - 1)) == 0) & (dist <= window)
    update(_dot_nt(q_rows, k_all), v_all, ok)

    @pl.when(j == nj - 1)
    def _():
        r2 = lax.broadcasted_iota(jnp.int32, (nh * ts, nh * ts), 0)
        c2 = lax.broadcasted_iota(jnp.int32, (nh * ts, nh * ts), 1)
        d2 = r2 % ts - c2 % ts
        ok2 = (r2 // ts == c2 // ts) & (d2 >= 0) & ((d2 & (dil - 1)) == 0)
        update(_dot_nt(q_rows, heads_to_rows(kn_ref)), heads_to_rows(vn_ref), ok2)
        o_rows = acc_scr[...] / d_scr[...]
        lse = m_scr[...] + jnp.log(d_scr[...])
        for h in range(nh):
            o_ref[:, h * DSA_HD:(h + 1) * DSA_HD] = o_rows[h * ts:(h + 1) * ts]
        l_ref[...] = _lse_lanes([lse[h * ts:(h + 1) * ts, 0:1] for h in range(nh)])


def dsa_sample(rq, rk, pm, cache, g, geom):
    bp, tp, bs, ts = geom
    window, dil = DSA_GROUPS[g]
    lcache = cache.shape[1]
    assert lcache == window and window % dil == 0, "cache must hold exactly the window"
    n_res = min(dil, ts)
    strides = lcache // dil
    ub = min(strides, 512 // n_res)
    rb0 = bp * tp // ts
    kv_rows = 2 * DSA_HEADS
    cache4 = cache.reshape(bs, strides, dil * kv_rows, DSA_HD)
    blk = (ts, DSA_GW)
    body = functools.partial(_dsa_sample_body, window=window, dil=dil)
    return pl.pallas_call(
        body,
        grid=(bs, strides // ub),
        in_specs=[pl.BlockSpec(blk, lambda b, j: (rb0 + b, g)),
                  pl.BlockSpec(blk, lambda b, j: (rb0 + b, g)),
                  pl.BlockSpec(blk, lambda b, j: (rb0 + b, ODB_DV // DSA_GW + g)),
                  pl.BlockSpec((1, ub, n_res * kv_rows, DSA_HD), lambda b, j: (b, j, 0, 0))],
        out_specs=[pl.BlockSpec(blk, lambda b, j: (b, 0)), pl.BlockSpec((ts, LANES), lambda b, j: (b, 0))],
        out_shape=[jax.ShapeDtypeStruct((bs * ts, DSA_GW), F32),
                   jax.ShapeDtypeStruct((bs * ts, LANES), F32)],
        scratch_shapes=[pltpu.VMEM((DSA_HEADS * ts, DSA_HD), F32)] * 3,
        compiler_params=_cparams("arbitrary", "arbitrary"),
        name=f"dsa_sample_w{window}",
    )(rq, rk, pm, cache4)


def _dsa_merge_body(o0, o1, o2, l0, l1, l2, e_ref, out_ref, *, dils):
    os_, ls_ = [], []
    for o_ref, l_ref, dil in zip((o0, o1, o2), (l0, l1, l2), dils):
        o = o_ref[...].reshape(ROW_TILE, DSA_GW)
        l = l_ref[...].reshape(ROW_TILE, LANES)
        if dil > 1:
            pinv = _residue_perm(dil, inverse=True)
            o = jnp.dot(pinv, o.astype(BF16), preferred_element_type=F32)
            l = _dot_exact_lhs(pinv, l)
        os_.append(o.astype(F32))
        ls_.append(l)
    mx = jnp.maximum(jnp.maximum(ls_[0], ls_[1]), ls_[2])
    ws = [jnp.exp(l - mx) for l in ls_]
    tot = ws[0] + ws[1] + ws[2]
    acc = None
    for w, o in zip(ws, os_):
        term = _dot_exact_rhs(w / tot, e_ref[...]) * o
        acc = term if acc is None else acc + term
    out_ref[...] = acc.astype(out_ref.dtype)


def dsa_merge(os_, ls_, dils, n_tiles):
    in_specs = []
    for width in (DSA_GW, LANES):
        for dil in dils:
            if dil:
                in_specs.append(pl.BlockSpec((dil, ROW_TILE // dil, width), lambda i: (0, i, 0)))
            else:
                in_specs.append(pl.BlockSpec((ROW_TILE, width), lambda i: (i, 0)))
    in_specs.append(pl.BlockSpec((LANES, DSA_GW), lambda i: (0, 0)))
    expand = jnp.asarray((np.arange(LANES)[:, None] == (np.arange(DSA_GW)[None, :] // DSA_HD)).astype(np.float32))
    return pl.pallas_call(
        functools.partial(_dsa_merge_body, dils=tuple(max(d, 1) for d in dils)),
        grid=(n_tiles,),
        in_specs=in_specs,
        out_specs=pl.BlockSpec((ROW_TILE, DSA_GW), lambda i: (i, 0)),
        out_shape=jax.ShapeDtypeStruct((n_tiles * ROW_TILE, DSA_GW), BF16),
        compiler_params=_cparams("arbitrary"),
        name="dsa_merge",
    )(*os_, *ls_, expand)


def _pad_lanes(vec, offset=0):
    return jnp.zeros((1, LANES), F32).at[0, offset:offset + vec.shape[0]].set(vec)


def kernel(x_prompt, x_sample, c_prompt, c_sample, state_gla, state_ssd, state_ssd_conv, state_gdn, state_gdn_conv, cache_dsa_w128, cache_dsa_w512, cache_dsa_w2048, ln_ffn1, ln_mix, ln_ffn2, w_ada, b_ada, ffn1_w_in, ffn1_w_out, ffn2_w_in, ffn2_w_out, even_w_in, gla_w_gate2, gla_b_gate, gla_norm, ssd_conv_w, ssd_conv_b, ssd_dt_bias, ssd_A_log, ssd_D, ssd_norm, even_w_out, odd_w_in, gdn_conv_w, gdn_A_log, gdn_dt_bias, gdn_norm, odd_w_out, final_norm):
    bp, tp, d = x_prompt.shape
    bs, ts, _ = x_sample.shape
    depth = w_ada.shape[0]
    geom = (bp, tp, bs, ts)
    assert tp % ROW_TILE == 0 and bs * ts == ROW_TILE and ts == SUBLANES
    mp = bp * tp
    caches = (cache_dsa_w128, cache_dsa_w512, cache_dsa_w2048)

    x = jnp.concatenate([x_prompt.reshape(mp, d), x_sample.reshape(bs * ts, d)], axis=0)
    n_pad = (-(bs + bp)) % SUBLANES
    c_all = jnp.concatenate([c_sample, c_prompt, jnp.zeros((n_pad, d), F32)], axis=0)
    mod4 = ada_mod(c_all, w_ada, b_ada).reshape(depth, c_all.shape[0], N_MOD, d)
    expand = jnp.asarray((np.arange(LANES)[:, None] - SSD_DT_LANE0
                          == (np.arange(SSD_INNER)[None, :] // SSD_P)).astype(np.float32))

    new ={k: ([], []) for k in ("gla", "ssd", "ssd_conv", "gdn", "gdn_conv", "dsa0", "dsa1", "dsa2")}

    h = rows_norm(x, mod4, ln_ffn1[0], geom, mod_layer=0, shift_idx=0, scale_idx=1)
    for l in range(depth):
        act = matmul_swiglu(h, ffn1_w_in, l)
        x = matmul_residual(act, ffn1_w_out, l, x, mod4, geom, gate_layer=l, gate_idx=2, coef=0.5, name="ffn_out")
        h = rows_norm(x, mod4, ln_mix[l], geom, mod_layer=l, shift_idx=3, scale_idx=4)
        i = l // 2
        if l % 2 == 0:
            assert even_w_in.shape[-1] == EVEN_IN
            pma = matmul(h, even_w_in, layer=i, n_cols=EVA_W, tn=1024, name="even_in_a")
            pmb = matmul_shift(h, even_w_in, i, EVB_COL, EVB_W, name="even_in_b")
            p_glr = matmul_cols128(h, even_w_in, i, EV_GLR_COL, name="even_in_glr")
            p_dt = matmul_cols128(h, even_w_in, i, EV_DT_COL, name="even_in_dt")
            wg_pad = jnp.zeros((LANES, GLA_QK), F32).at[:GLA_RANK].set(gla_w_gate2[i])
            dtb = _pad_lanes(ssd_dt_bias[i], SSD_DT_LANE0)
            alog = _pad_lanes(ssd_A_log[i], SSD_DT_LANE0)
            d_exp = jnp.repeat(ssd_D[i], SSD_P).reshape(1, SSD_INNER)
            mix = []
            for grp, (row0, nb, t, s_gla, s_ssd, s_conv) in enumerate((
                    (0, bp, tp, None, None, None),
                    (mp, bs, ts, state_gla[i], state_ssd[i], state_ssd_conv[i]))):
                o_gla, gla_new = gla_mixer(pma, p_glr, wg_pad, gla_b_gate[i], gla_norm[i], s_gla,
                                           row0=row0, nb=nb, t=t)
                y_ssd, ssd_new, conv_new = ssd_mixer(pmb, p_dt, ssd_conv_w[i], ssd_conv_b[i].reshape(1, -1), dtb,
                                                     alog, d_exp, ssd_norm[i], expand, s_ssd, s_conv,
                                                     row0=row0, nb=nb, t=t)
                new["gla"][grp].append(gla_new)
                new["ssd"][grp].append(ssd_new)
                new["ssd_conv"][grp].append(conv_new)
                mix += [o_gla.reshape(nb * t, GLA_VW), y_ssd.reshape(nb * t, SSD_INNER)]
            x = matmul_pair_residual(*mix, even_w_out[i], x, mod4, geom, gate_layer=l, gate_idx=5, coef=1.0,
                                     name="even_out")
        else:
            assert odd_w_in.shape[-1] == ODD_IN
            pma = matmul(h, odd_w_in, layer=i, n_cols=ODA_W, tn=1024, name="odd_in_a")
            pmb = matmul_shift(h, odd_w_in, i, ODB_COL, ODB_W, name="odd_in_b")
            p_ba = matmul_cols128(h, odd_w_in, i, OD_BA_COL, name="odd_in_ba")
            rq, rk, qkv = rope_qk(pmb, geom)
            alog = _pad_lanes(gdn_A_log[i], GDN_V_HEADS)
            dtb = _pad_lanes(gdn_dt_bias[i], GDN_V_HEADS)
            mix = []
            for grp, (row0, nb, t, s_gdn, s_conv) in enumerate((
                    (0, bp, tp, None, None), (mp, bs, ts, state_gdn[i], state_gdn_conv[i]))):
                o_gdn, gdn_new, conv_new = gdn_mixer(pma, p_ba, gdn_conv_w[i], alog, dtb, gdn_norm[i],
                                                     s_gdn, s_conv, row0=row0, nb=nb, t=t)
                new["gdn"][grp].append(gdn_new)
                new["gdn_conv"][grp].append(conv_new)
                os_, ls_ = [], []
                for g, (window, dil) in enumerate(DSA_GROUPS):
                    vcol = ODB_DV + g * DSA_GW
                    if grp == 0:
                        o_g, l_g = dsa_prompt(qkv[g], g, geom)
                        keep = min(window, tp)
                        kg = jnp.stack([rk[(b + 1) * tp - keep:(b + 1) * tp, g * DSA_GW:(g + 1) * DSA_GW]
                                        for b in range(bp)])
                        vg = jnp.stack([pmb[(b + 1) * tp - keep:(b + 1) * tp, vcol:vcol + DSA_GW]
                                        for b in range(bp)])
                    else:
                        o_g, l_g = dsa_sample(rq, rk, pmb, caches[g][i], g, geom)
                        keep = ts
                        kg = rk[mp:, g * DSA_GW:(g + 1) * DSA_GW]
                        vg = pmb[mp:, vcol:vcol + DSA_GW]
                    rows = jnp.stack([kg.reshape(nb, keep, DSA_HEADS, DSA_HD),
                                      vg.reshape(nb, keep, DSA_HEADS, DSA_HD)], axis=2)
                    new["dsa%d" % g][grp].append(rows)
                    os_.append(o_g)
                    ls_.append(l_g)
                if grp == 0:
                    o_dsa = dsa_merge(os_, ls_, [dil for _, dil in DSA_GROUPS], mp // ROW_TILE)
                else:
                    o_dsa = dsa_merge(os_, ls_, [0] * len(DSA_GROUPS), 1)
                mix += [o_gdn.reshape(nb * t, GDN_VW), o_dsa]
            x = matmul_pair_residual(*mix, odd_w_out[i], x, mod4, geom, gate_layer=l, gate_idx=5, coef=1.0,
                                     name="odd_out")
        h = rows_norm(x, mod4, ln_ffn2[l], geom, mod_layer=l, shift_idx=6, scale_idx=7)
        act = matmul_swiglu(h, ffn2_w_in, l)
        x = matmul_residual(act, ffn2_w_out, l, x, mod4, geom, gate_layer=l, gate_idx=8, coef=0.5, name="ffn_out")
        if l + 1 < depth:
            h = rows_norm(x, mod4, ln_ffn1[l + 1], geom, mod_layer=l + 1, shift_idx=0, scale_idx=1)
        else:
            y = rows_norm(x, mod4, final_norm, geom, modulate=False)

    y_prompt = y[:mp].reshape(bp, tp, d)
    y_sample = y[mp:].reshape(bs, ts, d)
    outs = [y_prompt, y_sample]
    for name in ("gla", "ssd", "ssd_conv", "gdn", "gdn_conv", "dsa0", "dsa1", "dsa2"):
        outs.append(jnp.stack(new[name][0]))
        outs.append(jnp.stack(new[name][1]))
    return tuple(outs)
```

```python
import functools

import numpy as np
import jax
import jax.numpy as jnp
from jax import lax
from jax.experimental import pallas as pl
from jax.experimental.pallas import tpu as pltpu

F32 = jnp.float32
BF16 = jnp.bfloat16

D_MODEL = 2048
D_FF = 5632
N_MOD = 9
EPS = 1e-6
CONV_W = 4
ROPE_THETA = 10000.0
PAST_LEN = 8192

GLA_HEADS, GLA_DK, GLA_DV, GLA_RANK, GLA_TAU = 4, 128, 256, 16, 16.0
GLA_QK, GLA_VW = GLA_HEADS * GLA_DK, GLA_HEADS * GLA_DV
SSD_HEADS, SSD_P, SSD_STATE, SSD_GROUPS = 32, 64, 128, 4
SSD_HPG = SSD_HEADS // SSD_GROUPS
SSD_INNER = SSD_HEADS * SSD_P
SSD_BC = SSD_GROUPS * SSD_STATE
SSD_CONV_DIM = SSD_INNER + 2 * SSD_BC
GDN_QK_HEADS, GDN_V_HEADS, GDN_DK, GDN_DV = 8, 16, 128, 128
GDN_QK, GDN_VW = GDN_QK_HEADS * GDN_DK, GDN_V_HEADS * GDN_DV
GDN_CONV_DIM = 2 * GDN_QK + GDN_VW
GDN_REP = GDN_V_HEADS // GDN_QK_HEADS
DSA_GROUPS = ((128, 1), (512, 4), (2048, 16))
DSA_HEADS, DSA_HD, DSA_BLOCK = 8, 128, 128
DSA_GW = DSA_HEADS * DSA_HD
DSA_W = len(DSA_GROUPS) * DSA_GW

LANES = 128
SUBLANES = 8
MXU_DIM = 256
VMEM_LIMIT_BYTES = 56 * 1024 * 1024

ROW_TILE = 256
RESIDUAL_TILE = 512
CHUNK = 64
GDN_PACK = MXU_DIM // CHUNK
NEG_BIG = -1e30

EVA_Q, EVA_K, EVA_V, EVA_R = 0, GLA_QK, 2 * GLA_QK, 2 * GLA_QK + GLA_VW
EVA_W = 2 * GLA_QK + 2 * GLA_VW
EV_GLR_COL = EVA_W
EVB_COL = EVA_W + GLA_RANK
EVB_Z, EVB_X, EVB_B, EVB_C = 0, SSD_INNER, 2 * SSD_INNER, 2 * SSD_INNER + SSD_BC
EVB_W = SSD_INNER + SSD_CONV_DIM
EV_DT_COL = EVB_COL + EVB_W
EVEN_IN = EV_DT_COL + SSD_HEADS
SSD_DT_LANE0 = EV_DT_COL % LANES
ODA_Q, ODA_K, ODA_V, ODA_Z = 0, GDN_QK, 2 * GDN_QK, 2 * GDN_QK + GDN_VW
ODA_W = 2 * GDN_QK + 2 * GDN_VW
OD_BA_COL = ODA_W
ODB_COL = ODA_W + 2 * GDN_V_HEADS
ODB_DQ, ODB_DK, ODB_DV = 0, DSA_W, 2 * DSA_W
ODB_W = 3 * DSA_W
ODD_IN = ODB_COL + ODB_W


def _cparams(*sem):
    return pltpu.CompilerParams(dimension_semantics=sem, vmem_limit_bytes=VMEM_LIMIT_BYTES)


def _dot(a, b):
    return jnp.dot(a.astype(BF16), b.astype(BF16), preferred_element_type=F32)


def _dot_nt(a, b):
    return lax.dot_general(a.astype(BF16), b.astype(BF16), (((1,), (1,)), ((), ())), preferred_element_type=F32)


def _dot_tn(a, b):
    return lax.dot_general(a.astype(BF16), b.astype(BF16), (((0,), (0,)), ((), ())), preferred_element_type=F32)


def _split2(x):
    hi = x.astype(BF16)
    lo = (x - hi.astype(F32)).astype(BF16)
    return hi, lo


def _split3(x):
    hi = x.astype(BF16)
    r1 = x - hi.astype(F32)
    mid = r1.astype(BF16)
    lo = (r1 - mid.astype(F32)).astype(BF16)
    return hi, mid, lo


def _dot_exact_rhs(x, e):
    hi, lo = _split2(x)
    eb = e.astype(BF16)
    return (jnp.dot(hi, eb, preferred_element_type=F32) + jnp.dot(lo, eb, preferred_element_type=F32))


def _dot_exact_lhs(e, x):
    eb = e.astype(BF16)
    return sum(jnp.dot(eb, p, preferred_element_type=F32) for p in _split3(x))


def _dot_nt_exact_lhs(e, x):
    eb = e.astype(BF16)
    nt = (((1,), (1,)), ((), ()))
    return sum(lax.dot_general(eb, p, nt, preferred_element_type=F32) for p in _split3(x))


def _cumsum_rows(x):
    n = x.shape[0]
    r = lax.broadcasted_iota(jnp.int32, (n, n), 0)
    c = lax.broadcasted_iota(jnp.int32, (n, n), 1)
    return _dot_exact_lhs(jnp.where(r >= c, 1.0, 0.0), x)


def _rows_of(x, n_rows):
    r = lax.broadcasted_iota(jnp.int32, (n_rows, x.shape[1]), 0)
    c = lax.broadcasted_iota(jnp.int32, (n_rows, x.shape[1]), 1)
    return _dot_nt_exact_lhs(jnp.where(r == c, 1.0, 0.0), x)


def _pad_rows(x, n):
    if x.shape[0] == n:
        return x
    return jnp.concatenate([x, jnp.zeros((n - x.shape[0],) + x.shape[1:], x.dtype)], axis=0)


def _silu(x):
    return x * jax.nn.sigmoid(x)


def _tri_masks(n):
    r = lax.broadcasted_iota(jnp.int32, (n, n), 0)
    c = lax.broadcasted_iota(jnp.int32, (n, n), 1)
    return r >= c, r > c, r == c


def _ada_body(c_ref, w_ref, b_ref, o_ref):
    cond = _silu(c_ref[...])
    o_ref[...] = _dot(cond, w_ref[...]) + b_ref[...]


def ada_mod(c_all, w_ada, b_ada):
    depth, d, n = w_ada.shape
    rows = c_all.shape[0]
    tn = 1024
    return pl.pallas_call(
        _ada_body,
        grid=(depth, n // tn),
        in_specs=[pl.BlockSpec((rows, d), lambda l, j: (0, 0)),
                  pl.BlockSpec((None, d, tn), lambda l, j: (l, 0, j)),
                  pl.BlockSpec((None, 1, tn), lambda l, j: (l, 0, j))],
        out_specs=pl.BlockSpec((None, rows, tn), lambda l, j: (l, 0, j)),
        out_shape=jax.ShapeDtypeStruct((depth, rows, n), F32),
        compiler_params=_cparams("arbitrary", "arbitrary"),
        name="ada_mod",
    )(c_all, w_ada, b_ada.reshape(depth, 1, n))


def _mod_specs(layer, geom, tiles_axis, col_map, width, row_tile=ROW_TILE):
    bp, tp, bs, _ = geom
    tiles_per_seq = tp // row_tile

    def prompt_map(*idx):
        return (layer, bs + jnp.minimum(idx[tiles_axis] // tiles_per_seq, bp - 1), 0, col_map(*idx))

    def sample_map(*idx):
        return (layer, 0, 0, col_map(*idx))

    return [pl.BlockSpec((None, 1, N_MOD, width), prompt_map), pl.BlockSpec((None, bs, N_MOD, width), sample_map)]


def _norm_body(x_ref, mp_ref, ms_ref, gain_ref, o_ref, *, modulate, shift_idx, scale_idx, n_prompt_tiles, ts):
    i = pl.program_id(0)
    gain = gain_ref[...]

    def compute(x, mod):
        var = jnp.mean(x * x, axis=-1, keepdims=True)
        y = x * lax.rsqrt(var + EPS) * gain
        if modulate:
            y = y * (1.0 + mod(scale_idx)) + mod(shift_idx)
        return y

    @pl.when(i < n_prompt_tiles)
    def _():
        o_ref[...] = compute(x_ref[...], lambda k: mp_ref[0, k:k + 1, :]).astype(o_ref.dtype)

    @pl.when(i >= n_prompt_tiles)
    def _():
        d = x_ref.shape[1]
        y = compute(x_ref[:ROW_TILE, :].reshape(ROW_TILE // ts, ts, d), lambda k: ms_ref[:, k:k + 1, :])
        o_ref[:ROW_TILE, :] = y.reshape(ROW_TILE, d).astype(o_ref.dtype)


def rows_norm(x, mod4, gain, geom, *, mod_layer=0, shift_idx=0, scale_idx=0, modulate=True):
    bp, tp, bs, ts = geom
    mrows, d = x.shape
    tm = _pick(tp, (RESIDUAL_TILE, ROW_TILE))
    npt = bp * tp // tm
    row_spec = pl.BlockSpec((tm, d), lambda i: (i, 0))
    body = functools.partial(_norm_body, modulate=modulate, shift_idx=shift_idx, scale_idx=scale_idx,
                             n_prompt_tiles=npt, ts=ts)
    return pl.pallas_call(
        body,
        grid=(npt + 1,),
        in_specs=[row_spec] + _mod_specs(mod_layer, geom, 0, lambda i: 0, d, tm)
        + [pl.BlockSpec((1, d), lambda i: (0, 0))],
        out_specs=row_spec,
        out_shape=jax.ShapeDtypeStruct((mrows, d), BF16 if modulate else F32),
        compiler_params=_cparams("arbitrary"),
        name="rows_norm",
    )(x, mod4, mod4, gain.reshape(1, d))


def _mm_body(a_ref, w_ref, o_ref, wb_ref):
    @pl.when(pl.program_id(1) == 0)
    def _():
        wb_ref[...] = w_ref[...].astype(BF16)

    o_ref[...] = jnp.dot(a_ref[...], wb_ref[...], preferred_element_type=F32).astype(o_ref.dtype)


def _mm_swiglu_body(a_ref, wa_ref, wg_ref, o_ref, wb_ref):
    tn = o_ref.shape[1]

    @pl.when(pl.program_id(1) == 0)
    def _():
        wb_ref[:, :tn] = wa_ref[...].astype(BF16)
        wb_ref[:, tn:] = wg_ref[...].astype(BF16)

    u = jnp.dot(a_ref[...], wb_ref[...], preferred_element_type=F32)
    o_ref[...] = (_silu(u[:, :tn]) * u[:, tn:]).astype(o_ref.dtype)


def _pick(n, cands):
    for c in cands:
        if n % c == 0:
            return c
    raise ValueError(f"no tile for {n} in {cands}")


def matmul(a, w, *, layer=None, n_cols=None, tm_cands=(1056, 768, 512, 256), tn=512, out_dtype=F32, name="matmul"):
    mrows, k = a.shape
    n = w.shape[-1] if n_cols is None else n_cols
    assert n % tn == 0
    tm = _pick(mrows, tm_cands)
    if w.ndim == 3:
        w_spec = pl.BlockSpec((None, k, tn), lambda j, i: (layer, 0, j))
    else:
        w_spec = pl.BlockSpec((k, tn), lambda j, i: (0, j))
    return pl.pallas_call(
        _mm_body,
        grid=(n // tn, mrows // tm),
        in_specs=[pl.BlockSpec((tm, k), lambda j, i: (i, 0)), w_spec],
        out_specs=pl.BlockSpec((tm, tn), lambda j, i: (i, j)),
        out_shape=jax.ShapeDtypeStruct((mrows, n), out_dtype),
        scratch_shapes=[pltpu.VMEM((k, tn), BF16)],
        compiler_params=_cparams("arbitrary", "arbitrary"),
        name=name,
    )(a, w)


def matmul_swiglu(a, w_in, layer, *, tn=512, tm_cands=(2112, 1056, 768, 512, 256)):
    mrows, k = a.shape
    f = w_in.shape[-1] // 2
    tm = _pick(mrows, tm_cands)
    nt = f // tn
    return pl.pallas_call(
        _mm_swiglu_body,
        grid=(nt, mrows // tm),
        in_specs=[pl.BlockSpec((tm, k), lambda j, i: (i, 0)),
                  pl.BlockSpec((None, k, tn), lambda j, i: (layer, 0, j)),
                  pl.BlockSpec((None, k, tn), lambda j, i: (layer, 0, j + nt))],
        out_specs=pl.BlockSpec((tm, tn), lambda j, i: (i, j)),
        out_shape=jax.ShapeDtypeStruct((mrows, f), BF16),
        scratch_shapes=[pltpu.VMEM((k, 2 * tn), BF16)],
        compiler_params=_cparams("arbitrary", "arbitrary"),
        name="ffn_in_swiglu",
    )(a, w_in, w_in)


def _mm_shift_body(a_ref, w_ref, wx_ref, o_ref, wb_ref, *, shift):
    @pl.when(pl.program_id(1) == 0)
    def _():
        tn = o_ref.shape[1]
        full = jnp.concatenate([w_ref[...], wx_ref[...]], axis=1)
        wb_ref[...] = full[:, shift:shift + tn].astype(BF16)

    o_ref[...] = jnp.dot(a_ref[...], wb_ref[...], preferred_element_type=F32)


def matmul_shift(a, w, layer, col0, n, *, tn=1024, tm_cands=(1056, 768, 512, 256), name="matmul_shift"):
    mrows, k = a.shape
    tm = _pick(mrows, tm_cands)
    shift = col0 % LANES
    base = col0 - shift
    assert base % tn == 0 and n % tn == 0 and shift > 0
    return pl.pallas_call(
        functools.partial(_mm_shift_body, shift=shift),
        grid=(n // tn, mrows // tm),
        in_specs=[pl.BlockSpec((tm, k), lambda j, i: (i, 0)),
                  pl.BlockSpec((None, k, tn), lambda j, i: (layer, 0, base // tn + j)),
                  pl.BlockSpec((None, k, LANES), lambda j, i: (layer, 0, (base + (j + 1) * tn) // LANES))],
        out_specs=pl.BlockSpec((tm, tn), lambda j, i: (i, j)),
        out_shape=jax.ShapeDtypeStruct((mrows, n), F32),
        scratch_shapes=[pltpu.VMEM((k, tn), BF16)],
        compiler_params=_cparams("arbitrary", "arbitrary"),
        name=name,
    )(a, w, w)


def _mm_cols_body(a_ref, w_ref, o_ref, wb_ref, *, n_valid):
    @pl.when(pl.program_id(0) == 0)
    def _():
        w = w_ref[...]
        if n_valid < LANES:
            lane = lax.broadcasted_iota(jnp.int32, w.shape, 1)
            w = jnp.where(lane < n_valid, w, 0.0)
        wb_ref[...] = w.astype(BF16)

    o_ref[...] = jnp.dot(a_ref[...], wb_ref[...], preferred_element_type=F32)


def matmul_cols128(a, w, layer, col0, *, tm_cands=(1056, 768, 512, 256), name="matmul_cols128"):
    mrows, k = a.shape
    tm = _pick(mrows, tm_cands)
    cb = col0 // LANES
    n_valid = min(LANES, w.shape[-1] - cb * LANES)
    return pl.pallas_call(
        functools.partial(_mm_cols_body, n_valid=n_valid),
        grid=(mrows // tm,),
        in_specs=[pl.BlockSpec((tm, k), lambda i: (i, 0)),
                  pl.BlockSpec((None, k, LANES), lambda i: (layer, 0, cb))],
        out_specs=pl.BlockSpec((tm, LANES), lambda i: (i, 0)),
        out_shape=jax.ShapeDtypeStruct((mrows, LANES), F32),
        scratch_shapes=[pltpu.VMEM((k, LANES), BF16)],
        compiler_params=_cparams("arbitrary"),
        name=name,
    )(a, w)


def _residual_tile(geom):
    return _pick(geom[1], (RESIDUAL_TILE, ROW_TILE))


def _gated_residual(x_ref, acc, gp_ref, gs_ref, o_ref, *, prompt, coef, gate_idx, ts):
    if prompt:
        o_ref[...] = x_ref[...] + coef * gp_ref[0, gate_idx:gate_idx + 1, :] * acc
    else:
        tn = acc.shape[1]
        g = gs_ref[:, gate_idx:gate_idx + 1, :]
        o = (x_ref[:ROW_TILE, :].reshape(ROW_TILE // ts, ts, tn)
             + coef * g * acc[:ROW_TILE].reshape(ROW_TILE // ts, ts, tn))
        o_ref[:ROW_TILE, :] = o.reshape(ROW_TILE, tn)


def _mm_pair_body(p1_ref, p2_ref, s1_ref, s2_ref, w_ref, x_ref, gp_ref, gs_ref, o_ref, wb_ref,
                  *, n_prompt_tiles, coef, gate_idx, ts):
    i = pl.program_id(1)
    k1 = p1_ref.shape[1]

    @pl.when(i == 0)
    def _():
        wb_ref[...] = w_ref[...].astype(BF16)

    def go(r1, r2, prompt):
        acc = (jnp.dot(r1[...], wb_ref[:k1, :], preferred_element_type=F32)
               + jnp.dot(r2[...], wb_ref[k1:, :], preferred_element_type=F32))
        _gated_residual(x_ref, acc, gp_ref, gs_ref, o_ref, prompt=prompt, coef=coef, gate_idx=gate_idx, ts=ts)

    @pl.when(i < n_prompt_tiles)
    def _():
        go(p1_ref, p2_ref, True)

    @pl.when(i >= n_prompt_tiles)
    def _():
        go(s1_ref, s2_ref, False)


def matmul_pair_residual(p1, p2, s1, s2, w, x, mod4, geom, *, gate_layer, gate_idx, coef, tn=1024,
                         name="matmul_pair"):
    mp, k1 = p1.shape
    k2 = p2.shape[1]
    n = w.shape[-1]
    tm = _residual_tile(geom)
    npt = mp // tm
    assert s1.shape[0] == ROW_TILE and x.shape == (mp + ROW_TILE, n)

    pmap = lambda j, i: (jnp.minimum(i, npt - 1), 0)
    smap = lambda j, i: (0, 0)
    tile = pl.BlockSpec((tm, tn), lambda j, i: (i, j))
    body = functools.partial(_mm_pair_body, n_prompt_tiles=npt, coef=coef, gate_idx=gate_idx, ts=geom[3])
    return pl.pallas_call(
        body,
        grid=(n // tn, npt + 1),
        in_specs=[pl.BlockSpec((tm, k1), pmap), pl.BlockSpec((tm, k2), pmap),
                  pl.BlockSpec((ROW_TILE, k1), smap), pl.BlockSpec((ROW_TILE, k2), smap),
                  pl.BlockSpec((k1 + k2, tn), lambda j, i: (0, j)), tile]
        + _mod_specs(gate_layer, geom, 1, lambda j, i: j, tn, tm),
        out_specs=tile,
        out_shape=jax.ShapeDtypeStruct((mp + ROW_TILE, n), F32),
        scratch_shapes=[pltpu.VMEM((k1 + k2, tn), BF16)],
        compiler_params=_cparams("arbitrary", "arbitrary"),
        name=name,
    )(p1, p2, s1, s2, w, x, mod4, mod4)


def _mm_res_body(a_ref, w_ref, x_ref, gp_ref, gs_ref, o_ref, wb_ref, *, n_prompt_tiles, coef, gate_idx, ts):
    i = pl.program_id(1)

    @pl.when(i == 0)
    def _():
        wb_ref[...] = w_ref[...].astype(BF16)

    acc = jnp.dot(a_ref[...], wb_ref[...], preferred_element_type=F32)

    @pl.when(i < n_prompt_tiles)
    def _():
        _gated_residual(x_ref, acc, gp_ref, gs_ref, o_ref, prompt=True, coef=coef, gate_idx=gate_idx, ts=ts)

    @pl.when(i >= n_prompt_tiles)
    def _():
        _gated_residual(x_ref, acc, gp_ref, gs_ref, o_ref, prompt=False, coef=coef, gate_idx=gate_idx, ts=ts)


def matmul_residual(a, w, layer, x, mod4, geom, *, gate_layer, gate_idx, coef, tn=512, name="matmul_residual"):
    bp, tp, _, ts = geom
    mrows, k = a.shape
    n = w.shape[-1]
    tm = _residual_tile(geom)
    npt = bp * tp // tm
    tile = pl.BlockSpec((tm, tn), lambda j, i: (i, j))
    body = functools.partial(_mm_res_body, n_prompt_tiles=npt, coef=coef, gate_idx=gate_idx, ts=ts)
    return pl.pallas_call(
        body,
        grid=(n // tn, npt + 1),
        in_specs=[pl.BlockSpec((tm, k), lambda j, i: (i, 0)),
                  pl.BlockSpec((None, k, tn), lambda j, i: (layer, 0, j)), tile]
        + _mod_specs(gate_layer, geom, 1, lambda j, i: j, tn, tm),
        out_specs=tile,
        out_shape=jax.ShapeDtypeStruct((mrows, n), F32),
        scratch_shapes=[pltpu.VMEM((k, tn), BF16)],
        compiler_params=_cparams("arbitrary", "arbitrary"),
        name=name,
    )(a, w, x, mod4, mod4)


def _conv_chunk(ext_ref, raw, w_ref, cin):
    ext_ref[SUBLANES:SUBLANES + CHUNK, :] = _pad_rows(raw, CHUNK)
    acc = None
    for j in range(CONV_W):
        off = SUBLANES - (CONV_W - 1) + j
        term = ext_ref[off:off + CHUNK, :] * w_ref[j:j + 1, :]
        acc = term if acc is None else acc + term
    return acc


def _conv_init(ext_ref, c0_ref, has_state):
    ext_ref[0:SUBLANES, :] = jnp.zeros((SUBLANES, ext_ref.shape[1]), F32)
    if has_state:
        ext_ref[SUBLANES - (CONV_W - 1):SUBLANES, :] = c0_ref[0]


def _conv_tail(ext_ref, cin):
    return ext_ref[cin + SUBLANES - (CONV_W - 1):cin + SUBLANES, :]


def _conv_advance(ext_ref):
    ext_ref[0:SUBLANES, :] = ext_ref[CHUNK:CHUNK + SUBLANES, :]


def _gla_body(q_ref, k_ref, v_ref, r_ref, g_ref, wg_ref, bg_ref, nrm_ref, s0_ref, o_ref, so_ref, st_ref,
              *, cin, has_state):
    c = pl.program_id(1)
    heads = range(GLA_HEADS)

    @pl.when(c == 0)
    def _():
        for h in heads:
            if has_state:
                st_ref[h] = s0_ref[0, h].T
            else:
                st_ref[h] = jnp.zeros(st_ref.shape[1:], F32)

    q = _pad_rows(q_ref[...], CHUNK) * (GLA_DK ** -0.5)
    k = _pad_rows(k_ref[...], CHUNK)
    v = _pad_rows(v_ref[...], CHUNK)
    glr = _pad_rows(g_ref[...], CHUNK)
    x = _dot(glr, wg_ref[...]) + bg_ref[...]
    logf = jax.nn.log_sigmoid(x) * (1.0 / GLA_TAU)
    if cin < CHUNK:
        rows = lax.broadcasted_iota(jnp.int32, logf.shape, 0)
        logf = jnp.where(rows < cin, logf, 0.0)
    b = _cumsum_rows(logf)
    bmid = b[CHUNK // 2 - 1:CHUNK // 2, :]
    blast = b[CHUNK - 1:CHUNK, :]
    incl, _, _ = _tri_masks(CHUNK)
    qe = q * jnp.exp(b - bmid)
    ke = k * jnp.exp(bmid - b)
    qb = q * jnp.exp(b)
    kw = k * jnp.exp(blast - b)
    sdec = jnp.exp(blast)

    def hk(a, h):
        return a[:, h * GLA_DK:(h + 1) * GLA_DK]

    def hv(a, h):
        return a[:, h * GLA_DV:(h + 1) * GLA_DV]

    sts = [st_ref[h] for h in heads]
    atts = [jnp.where(incl, _dot_nt(hk(qe, h), hk(ke, h)), 0.0) for h in heads]
    os_ = [_dot(atts[h], hv(v, h)) + _dot_nt(hk(qb, h), sts[h]) for h in heads]
    st_news = [sts[h] * hk(sdec, h) + _dot_tn(hv(v, h), hk(kw, h)) for h in heads]
    ys = []
    for h in heads:
        st_ref[h] = st_news[h]
        o = os_[h]
        var = jnp.mean(o * o, axis=-1, keepdims=True)
        ys.append(o * lax.rsqrt(var + EPS) * nrm_ref[...])
    y = jnp.concatenate(ys, axis=1)
    o_ref[0] = (y[:cin] * _silu(r_ref[...])).astype(o_ref.dtype)

    @pl.when(c == pl.num_programs(1) - 1)
    def _():
        for h in heads:
            so_ref[0, h] = st_news[h].T


def gla_mixer(pm, psm, wg_pad, b_gate, nrm, s0, *, row0, nb, t):
    cin = min(t, CHUNK)
    nc = t // cin
    rb0 = row0 // cin
    has_state = s0 is not None
    if s0 is None:
        s0 = jnp.zeros((1, GLA_HEADS, GLA_DK, GLA_DV), F32)
        s0_map = lambda b, c: (0, 0, 0, 0)
    else:
        s0_map = lambda b, c: (b, 0, 0, 0)

    def rmap(col0, width):
        return lambda b, c: (rb0 + b * nc + c, col0 // width)

    const = lambda b, c: (0, 0)
    body = functools.partial(_gla_body, cin=cin, has_state=has_state)
    return pl.pallas_call(
        body,
        grid=(nb, nc),
        in_specs=[pl.BlockSpec((cin, GLA_QK), rmap(EVA_Q, GLA_QK)),
                  pl.BlockSpec((cin, GLA_QK), rmap(EVA_K, GLA_QK)),
                  pl.BlockSpec((cin, GLA_VW), rmap(EVA_V, GLA_VW)),
                  pl.BlockSpec((cin, GLA_VW), rmap(EVA_R, GLA_VW)),
                  pl.BlockSpec((cin, LANES), lambda b, c: (rb0 + b * nc + c, 0)),
                  pl.BlockSpec((LANES, GLA_QK), const),
                  pl.BlockSpec((1, GLA_QK), const),
                  pl.BlockSpec((1, GLA_DV), const),
                  pl.BlockSpec((1, GLA_HEADS, GLA_DK, GLA_DV), s0_map)],
        out_specs=[pl.BlockSpec((1, cin, GLA_VW), lambda b, c: (b, c, 0)),
                   pl.BlockSpec((1, GLA_HEADS, GLA_DK, GLA_DV), lambda b, c: (b, 0, 0, 0))],
        out_shape=[jax.ShapeDtypeStruct((nb, t, GLA_VW), BF16),
                   jax.ShapeDtypeStruct((nb, GLA_HEADS, GLA_DK, GLA_DV), F32)],
        scratch_shapes=[pltpu.VMEM((GLA_HEADS, GLA_DV, GLA_DK), F32)],
        compiler_params=_cparams("arbitrary", "arbitrary"),
        name="gla_mixer",
    )(pm, pm, pm, pm, psm, wg_pad, b_gate.reshape(1, GLA_QK), nrm.reshape(1, GLA_DV), s0)


def _ssd_body(z_ref, x_ref, b_ref, c_ref, sm_ref, cwx_ref, cwb_ref, cwc_ref, cbx_ref, cbb_ref, cbc_ref,
              dtb_ref, alog_ref, dexp_ref, nrm_ref, e_ref, s0_ref, c0x_ref, c0b_ref, c0c_ref,
              y_ref, so_ref, cox_ref, cob_ref, coc_ref,
              st_ref, ex_ref, eb_ref, ec_ref, *, cin, has_state):
    c = pl.program_id(1)

    @pl.when(c == 0)
    def _():
        if has_state:
            st_ref[...] = s0_ref[0].T
        else:
            st_ref[...] = jnp.zeros(st_ref.shape, F32)
        _conv_init(ex_ref, c0x_ref, has_state)
        _conv_init(eb_ref, c0b_ref, has_state)
        _conv_init(ec_ref, c0c_ref, has_state)

    xs = _silu(_conv_chunk(ex_ref, x_ref[...], cwx_ref, cin) + cbx_ref[...])
    bm = _silu(_conv_chunk(eb_ref, b_ref[...], cwb_ref, cin) + cbb_ref[...])
    cm = _silu(_conv_chunk(ec_ref, c_ref[...], cwc_ref, cin) + cbc_ref[...])

    dt = jax.nn.softplus(_pad_rows(sm_ref[...], CHUNK) + dtb_ref[...])
    if cin < CHUNK:
        rows = lax.broadcasted_iota(jnp.int32, dt.shape, 0)
        dt = jnp.where(rows < cin, dt, 0.0)
    a = dt * (-jnp.exp(alog_ref[...]))
    cum = _cumsum_rows(a)
    cum_t = _rows_of(cum, SSD_DT_LANE0 + SSD_HEADS)
    last = cum[CHUNK - 1:CHUNK, :]
    stacked = jnp.concatenate([dt, jnp.exp(cum), jnp.exp(last - cum)], axis=0)
    ex = _dot_exact_rhs(stacked, e_ref[...])
    dt_e, ecum_e, w_e = ex[:CHUNK], ex[CHUNK:2 * CHUNK], ex[2 * CHUNK:]
    xdt = xs * dt_e
    xw = xdt * w_e
    incl, _, _ = _tri_masks(CHUNK)
    lane = lax.broadcasted_iota(jnp.int32, (CHUNK, LANES), 1)
    st = st_ref[...]
    gw = SSD_HPG * SSD_P
    groups = range(SSD_GROUPS)
    pairs = SSD_HPG // 2

    def grp(a, g, w):
        return a[:, g * w:(g + 1) * w]

    cbs = [_dot_nt(grp(cm, g, SSD_STATE), grp(bm, g, SSD_STATE)) for g in groups]
    inters = [_dot(grp(cm, g, SSD_STATE), grp(st, g, gw)) for g in groups]
    st_adds = [_dot_tn(grp(bm, g, SSD_STATE), grp(xw, g, gw)) for g in groups]
    att_pairs = []
    for g in groups:
        for pr in range(pairs):
            atts = []
            for h in (g * SSD_HPG + 2 * pr, g * SSD_HPG + 2 * pr + 1):
                ln = SSD_DT_LANE0 + h
                seg = cum[:, ln:ln + 1] - cum_t[ln:ln + 1, :]
                atts.append(cbs[g] * jnp.exp(jnp.where(incl, seg, NEG_BIG)))
            att_pairs.append(jnp.concatenate(atts, axis=0))
    yys = [_dot(att_pairs[p], xdt[:, p * LANES:(p + 1) * LANES]) for p in range(SSD_GROUPS * pairs)]
    ys = [jnp.where(lane < SSD_P, yy[:CHUNK], yy[CHUNK:]) for yy in yys]
    for g in groups:
        st_ref[:, g * gw:(g + 1) * gw] = grp(st, g, gw) * grp(ecum_e, g, gw)[CHUNK - 1:CHUNK] + st_adds[g]
    y = (jnp.concatenate(ys, axis=1) + jnp.concatenate(inters, axis=1) * ecum_e + xs * dexp_ref[...])
    y = y[:cin] * _silu(z_ref[...])
    var = jnp.mean(y * y, axis=-1, keepdims=True)
    y_ref[0] = (y * lax.rsqrt(var + EPS) * nrm_ref[...]).astype(y_ref.dtype)

    @pl.when(c == pl.num_programs(1) - 1)
    def _():
        so_ref[0] = st_ref[...].T
        cox_ref[0] = _conv_tail(ex_ref, cin)
        cob_ref[0] = _conv_tail(eb_ref, cin)
        coc_ref[0] = _conv_tail(ec_ref, cin)

    if cin == CHUNK:
        _conv_advance(ex_ref)
        _conv_advance(eb_ref)
        _conv_advance(ec_ref)


def ssd_mixer(pm, psm, conv_w, conv_b, dtb_pad, alog_pad, d_exp, nrm, expand, s0, c0, *, row0, nb, t):
    cin = min(t, CHUNK)
    nc = t // cin
    rb0 = row0 // cin
    has_state = s0 is not None
    hp = SSD_INNER
    if s0 is None:
        s0 = jnp.zeros((1, hp, SSD_STATE), F32)
        c0 = jnp.zeros((1, CONV_W - 1, SSD_CONV_DIM), F32)
        bmap = lambda b, c: 0
    else:
        s0 = s0.reshape(nb, hp, SSD_STATE)
        bmap = lambda b, c: b

    def rmap(col0, width):
        return lambda b, c: (rb0 + b * nc + c, col0 // width)

    const = lambda b, c: (0, 0)
    kconv = CONV_W - 1
    body = functools.partial(_ssd_body, cin=cin, has_state=has_state)
    outs = pl.pallas_call(
        body,
        grid=(nb, nc),
        in_specs=[pl.BlockSpec((cin, hp), rmap(EVB_Z, hp)),
                  pl.BlockSpec((cin, hp), rmap(EVB_X, hp)),
                  pl.BlockSpec((cin, SSD_BC), rmap(EVB_B, SSD_BC)),
                  pl.BlockSpec((cin, SSD_BC), rmap(EVB_C, SSD_BC)),
                  pl.BlockSpec((cin, LANES), lambda b, c: (rb0 + b * nc + c, 0)),
                  pl.BlockSpec((CONV_W, hp), lambda b, c: (0, 0)),
                  pl.BlockSpec((CONV_W, SSD_BC), lambda b, c: (0, hp // SSD_BC)),
                  pl.BlockSpec((CONV_W, SSD_BC), lambda b, c: (0, hp // SSD_BC + 1)),
                  pl.BlockSpec((1, hp), lambda b, c: (0, 0)),
                  pl.BlockSpec((1, SSD_BC), lambda b, c: (0, hp // SSD_BC)),
                  pl.BlockSpec((1, SSD_BC), lambda b, c: (0, hp // SSD_BC + 1)),
                  pl.BlockSpec((1, LANES), const),
                  pl.BlockSpec((1, LANES), const),
                  pl.BlockSpec((1, hp), const),
                  pl.BlockSpec((1, hp), const),
                  pl.BlockSpec((LANES, hp), const),
                  pl.BlockSpec((1, hp, SSD_STATE), lambda b, c: (bmap(b, c), 0, 0)),
                  pl.BlockSpec((1, kconv, hp), lambda b, c: (bmap(b, c), 0, 0)),
                  pl.BlockSpec((1, kconv, SSD_BC), lambda b, c: (bmap(b, c), 0, hp // SSD_BC)),
                  pl.BlockSpec((1, kconv, SSD_BC), lambda b, c: (bmap(b, c), 0, hp // SSD_BC + 1))],
        out_specs=[pl.BlockSpec((1, cin, hp), lambda b, c: (b, c, 0)),
                   pl.BlockSpec((1, hp, SSD_STATE), lambda b, c: (b, 0, 0)),
                   pl.BlockSpec((1, kconv, hp), lambda b, c: (b, 0, 0)),
                   pl.BlockSpec((1, kconv, SSD_BC), lambda b, c: (b, 0, 0)),
                   pl.BlockSpec((1, kconv, SSD_BC), lambda b, c: (b, 0, 0))],
        out_shape=[jax.ShapeDtypeStruct((nb, t, hp), BF16),
                   jax.ShapeDtypeStruct((nb, hp, SSD_STATE), F32),
                   jax.ShapeDtypeStruct((nb, kconv, hp), F32),
                   jax.ShapeDtypeStruct((nb, kconv, SSD_BC), F32),
                   jax.ShapeDtypeStruct((nb, kconv, SSD_BC), F32)],
        scratch_shapes=[pltpu.VMEM((SSD_STATE, hp), F32),
                        pltpu.VMEM((CHUNK + SUBLANES, hp), F32),
                        pltpu.VMEM((CHUNK + SUBLANES, SSD_BC), F32),
                        pltpu.VMEM((CHUNK + SUBLANES, SSD_BC), F32)],
        compiler_params=_cparams("arbitrary", "arbitrary"),
        name="ssd_mixer",
    )(pm, pm, pm, pm, psm, conv_w, conv_w, conv_w, conv_b, conv_b, conv_b, dtb_pad, alog_pad, d_exp,
      nrm.reshape(1, hp), expand, s0, c0, c0, c0)
    y, s_new, cx, cb_, cc = outs
    conv_s = jnp.concatenate([cx, cb_, cc], axis=-1)
    return y, s_new.reshape(nb, SSD_HEADS, SSD_P, SSD_STATE), conv_s


def _l2norm_heads(x, n_heads, width, scale):
    outs = []
    for e in range(n_heads):
        xe = x[:, e * width:(e + 1) * width]
        ss = jnp.sum(xe * xe, axis=-1, keepdims=True)
        outs.append(xe * (lax.rsqrt(ss + EPS) * scale))
    return outs


def _unit_lower_inverse(a_mats, eye, order):
    def mm(a, b):
        return jnp.dot(a, b, preferred_element_type=F32)

    ts = [eye - a for a in a_mats]
    pbs = [(-a).astype(BF16) for a in a_mats]
    for _ in range(max(order - 1, 1).bit_length() - 1):
        ps = [mm(pb, pb) for pb in pbs]
        pbs = [p.astype(BF16) for p in ps]
        ts = [t + mm(t.astype(BF16), pb) for t, pb in zip(ts, pbs)]
    outs = []
    splits = [(_split2(a), _split2(t)) for a, t in zip(a_mats, ts)]
    ats = [mm(ah, th) + mm(ah, tl) + mm(al, th) for (ah, al), (th, tl) in splits]
    for t, at, (_, (th, _)) in zip(ts, ats, splits):
        resid = eye - t - at
        outs.append(t + mm(th, resid.astype(BF16)))
    return outs


def _gdn_body(q_ref, k_ref, v_ref, z_ref, sm_ref, cwq_ref, cwk_ref, cwv_ref, alog_ref, dtb_ref, nrm_ref,
              s0_ref, c0q_ref, c0k_ref, c0v_ref,
              o_ref, so_ref, coq_ref, cok_ref, cov_ref,
              s_ref, eq_ref, ek_ref, ev_ref, *, cin, has_state):
    c = pl.program_id(1)
    n = GDN_PACK * CHUNK

    @pl.when(c == 0)
    def _():
        if has_state:
            s_ref[...] = s0_ref[0]
        else:
            s_ref[...] = jnp.zeros(s_ref.shape, F32)
        _conv_init(eq_ref, c0q_ref, has_state)
        _conv_init(ek_ref, c0k_ref, has_state)
        _conv_init(ev_ref, c0v_ref, has_state)

    qc = _silu(_conv_chunk(eq_ref, q_ref[...], cwq_ref, cin))
    kc = _silu(_conv_chunk(ek_ref, k_ref[...], cwk_ref, cin))
    vc = _silu(_conv_chunk(ev_ref, v_ref[...], cwv_ref, cin))
    qs = _l2norm_heads(qc, GDN_QK_HEADS, GDN_DK, GDN_DK ** -0.5)
    ks = _l2norm_heads(kc, GDN_QK_HEADS, GDN_DK, 1.0)

    sm = _pad_rows(sm_ref[...], CHUNK)
    beta = jax.nn.sigmoid(sm)
    gl = -jnp.exp(alog_ref[...]) * jax.nn.softplus(sm + dtb_ref[...])
    if cin < CHUNK:
        rows = lax.broadcasted_iota(jnp.int32, sm.shape, 0)
        beta = jnp.where(rows < cin, beta, 0.0)
        gl = jnp.where(rows < cin, gl, 0.0)
    cum = _cumsum_rows(gl)
    last_row = cum[CHUNK - 1:CHUNK, :]

    rr = lax.broadcasted_iota(jnp.int32, (n, n), 0)
    cc = lax.broadcasted_iota(jnp.int32, (n, n), 1)
    same = (rr // CHUNK) == (cc // CHUNK)
    incl = same & (rr >= cc)
    strict = same & (rr > cc)
    eye = jnp.where(rr == cc, 1.0, 0.0)
    head_of_row = lax.broadcasted_iota(jnp.int32, (n, LANES), 0) // CHUNK
    lane = lax.broadcasted_iota(jnp.int32, (n, LANES), 1)
    bd_mask = (lax.broadcasted_iota(jnp.int32, (n, GDN_PACK * GDN_DK), 0) // CHUNK
               == lax.broadcasted_iota(jnp.int32, (n, GDN_PACK * GDN_DK), 1) // GDN_DK)
    srow = lax.broadcasted_iota(jnp.int32, (GDN_PACK * GDN_DK, LANES), 0) // GDN_DK
    slane = lax.broadcasted_iota(jnp.int32, (GDN_PACK * GDN_DK, LANES), 1)

    def block_diag(x):
        return jnp.where(bd_mask, jnp.concatenate([x] * GDN_PACK, axis=1), 0.0)

    def stack(parts):
        return jnp.concatenate(parts, axis=0)

    cum_rep = stack([cum] * GDN_PACK)
    beta_rep = stack([beta] * GDN_PACK)
    last_rep = jnp.broadcast_to(last_row, (n, LANES))
    cum_t = _rows_of(cum, 2 * GDN_V_HEADS)
    packs = range(GDN_V_HEADS // GDN_PACK)

    pre = []
    for j in packs:
        h0 = j * GDN_PACK
        heads = range(h0, h0 + GDN_PACK)
        k_st = stack([ks[i // GDN_REP] for i in heads])
        q_st = stack([qs[i // GDN_REP] for i in heads])
        v_st = stack([vc[:, i * GDN_DV:(i + 1) * GDN_DV] for i in heads])
        g_sel = lane == GDN_V_HEADS + h0 + head_of_row
        col = jnp.sum(jnp.where(g_sel, cum_rep, 0.0), axis=1, keepdims=True)
        rowv = jnp.concatenate([cum_t[GDN_V_HEADS + i:GDN_V_HEADS + i + 1, :] for i in heads], axis=1)
        bcol = jnp.sum(jnp.where(lane == h0 + head_of_row, beta_rep, 0.0), axis=1, keepdims=True)
        lastc = jnp.sum(jnp.where(g_sel, last_rep, 0.0), axis=1, keepdims=True)
        dec = jnp.exp(jnp.where(incl, col - rowv, NEG_BIG))
        ecol = jnp.exp(col)
        s_st = s_ref[h0:h0 + GDN_PACK].reshape(GDN_PACK * GDN_DK, GDN_DV)
        pre.append(dict(k=k_st, q=q_st, v=v_st, col=col, bcol=bcol, lastc=lastc, dec=dec, ecol=ecol, s=s_st))
    kks = [_dot_nt(p["k"], p["k"]) for p in pre]
    qks = [_dot_nt(p["q"], p["k"]) for p in pre]
    a_mats = [jnp.where(strict, p["bcol"] * kk * p["dec"], 0.0) for p, kk in zip(pre, kks)]
    t_invs = _unit_lower_inverse(a_mats, eye, cin)
    rhss = [jnp.concatenate([p["v"] * p["bcol"], p["k"] * (p["bcol"] * p["ecol"])], axis=1) for p in pre]
    sols = [_dot(t, r) for t, r in zip(t_invs, rhss)]
    deltas = [sol[:, :GDN_DV] - _dot(block_diag(sol[:, GDN_DV:]), p["s"]) for sol, p in zip(sols, pre)]
    o_sts = [_dot(block_diag(p["q"] * p["ecol"]), p["s"]) + _dot(jnp.where(incl, qk * p["dec"], 0.0), dl)
             for p, qk, dl in zip(pre, qks, deltas)]
    outs = []
    for j, (p, dl, o_st) in enumerate(zip(pre, deltas, o_sts)):
        h0 = j * GDN_PACK
        kw_bd = block_diag(p["k"] * jnp.exp(p["lastc"] - p["col"]))
        s_dec = jnp.exp(jnp.sum(jnp.where(slane == GDN_V_HEADS + h0 + srow,
                                          jnp.broadcast_to(last_row, srow.shape), 0.0), axis=1, keepdims=True))
        s_new = p["s"] * s_dec + _dot_tn(kw_bd, dl)
        s_ref[h0:h0 + GDN_PACK] = s_new.reshape(GDN_PACK, GDN_DK, GDN_DV)
        var = jnp.mean(o_st * o_st, axis=-1, keepdims=True)
        y_st = o_st * lax.rsqrt(var + EPS) * nrm_ref[...]
        outs += [y_st[i * CHUNK:(i + 1) * CHUNK] for i in range(GDN_PACK)]
    o = jnp.concatenate(outs, axis=1)
    o_ref[0] = (o[:cin] * _silu(z_ref[...])).astype(o_ref.dtype)

    @pl.when(c == pl.num_programs(1) - 1)
    def _():
        so_ref[0] = s_ref[...]
        coq_ref[0] = _conv_tail(eq_ref, cin)
        cok_ref[0] = _conv_tail(ek_ref, cin)
        cov_ref[0] = _conv_tail(ev_ref, cin)

    if cin == CHUNK:
        _conv_advance(eq_ref)
        _conv_advance(ek_ref)
        _conv_advance(ev_ref)


def gdn_mixer(pm, psm, conv_w, alog_pad, dtb_pad, nrm, s0, c0, *, row0, nb, t):
    cin = min(t, CHUNK)
    nc = t // cin
    rb0 = row0 // cin
    has_state = s0 is not None
    kconv = CONV_W - 1
    if s0 is None:
        s0 = jnp.zeros((1, GDN_V_HEADS, GDN_DK, GDN_DV), F32)
        c0 = jnp.zeros((1, kconv, GDN_CONV_DIM), F32)
        bmap = lambda b: 0
    else:
        bmap = lambda b: b

    def rmap(col0, width):
        return lambda b, c: (rb0 + b * nc + c, col0 // width)

    const = lambda b, c: (0, 0)
    body = functools.partial(_gdn_body, cin=cin, has_state=has_state)
    outs = pl.pallas_call(
        body,
        grid=(nb, nc),
        in_specs=[pl.BlockSpec((cin, GDN_QK), rmap(ODA_Q, GDN_QK)),
                  pl.BlockSpec((cin, GDN_QK), rmap(ODA_K, GDN_QK)),
                  pl.BlockSpec((cin, GDN_VW), rmap(ODA_V, GDN_VW)),
                  pl.BlockSpec((cin, GDN_VW), rmap(ODA_Z, GDN_VW)),
                  pl.BlockSpec((cin, LANES), lambda b, c: (rb0 + b * nc + c, 0)),
                  pl.BlockSpec((CONV_W, GDN_QK), lambda b, c: (0, 0)),
                  pl.BlockSpec((CONV_W, GDN_QK), lambda b, c: (0, 1)),
                  pl.BlockSpec((CONV_W, GDN_VW), lambda b, c: (0, 1)),
                  pl.BlockSpec((1, LANES), const),
                  pl.BlockSpec((1, LANES), const),
                  pl.BlockSpec((1, GDN_DV), const),
                  pl.BlockSpec((1, GDN_V_HEADS, GDN_DK, GDN_DV), lambda b, c: (bmap(b), 0, 0, 0)),
                  pl.BlockSpec((1, kconv, GDN_QK), lambda b, c: (bmap(b), 0, 0)),
                  pl.BlockSpec((1, kconv, GDN_QK), lambda b, c: (bmap(b), 0, 1)),
                  pl.BlockSpec((1, kconv, GDN_VW), lambda b, c: (bmap(b), 0, 1))],
        out_specs=[pl.BlockSpec((1, cin, GDN_VW), lambda b, c: (b, c, 0)),
                   pl.BlockSpec((1, GDN_V_HEADS, GDN_DK, GDN_DV), lambda b, c: (b, 0, 0, 0)),
                   pl.BlockSpec((1, kconv, GDN_QK), lambda b, c: (b, 0, 0)),
                   pl.BlockSpec((1, kconv, GDN_QK), lambda b, c: (b, 0, 0)),
                   pl.BlockSpec((1, kconv, GDN_VW), lambda b, c: (b, 0, 0))],
        out_shape=[jax.ShapeDtypeStruct((nb, t, GDN_VW), BF16),
                   jax.ShapeDtypeStruct((nb, GDN_V_HEADS, GDN_DK, GDN_DV), F32),
                   jax.ShapeDtypeStruct((nb, kconv, GDN_QK), F32),
                   jax.ShapeDtypeStruct((nb, kconv, GDN_QK), F32),
                   jax.ShapeDtypeStruct((nb, kconv, GDN_VW), F32)],
        scratch_shapes=[pltpu.VMEM((GDN_V_HEADS, GDN_DK, GDN_DV), F32),
                        pltpu.VMEM((CHUNK + SUBLANES, GDN_QK), F32),
                        pltpu.VMEM((CHUNK + SUBLANES, GDN_QK), F32),
                        pltpu.VMEM((CHUNK + SUBLANES, GDN_VW), F32)],
        compiler_params=_cparams("arbitrary", "arbitrary"),
        name="gdn_mixer",
    )(pm, pm, pm, pm, psm, conv_w, conv_w, conv_w, alog_pad, dtb_pad, nrm.reshape(1, GDN_DV),
      s0, c0, c0, c0)
    o, s_new, cq, ck, cv = outs
    return o, s_new, jnp.concatenate([cq, ck, cv], axis=-1)


def _residue_perm(dil, inverse=False):
    n = ROW_TILE // dil
    i = lax.broadcasted_iota(jnp.int32, (ROW_TILE, ROW_TILE), 0)
    j = lax.broadcasted_iota(jnp.int32, (ROW_TILE, ROW_TILE), 1)
    if inverse:
        i, j = j, i
    return jnp.where(j == (i % n) * dil + i // n, 1.0, 0.0).astype(BF16)


def _rope_body(q_ref, k_ref, v_ref, inv_ref, qo_ref, ko_ref, *rest, n_prompt_tiles, tiles_per_seq, ts):
    i = pl.program_id(0)
    rt = q_ref.shape[0]
    r = lax.broadcasted_iota(jnp.int32, (rt, LANES), 0)
    pos_prompt = (i % tiles_per_seq) * rt + r
    pos_sample = PAST_LEN + r % ts
    pos = jnp.where(i < n_prompt_tiles, pos_prompt, pos_sample).astype(F32)
    ang = pos * inv_ref[...]
    cos = jnp.cos(ang)
    sin = jnp.sin(ang)
    lane = lax.broadcasted_iota(jnp.int32, (rt, LANES), 1)
    sin = jnp.where(lane < LANES // 2, -sin, sin)
    for src, dst in ((q_ref, qo_ref), (k_ref, ko_ref)):
        for h in range(DSA_W // DSA_HD):
            x = src[:, h * DSA_HD:(h + 1) * DSA_HD]
            dst[:, h * DSA_HD:(h + 1) * DSA_HD] = x * cos + pltpu.roll(x, DSA_HD // 2, 1) * sin
    for g, (_, dil) in enumerate(DSA_GROUPS):
        perm = None if dil == 1 else _residue_perm(dil)
        for src, dst in zip((qo_ref, ko_ref, v_ref), rest[3 * g:3 * g + 3]):
            xb = src[:, g * DSA_GW:(g + 1) * DSA_GW].astype(BF16)
            if perm is not None:
                xb = jnp.dot(perm, xb, preferred_element_type=F32).astype(BF16)
            dst[...] = xb.reshape(dst.shape)


def rope_qk(pm, geom):
    bp, tp, bs, ts = geom
    mrows = pm.shape[0]
    half = DSA_HD // 2
    inv = ROPE_THETA ** (-jnp.arange(half, dtype=F32) / half)
    inv2 = jnp.concatenate([inv, inv]).reshape(1, DSA_HD)
    body = functools.partial(_rope_body, n_prompt_tiles=bp * tp // ROW_TILE, tiles_per_seq=tp // ROW_TILE, ts=ts)
    spec_o = pl.BlockSpec((ROW_TILE, DSA_W), lambda i: (i, 0))
    out_specs = [spec_o, spec_o]
    out_shape = [jax.ShapeDtypeStruct((mrows, DSA_W), F32)] * 2
    for _, dil in DSA_GROUPS:
        n = ROW_TILE // dil
        out_specs += [pl.BlockSpec((dil, n, DSA_GW), lambda i: (0, i, 0))] * 3
        out_shape += [jax.ShapeDtypeStruct((dil, mrows // dil, DSA_GW), BF16)] * 3
    outs = pl.pallas_call(
        body,
        grid=(mrows // ROW_TILE,),
        in_specs=[pl.BlockSpec((ROW_TILE, DSA_W), lambda i: (i, ODB_DQ // DSA_W)),
                  pl.BlockSpec((ROW_TILE, DSA_W), lambda i: (i, ODB_DK // DSA_W)),
                  pl.BlockSpec((ROW_TILE, DSA_W), lambda i: (i, ODB_DV // DSA_W)),
                  pl.BlockSpec((1, DSA_HD), lambda i: (0, 0))],
        out_specs=out_specs,
        out_shape=out_shape,
        compiler_params=_cparams("arbitrary"),
        name="rope_qk",
    )(pm, pm, pm, inv2)
    return outs[0], outs[1], [tuple(outs[2 + 3 * g:5 + 3 * g]) for g in range(len(DSA_GROUPS))]


def _lse_lanes(lses):
    rows = lses[0].shape[0]
    lane = lax.broadcasted_iota(jnp.int32, (rows, LANES), 1)
    out = jnp.zeros((rows, LANES), F32)
    for h, v in enumerate(lses):
        out = jnp.where(lane == h, v, out)
    return out


def _dsa_prompt_body(q_ref, kp_ref, kc_ref, vp_ref, vc_ref, o_ref, l_ref, *, jmax):
    n = pl.program_id(2)
    blk = DSA_BLOCK
    i_idx = lax.broadcasted_iota(jnp.int32, (blk, 2 * blk), 0)
    m_idx = lax.broadcasted_iota(jnp.int32, (blk, 2 * blk), 1)
    dist = i_idx + blk - m_idx
    valid = (dist >= 0) & (dist <= jmax) & ((m_idx >= blk) | (n > 0))
    scale = DSA_HD ** -0.5
    lses = []
    for h in range(DSA_HEADS):
        sl = slice(h * DSA_HD, (h + 1) * DSA_HD)
        kk = jnp.concatenate([kp_ref[:, sl], kc_ref[:, sl]], axis=0)
        vv = jnp.concatenate([vp_ref[:, sl], vc_ref[:, sl]], axis=0)
        s = _dot_nt(q_ref[:, sl], kk) * scale
        s = jnp.where(valid, s, NEG_BIG)
        mx = jnp.max(s, axis=-1, keepdims=True)
        p = jnp.exp(s - mx)
        den = jnp.sum(p, axis=-1, keepdims=True)
        o_ref[:, sl] = (_dot(p, vv) / den).astype(o_ref.dtype)
        lses.append(mx + jnp.log(den))
    l_ref[...] = _lse_lanes(lses)


def dsa_prompt(qkv, g, geom):
    bp, tp, _, _ = geom
    window, dil = DSA_GROUPS[g]
    ls = tp // dil
    nblk = ls // DSA_BLOCK
    q, k, v = qkv
    mu = bp * ls

    cur = lambda b, r, n: (r, b * nblk + n, 0)
    prev = lambda b, r, n: (r, b * nblk + jnp.maximum(n - 1, 0), 0)
    blk = (None, DSA_BLOCK, DSA_GW)
    return pl.pallas_call(
        functools.partial(_dsa_prompt_body, jmax=window // dil),
        grid=(bp, dil, nblk),
        in_specs=[pl.BlockSpec(blk, cur), pl.BlockSpec(blk, prev), pl.BlockSpec(blk, cur),
                  pl.BlockSpec(blk, prev), pl.BlockSpec(blk, cur)],
        out_specs=[pl.BlockSpec(blk, cur), pl.BlockSpec((None, DSA_BLOCK, LANES), cur)],
        out_shape=[jax.ShapeDtypeStruct((dil, mu, DSA_GW), BF16),
                   jax.ShapeDtypeStruct((dil, mu, LANES), F32)],
        compiler_params=_cparams("arbitrary", "arbitrary", "arbitrary"),
        name=f"dsa_prompt_w{window}",
    )(q, k, k, v, v)


def _dsa_sample_body(q_ref, kn_ref, vn_ref, cache_ref, o_ref, l_ref, m_scr, d_scr, acc_scr, *, window, dil):
    j = pl.program_id(1)
    nj = pl.num_programs(1)
    ts = q_ref.shape[0]
    nh = DSA_HEADS
    scale = DSA_HD ** -0.5

    def heads_to_rows(ref):
        return jnp.concatenate([ref[:, h * DSA_HD:(h + 1) * DSA_HD] for h in range(nh)], axis=0)

    @pl.when(j == 0)
    def _():
        m_scr[...] = jnp.full(m_scr.shape, NEG_BIG, F32)
        d_scr[...] = jnp.zeros(d_scr.shape, F32)
        acc_scr[...] = jnp.zeros(acc_scr.shape, F32)

    def update(s, vals, ok):
        s = jnp.where(ok, s, NEG_BIG)
        m_old = m_scr[...]
        m_new = jnp.maximum(m_old, jnp.max(s, axis=-1, keepdims=True))
        alpha = jnp.exp(m_old - m_new)
        p = jnp.where(ok, jnp.exp(s - m_new[:, 0:1]), 0.0)
        d_scr[...] = alpha * d_scr[...] + jnp.sum(p, axis=-1, keepdims=True)
        acc_scr[...] = alpha * acc_scr[...] + _dot(p, vals)
        m_scr[...] = m_new

    q_rows = heads_to_rows(q_ref) * scale
    ub, n_res = cache_ref.shape[1], cache_ref.shape[2] // (2 * nh)
    lb = ub * n_res
    x4 = cache_ref[0].reshape(lb, 2, nh, DSA_HD)
    k_all = x4[:, 0].reshape(lb * nh, DSA_HD)
    v_all = x4[:, 1].reshape(lb * nh, DSA_HD)
    r = lax.broadcasted_iota(jnp.int32, (nh * ts, lb * nh), 0)
    c = lax.broadcasted_iota(jnp.int32, (nh * ts, lb * nh), 1)
    pos = c // nh
    dist = window + r % ts - ((j * ub + pos // n_res) * dil + pos % n_res)
    ok = (r // ts == c % nh) & ((dist & (dil - 1)) == 0) & (dist <= window)
    update(_dot_nt(q_rows, k_all), v_all, ok)

    @pl.when(j == nj - 1)
    def _():
        r2 = lax.broadcasted_iota(jnp.int32, (nh * ts, nh * ts), 0)
        c2 = lax.broadcasted_iota(jnp.int32, (nh * ts, nh * ts), 1)
        d2 = r2 % ts - c2 % ts
        ok2 = (r2 // ts == c2 // ts) & (d2 >= 0) & ((d2 & (dil - 1)) == 0)
        update(_dot_nt(q_rows, heads_to_rows(kn_ref)), heads_to_rows(vn_ref), ok2)
        o_rows = acc_scr[...] / d_scr[...]
        lse = m_scr[...] + jnp.log(d_scr[...])
        for h in range(nh):
            o_ref[:, h * DSA_HD:(h + 1) * DSA_HD] = o_rows[h * ts:(h + 1) * ts]
        l_ref[...] = _lse_lanes([lse[h * ts:(h + 1) * ts, 0:1] for h in range(nh)])


def dsa_sample(rq, rk, pm, cache, g, geom):
    bp, tp, bs, ts = geom
    window, dil = DSA_GROUPS[g]
    lcache = cache.shape[1]
    assert lcache == window and window % dil == 0, "cache must hold exactly the window"
    n_res = min(dil, ts)
    strides = lcache // dil
    ub = min(strides, 512 // n_res)
    rb0 = bp * tp // ts
    kv_rows = 2 * DSA_HEADS
    cache4 = cache.reshape(bs, strides, dil * kv_rows, DSA_HD)
    blk = (ts, DSA_GW)
    body = functools.partial(_dsa_sample_body, window=window, dil=dil)
    return pl.pallas_call(
        body,
        grid=(bs, strides // ub),
        in_specs=[pl.BlockSpec(blk, lambda b, j: (rb0 + b, g)),
                  pl.BlockSpec(blk, lambda b, j: (rb0 + b, g)),
                  pl.BlockSpec(blk, lambda b, j: (rb0 + b, ODB_DV // DSA_GW + g)),
                  pl.BlockSpec((1, ub, n_res * kv_rows, DSA_HD), lambda b, j: (b, j, 0, 0))],
        out_specs=[pl.BlockSpec(blk, lambda b, j: (b, 0)), pl.BlockSpec((ts, LANES), lambda b, j: (b, 0))],
        out_shape=[jax.ShapeDtypeStruct((bs * ts, DSA_GW), F32),
                   jax.ShapeDtypeStruct((bs * ts, LANES), F32)],
        scratch_shapes=[pltpu.VMEM((DSA_HEADS * ts, DSA_HD), F32)] * 3,
        compiler_params=_cparams("arbitrary", "arbitrary"),
        name=f"dsa_sample_w{window}",
    )(rq, rk, pm, cache4)


def _dsa_merge_body(o0, o1, o2, l0, l1, l2, e_ref, out_ref, *, dils):
    os_, ls_ = [], []
    for o_ref, l_ref, dil in zip((o0, o1, o2), (l0, l1, l2), dils):
        o = o_ref[...].reshape(ROW_TILE, DSA_GW)
        l = l_ref[...].reshape(ROW_TILE, LANES)
        if dil > 1:
            pinv = _residue_perm(dil, inverse=True)
            o = jnp.dot(pinv, o.astype(BF16), preferred_element_type=F32)
            l = _dot_exact_lhs(pinv, l)
        os_.append(o.astype(F32))
        ls_.append(l)
    mx = jnp.maximum(jnp.maximum(ls_[0], ls_[1]), ls_[2])
    ws = [jnp.exp(l - mx) for l in ls_]
    tot = ws[0] + ws[1] + ws[2]
    acc = None
    for w, o in zip(ws, os_):
        term = _dot_exact_rhs(w / tot, e_ref[...]) * o
        acc = term if acc is None else acc + term
    out_ref[...] = acc.astype(out_ref.dtype)


def dsa_merge(os_, ls_, dils, n_tiles):
    in_specs = []
    for width in (DSA_GW, LANES):
        for dil in dils:
            if dil:
                in_specs.append(pl.BlockSpec((dil, ROW_TILE // dil, width), lambda i: (0, i, 0)))
            else:
                in_specs.append(pl.BlockSpec((ROW_TILE, width), lambda i: (i, 0)))
    in_specs.append(pl.BlockSpec((LANES, DSA_GW), lambda i: (0, 0)))
    expand = jnp.asarray((np.arange(LANES)[:, None] == (np.arange(DSA_GW)[None, :] // DSA_HD)).astype(np.float32))
    return pl.pallas_call(
        functools.partial(_dsa_merge_body, dils=tuple(max(d, 1) for d in dils)),
        grid=(n_tiles,),
        in_specs=in_specs,
        out_specs=pl.BlockSpec((ROW_TILE, DSA_GW), lambda i: (i, 0)),
        out_shape=jax.ShapeDtypeStruct((n_tiles * ROW_TILE, DSA_GW), BF16),
        compiler_params=_cparams("arbitrary"),
        name="dsa_merge",
    )(*os_, *ls_, expand)


def _pad_lanes(vec, offset=0):
    return jnp.zeros((1, LANES), F32).at[0, offset:offset + vec.shape[0]].set(vec)


def kernel(x_prompt, x_sample, c_prompt, c_sample, state_gla, state_ssd, state_ssd_conv, state_gdn, state_gdn_conv, cache_dsa_w128, cache_dsa_w512, cache_dsa_w2048, ln_ffn1, ln_mix, ln_ffn2, w_ada, b_ada, ffn1_w_in, ffn1_w_out, ffn2_w_in, ffn2_w_out, even_w_in, gla_w_gate2, gla_b_gate, gla_norm, ssd_conv_w, ssd_conv_b, ssd_dt_bias, ssd_A_log, ssd_D, ssd_norm, even_w_out, odd_w_in, gdn_conv_w, gdn_A_log, gdn_dt_bias, gdn_norm, odd_w_out, final_norm):
    bp, tp, d = x_prompt.shape
    bs, ts, _ = x_sample.shape
    depth = w_ada.shape[0]
    geom = (bp, tp, bs, ts)
    assert tp % ROW_TILE == 0 and bs * ts == ROW_TILE and ts == SUBLANES
    mp = bp * tp
    caches = (cache_dsa_w128, cache_dsa_w512, cache_dsa_w2048)

    x = jnp.concatenate([x_prompt.reshape(mp, d), x_sample.reshape(bs * ts, d)], axis=0)
    n_pad = (-(bs + bp)) % SUBLANES
    c_all = jnp.concatenate([c_sample, c_prompt, jnp.zeros((n_pad, d), F32)], axis=0)
    mod4 = ada_mod(c_all, w_ada, b_ada).reshape(depth, c_all.shape[0], N_MOD, d)
    expand = jnp.asarray((np.arange(LANES)[:, None] - SSD_DT_LANE0
                          == (np.arange(SSD_INNER)[None, :] // SSD_P)).astype(np.float32))

    new ={k: ([], []) for k in ("gla", "ssd", "ssd_conv", "gdn", "gdn_conv", "dsa0", "dsa1", "dsa2")}

    h = rows_norm(x, mod4, ln_ffn1[0], geom, mod_layer=0, shift_idx=0, scale_idx=1)
    for l in range(depth):
        act = matmul_swiglu(h, ffn1_w_in, l)
        x = matmul_residual(act, ffn1_w_out, l, x, mod4, geom, gate_layer=l, gate_idx=2, coef=0.5, name="ffn_out")
        h = rows_norm(x, mod4, ln_mix[l], geom, mod_layer=l, shift_idx=3, scale_idx=4)
        i = l // 2
        if l % 2 == 0:
            assert even_w_in.shape[-1] == EVEN_IN
            pma = matmul(h, even_w_in, layer=i, n_cols=EVA_W, tn=1024, name="even_in_a")
            pmb = matmul_shift(h, even_w_in, i, EVB_COL, EVB_W, name="even_in_b")
            p_glr = matmul_cols128(h, even_w_in, i, EV_GLR_COL, name="even_in_glr")
            p_dt = matmul_cols128(h, even_w_in, i, EV_DT_COL, name="even_in_dt")
            wg_pad = jnp.zeros((LANES, GLA_QK), F32).at[:GLA_RANK].set(gla_w_gate2[i])
            dtb = _pad_lanes(ssd_dt_bias[i], SSD_DT_LANE0)
            alog = _pad_lanes(ssd_A_log[i], SSD_DT_LANE0)
            d_exp = jnp.repeat(ssd_D[i], SSD_P).reshape(1, SSD_INNER)
            mix = []
            for grp, (row0, nb, t, s_gla, s_ssd, s_conv) in enumerate((
                    (0, bp, tp, None, None, None),
                    (mp, bs, ts, state_gla[i], state_ssd[i], state_ssd_conv[i]))):
                o_gla, gla_new = gla_mixer(pma, p_glr, wg_pad, gla_b_gate[i], gla_norm[i], s_gla,
                                           row0=row0, nb=nb, t=t)
                y_ssd, ssd_new, conv_new = ssd_mixer(pmb, p_dt, ssd_conv_w[i], ssd_conv_b[i].reshape(1, -1), dtb,
                                                     alog, d_exp, ssd_norm[i], expand, s_ssd, s_conv,
                                                     row0=row0, nb=nb, t=t)
                new["gla"][grp].append(gla_new)
                new["ssd"][grp].append(ssd_new)
                new["ssd_conv"][grp].append(conv_new)
                mix += [o_gla.reshape(nb * t, GLA_VW), y_ssd.reshape(nb * t, SSD_INNER)]
            x = matmul_pair_residual(*mix, even_w_out[i], x, mod4, geom, gate_layer=l, gate_idx=5, coef=1.0,
                                     name="even_out")
        else:
            assert odd_w_in.shape[-1] == ODD_IN
            pma = matmul(h, odd_w_in, layer=i, n_cols=ODA_W, tn=1024, name="odd_in_a")
            pmb = matmul_shift(h, odd_w_in, i, ODB_COL, ODB_W, name="odd_in_b")
            p_ba = matmul_cols128(h, odd_w_in, i, OD_BA_COL, name="odd_in_ba")
            rq, rk, qkv = rope_qk(pmb, geom)
            alog = _pad_lanes(gdn_A_log[i], GDN_V_HEADS)
            dtb = _pad_lanes(gdn_dt_bias[i], GDN_V_HEADS)
            mix = []
            for grp, (row0, nb, t, s_gdn, s_conv) in enumerate((
                    (0, bp, tp, None, None), (mp, bs, ts, state_gdn[i], state_gdn_conv[i]))):
                o_gdn, gdn_new, conv_new = gdn_mixer(pma, p_ba, gdn_conv_w[i], alog, dtb, gdn_norm[i],
                                                     s_gdn, s_conv, row0=row0, nb=nb, t=t)
                new["gdn"][grp].append(gdn_new)
                new["gdn_conv"][grp].append(conv_new)
                os_, ls_ = [], []
                for g, (window, dil) in enumerate(DSA_GROUPS):
                    vcol = ODB_DV + g * DSA_GW
                    if grp == 0:
                        o_g, l_g = dsa_prompt(qkv[g], g, geom)
                        keep = min(window, tp)
                        kg = jnp.stack([rk[(b + 1) * tp - keep:(b + 1) * tp, g * DSA_GW:(g + 1) * DSA_GW]
                                        for b in range(bp)])
                        vg = jnp.stack([pmb[(b + 1) * tp - keep:(b + 1) * tp, vcol:vcol + DSA_GW]
                                        for b in range(bp)])
                    else:
                        o_g, l_g = dsa_sample(rq, rk, pmb, caches[g][i], g, geom)
                        keep = ts
                        kg = rk[mp:, g * DSA_GW:(g + 1) * DSA_GW]
                        vg = pmb[mp:, vcol:vcol + DSA_GW]
                    rows = jnp.stack([kg.reshape(nb, keep, DSA_HEADS, DSA_HD),
                                      vg.reshape(nb, keep, DSA_HEADS, DSA_HD)], axis=2)
                    new["dsa%d" % g][grp].append(rows)
                    os_.append(o_g)
                    ls_.append(l_g)
                if grp == 0:
                    o_dsa = dsa_merge(os_, ls_, [dil for _, dil in DSA_GROUPS], mp // ROW_TILE)
                else:
                    o_dsa = dsa_merge(os_, ls_, [0] * len(DSA_GROUPS), 1)
                mix += [o_gdn.reshape(nb * t, GDN_VW), o_dsa]
            x = matmul_pair_residual(*mix, odd_w_out[i], x, mod4, geom, gate_layer=l, gate_idx=5, coef=1.0,
                                     name="odd_out")
        h = rows_norm(x, mod4, ln_ffn2[l], geom, mod_layer=l, shift_idx=6, scale_idx=7)
        act = matmul_swiglu(h, ffn2_w_in, l)
        x = matmul_residual(act, ffn2_w_out, l, x, mod4, geom, gate_layer=l, gate_idx=8, coef=0.5, name="ffn_out")
        if l + 1 < depth:
            h = rows_norm(x, mod4, ln_ffn1[l + 1], geom, mod_layer=l + 1, shift_idx=0, scale_idx=1)
        else:
            y = rows_norm(x, mod4, final_norm, geom, modulate=False)

    y_prompt = y[:mp].reshape(bp, tp, d)
    y_sample = y[mp:].reshape(bs, ts, d)
    outs = [y_prompt, y_sample]
    for name in ("gla", "ssd", "ssd_conv", "gdn", "gdn_conv", "dsa0", "dsa1", "dsa2"):
        outs.append(jnp.stack(new[name][0]))
        outs.append(jnp.stack(new[name][1]))
    return tuple(outs)
```

```python
import functools

import numpy as np
import jax
import jax.numpy as jnp
from jax import lax
from jax.experimental import pallas as pl
from jax.experimental.pallas import tpu as pltpu

F32 = jnp.float32
BF16 = jnp.bfloat16

D_MODEL = 2048
D_FF = 5632
N_MOD = 9
EPS = 1e-6
CONV_W = 4
ROPE_THETA = 10000.0
PAST_LEN = 8192

GLA_HEADS, GLA_DK, GLA_DV, GLA_RANK, GLA_TAU = 4, 128, 256, 16, 16.0
GLA_QK, GLA_VW = GLA_HEADS * GLA_DK, GLA_HEADS * GLA_DV
SSD_HEADS, SSD_P, SSD_STATE, SSD_GROUPS = 32, 64, 128, 4
SSD_HPG = SSD_HEADS // SSD_GROUPS
SSD_INNER = SSD_HEADS * SSD_P
SSD_BC = SSD_GROUPS * SSD_STATE
SSD_CONV_DIM = SSD_INNER + 2 * SSD_BC
GDN_QK_HEADS, GDN_V_HEADS, GDN_DK, GDN_DV = 8, 16, 128, 128
GDN_QK, GDN_VW = GDN_QK_HEADS * GDN_DK, GDN_V_HEADS * GDN_DV
GDN_CONV_DIM = 2 * GDN_QK + GDN_VW
GDN_REP = GDN_V_HEADS // GDN_QK_HEADS
DSA_GROUPS = ((128, 1), (512, 4), (2048, 16))
DSA_HEADS, DSA_HD, DSA_BLOCK = 8, 128, 128
DSA_GW = DSA_HEADS * DSA_HD
DSA_W = len(DSA_GROUPS) * DSA_GW

LANES = 128
SUBLANES = 8
MXU_DIM = 256
VMEM_LIMIT_BYTES = 56 * 1024 * 1024

ROW_TILE = 256
RESIDUAL_TILE = 512
CHUNK = 64
GDN_PACK = MXU_DIM // CHUNK
NEG_BIG = -1e30

EVA_Q, EVA_K, EVA_V, EVA_R = 0, GLA_QK, 2 * GLA_QK, 2 * GLA_QK + GLA_VW
EVA_W = 2 * GLA_QK + 2 * GLA_VW
EV_GLR_COL = EVA_W
EVB_COL = EVA_W + GLA_RANK
EVB_Z, EVB_X, EVB_B, EVB_C = 0, SSD_INNER, 2 * SSD_INNER, 2 * SSD_INNER + SSD_BC
EVB_W = SSD_INNER + SSD_CONV_DIM
EV_DT_COL = EVB_COL + EVB_W
EVEN_IN = EV_DT_COL + SSD_HEADS
SSD_DT_LANE0 = EV_DT_COL % LANES
ODA_Q, ODA_K, ODA_V, ODA_Z = 0, GDN_QK, 2 * GDN_QK, 2 * GDN_QK + GDN_VW
ODA_W = 2 * GDN_QK + 2 * GDN_VW
OD_BA_COL = ODA_W
ODB_COL = ODA_W + 2 * GDN_V_HEADS
ODB_DQ, ODB_DK, ODB_DV = 0, DSA_W, 2 * DSA_W
ODB_W = 3 * DSA_W
ODD_IN = ODB_COL + ODB_W


def _cparams(*sem):
    return pltpu.CompilerParams(dimension_semantics=sem, vmem_limit_bytes=VMEM_LIMIT_BYTES)


def _dot(a, b):
    return jnp.dot(a.astype(BF16), b.astype(BF16), preferred_element_type=F32)


def _dot_nt(a, b):
    return lax.dot_general(a.astype(BF16), b.astype(BF16), (((1,), (1,)), ((), ())), preferred_element_type=F32)


def _dot_tn(a, b):
    return lax.dot_general(a.astype(BF16), b.astype(BF16), (((0,), (0,)), ((), ())), preferred_element_type=F32)


def _split2(x):
    hi = x.astype(BF16)
    lo = (x - hi.astype(F32)).astype(BF16)
    return hi, lo


def _split3(x):
    hi = x.astype(BF16)
    r1 = x - hi.astype(F32)
    mid = r1.astype(BF16)
    lo = (r1 - mid.astype(F32)).astype(BF16)
    return hi, mid, lo


def _dot_exact_rhs(x, e):
    hi, lo = _split2(x)
    eb = e.astype(BF16)
    return (jnp.dot(hi, eb, preferred_element_type=F32) + jnp.dot(lo, eb, preferred_element_type=F32))


def _dot_exact_lhs(e, x):
    eb = e.astype(BF16)
    return sum(jnp.dot(eb, p, preferred_element_type=F32) for p in _split3(x))


def _dot_nt_exact_lhs(e, x):
    eb = e.astype(BF16)
    nt = (((1,), (1,)), ((), ()))
    return sum(lax.dot_general(eb, p, nt, preferred_element_type=F32) for p in _split3(x))


def _cumsum_rows(x):
    n = x.shape[0]
    r = lax.broadcasted_iota(jnp.int32, (n, n), 0)
    c = lax.broadcasted_iota(jnp.int32, (n, n), 1)
    return _dot_exact_lhs(jnp.where(r >= c, 1.0, 0.0), x)


def _rows_of(x, n_rows):
    r = lax.broadcasted_iota(jnp.int32, (n_rows, x.shape[1]), 0)
    c = lax.broadcasted_iota(jnp.int32, (n_rows, x.shape[1]), 1)
    return _dot_nt_exact_lhs(jnp.where(r == c, 1.0, 0.0), x)


def _pad_rows(x, n):
    if x.shape[0] == n:
        return x
    return jnp.concatenate([x, jnp.zeros((n - x.shape[0],) + x.shape[1:], x.dtype)], axis=0)


def _silu(x):
    return x * jax.nn.sigmoid(x)


def _tri_masks(n):
    r = lax.broadcasted_iota(jnp.int32, (n, n), 0)
    c = lax.broadcasted_iota(jnp.int32, (n, n), 1)
    return r >= c, r > c, r == c


def _ada_body(c_ref, w_ref, b_ref, o_ref):
    cond = _silu(c_ref[...])
    o_ref[...] = _dot(cond, w_ref[...]) + b_ref[...]


def ada_mod(c_all, w_ada, b_ada):
    depth, d, n = w_ada.shape
    rows = c_all.shape[0]
    tn = 1024
    return pl.pallas_call(
        _ada_body,
        grid=(depth, n // tn),
        in_specs=[pl.BlockSpec((rows, d), lambda l, j: (0, 0)),
                  pl.BlockSpec((None, d, tn), lambda l, j: (l, 0, j)),
                  pl.BlockSpec((None, 1, tn), lambda l, j: (l, 0, j))],
        out_specs=pl.BlockSpec((None, rows, tn), lambda l, j: (l, 0, j)),
        out_shape=jax.ShapeDtypeStruct((depth, rows, n), F32),
        compiler_params=_cparams("arbitrary", "arbitrary"),
        name="ada_mod",
    )(c_all, w_ada, b_ada.reshape(depth, 1, n))


def _mod_specs(layer, geom, tiles_axis, col_map, width, row_tile=ROW_TILE):
    bp, tp, bs, _ = geom
    tiles_per_seq = tp // row_tile

    def prompt_map(*idx):
        return (layer, bs + jnp.minimum(idx[tiles_axis] // tiles_per_seq, bp - 1), 0, col_map(*idx))

    def sample_map(*idx):
        return (layer, 0, 0, col_map(*idx))

    return [pl.BlockSpec((None, 1, N_MOD, width), prompt_map), pl.BlockSpec((None, bs, N_MOD, width), sample_map)]


def _norm_body(x_ref, mp_ref, ms_ref, gain_ref, o_ref, *os_ref, modulate, shift_idx, scale_idx, n_prompt_tiles, ts):
    i = pl.program_id(0)
    gain = gain_ref[...]

    def compute(x, mod):
        var = jnp.mean(x * x, axis=-1, keepdims=True)
        y = x * lax.rsqrt(var + EPS) * gain
        if modulate:
            y = y * (1.0 + mod(scale_idx)) + mod(shift_idx)
        return y

    @pl.when(i < n_prompt_tiles)
    def _():
        o_ref[...] = compute(x_ref[...], lambda k: mp_ref[0, k:k + 1, :]).astype(o_ref.dtype)

    @pl.when(i >= n_prompt_tiles)
    def _():
        d = x_ref.shape[1]
        y = compute(x_ref[:ROW_TILE, :].reshape(ROW_TILE // ts, ts, d), lambda k: ms_ref[:, k:k + 1, :])
        if os_ref:
            os_ref[0][...] = y.reshape(ROW_TILE, d).astype(os_ref[0].dtype)
        else:
            o_ref[:ROW_TILE, :] = y.reshape(ROW_TILE, d).astype(o_ref.dtype)


def rows_norm(x, mod4, gain, geom, *, mod_layer=0, shift_idx=0, scale_idx=0, modulate=True, split=False):
    bp, tp, bs, ts = geom
    mrows, d = x.shape
    odt = BF16 if modulate else F32
    tm = _pick(tp, (RESIDUAL_TILE, ROW_TILE))
    npt = bp * tp // tm
    row_spec = pl.BlockSpec((tm, d), lambda i: (i, 0))
    body = functools.partial(_norm_body, modulate=modulate, shift_idx=shift_idx, scale_idx=scale_idx,
                             n_prompt_tiles=npt, ts=ts)
    return pl.pallas_call(
        body,
        grid=(npt + 1,),
        in_specs=[row_spec] + _mod_specs(mod_layer, geom, 0, lambda i: 0, d, tm)
        + [pl.BlockSpec((1, d), lambda i: (0, 0))],
        out_specs=([pl.BlockSpec((tm, d), lambda i: (jnp.minimum(i, npt - 1), 0)),
                    pl.BlockSpec((ROW_TILE, d), lambda i: (0, 0))] if split else row_spec),
        out_shape=([jax.ShapeDtypeStruct((bp * tp, d), odt), jax.ShapeDtypeStruct((ROW_TILE, d), odt)]
                   if split else jax.ShapeDtypeStruct((mrows, d), odt)),
        compiler_params=_cparams("arbitrary"),
        name="rows_norm",
    )(x, mod4, mod4, gain.reshape(1, d))


def _mm_body(a_ref, w_ref, o_ref, wb_ref):
    @pl.when(pl.program_id(1) == 0)
    def _():
        wb_ref[...] = w_ref[...].astype(BF16)

    o_ref[...] = jnp.dot(a_ref[...], wb_ref[...], preferred_element_type=F32).astype(o_ref.dtype)


def _mm_swiglu_body(a_ref, wa_ref, wg_ref, o_ref, wb_ref):
    tn = o_ref.shape[1]

    @pl.when(pl.program_id(1) == 0)
    def _():
        wb_ref[:, :tn] = wa_ref[...].astype(BF16)
        wb_ref[:, tn:] = wg_ref[...].astype(BF16)

    u = jnp.dot(a_ref[...], wb_ref[...], preferred_element_type=F32)
    o_ref[...] = (_silu(u[:, :tn]) * u[:, tn:]).astype(o_ref.dtype)


def _pick(n, cands):
    for c in cands:
        if n % c == 0:
            return c
    raise ValueError(f"no tile for {n} in {cands}")


def matmul(a, w, *, layer=None, n_cols=None, tm_cands=(1056, 768, 512, 256), tn=512, out_dtype=F32, name="matmul"):
    mrows, k = a.shape
    n = w.shape[-1] if n_cols is None else n_cols
    assert n % tn == 0
    tm = _pick(mrows, tm_cands)
    if w.ndim == 3:
        w_spec = pl.BlockSpec((None, k, tn), lambda j, i: (layer, 0, j))
    else:
        w_spec = pl.BlockSpec((k, tn), lambda j, i: (0, j))
    return pl.pallas_call(
        _mm_body,
        grid=(n // tn, mrows // tm),
        in_specs=[pl.BlockSpec((tm, k), lambda j, i: (i, 0)), w_spec],
        out_specs=pl.BlockSpec((tm, tn), lambda j, i: (i, j)),
        out_shape=jax.ShapeDtypeStruct((mrows, n), out_dtype),
        scratch_shapes=[pltpu.VMEM((k, tn), BF16)],
        compiler_params=_cparams("arbitrary", "arbitrary"),
        name=name,
    )(a, w)


def matmul_swiglu(a, w_in, layer, *, tn=512, tm_cands=(2112, 1056, 768, 512, 256)):
    mrows, k = a.shape
    f = w_in.shape[-1] // 2
    tm = _pick(mrows, tm_cands)
    nt = f // tn
    return pl.pallas_call(
        _mm_swiglu_body,
        grid=(nt, mrows // tm),
        in_specs=[pl.BlockSpec((tm, k), lambda j, i: (i, 0)),
                  pl.BlockSpec((None, k, tn), lambda j, i: (layer, 0, j)),
                  pl.BlockSpec((None, k, tn), lambda j, i: (layer, 0, j + nt))],
        out_specs=pl.BlockSpec((tm, tn), lambda j, i: (i, j)),
        out_shape=jax.ShapeDtypeStruct((mrows, f), BF16),
        scratch_shapes=[pltpu.VMEM((k, 2 * tn), BF16)],
        compiler_params=_cparams("arbitrary", "arbitrary"),
        name="ffn_in_swiglu",
    )(a, w_in, w_in)


def _mm_shift_body(a_ref, w_ref, wx_ref, o_ref, wb_ref, *, shift):
    @pl.when(pl.program_id(1) == 0)
    def _():
        tn = o_ref.shape[1]
        full = jnp.concatenate([w_ref[...], wx_ref[...]], axis=1)
        wb_ref[...] = full[:, shift:shift + tn].astype(BF16)

    o_ref[...] = jnp.dot(a_ref[...], wb_ref[...], preferred_element_type=F32)


def matmul_shift(a, w, layer, col0, n, *, tn=1024, tm_cands=(1056, 768, 512, 256), name="matmul_shift"):
    mrows, k = a.shape
    tm = _pick(mrows, tm_cands)
    shift = col0 % LANES
    base = col0 - shift
    assert base % tn == 0 and n % tn == 0 and shift > 0
    return pl.pallas_call(
        functools.partial(_mm_shift_body, shift=shift),
        grid=(n // tn, mrows // tm),
        in_specs=[pl.BlockSpec((tm, k), lambda j, i: (i, 0)),
                  pl.BlockSpec((None, k, tn), lambda j, i: (layer, 0, base // tn + j)),
                  pl.BlockSpec((None, k, LANES), lambda j, i: (layer, 0, (base + (j + 1) * tn) // LANES))],
        out_specs=pl.BlockSpec((tm, tn), lambda j, i: (i, j)),
        out_shape=jax.ShapeDtypeStruct((mrows, n), F32),
        scratch_shapes=[pltpu.VMEM((k, tn), BF16)],
        compiler_params=_cparams("arbitrary", "arbitrary"),
        name=name,
    )(a, w, w)


def _mm_cols_body(a_ref, w_ref, o_ref, wb_ref, *, n_valid):
    @pl.when(pl.program_id(0) == 0)
    def _():
        w = w_ref[...]
        if n_valid < LANES:
            lane = lax.broadcasted_iota(jnp.int32, w.shape, 1)
            w = jnp.where(lane < n_valid, w, 0.0)
        wb_ref[...] = w.astype(BF16)

    o_ref[...] = jnp.dot(a_ref[...], wb_ref[...], preferred_element_type=F32)


def matmul_cols128(a, w, layer, col0, *, tm_cands=(1056, 768, 512, 256), name="matmul_cols128"):
    mrows, k = a.shape
    tm = _pick(mrows, tm_cands)
    cb = col0 // LANES
    n_valid = min(LANES, w.shape[-1] - cb * LANES)
    return pl.pallas_call(
        functools.partial(_mm_cols_body, n_valid=n_valid),
        grid=(mrows // tm,),
        in_specs=[pl.BlockSpec((tm, k), lambda i: (i, 0)),
                  pl.BlockSpec((None, k, LANES), lambda i: (layer, 0, cb))],
        out_specs=pl.BlockSpec((tm, LANES), lambda i: (i, 0)),
        out_shape=jax.ShapeDtypeStruct((mrows, LANES), F32),
        scratch_shapes=[pltpu.VMEM((k, LANES), BF16)],
        compiler_params=_cparams("arbitrary"),
        name=name,
    )(a, w)


def _residual_tile(geom):
    return _pick(geom[1], (RESIDUAL_TILE, ROW_TILE))


def _gated_residual(x_ref, acc, gp_ref, gs_ref, o_ref, *, prompt, coef, gate_idx, ts):
    if prompt:
        o_ref[...] = x_ref[...] + coef * gp_ref[0, gate_idx:gate_idx + 1, :] * acc
    else:
        tn = acc.shape[1]
        g = gs_ref[:, gate_idx:gate_idx + 1, :]
        o = (x_ref[:ROW_TILE, :].reshape(ROW_TILE // ts, ts, tn)
             + coef * g * acc[:ROW_TILE].reshape(ROW_TILE // ts, ts, tn))
        o_ref[:ROW_TILE, :] = o.reshape(ROW_TILE, tn)


def _mm_pair_body(p1_ref, p2_ref, s1_ref, s2_ref, w_ref, x_ref, gp_ref, gs_ref, o_ref, wb_ref,
                  *, n_prompt_tiles, coef, gate_idx, ts):
    i = pl.program_id(1)
    k1 = p1_ref.shape[1]

    @pl.when(i == 0)
    def _():
        wb_ref[...] = w_ref[...].astype(BF16)

    def go(r1, r2, prompt):
        acc = (jnp.dot(r1[...], wb_ref[:k1, :], preferred_element_type=F32)
               + jnp.dot(r2[...], wb_ref[k1:, :], preferred_element_type=F32))
        _gated_residual(x_ref, acc, gp_ref, gs_ref, o_ref, prompt=prompt, coef=coef, gate_idx=gate_idx, ts=ts)

    @pl.when(i < n_prompt_tiles)
    def _():
        go(p1_ref, p2_ref, True)

    @pl.when(i >= n_prompt_tiles)
    def _():
        go(s1_ref, s2_ref, False)


def matmul_pair_residual(p1, p2, s1, s2, w, x, mod4, geom, *, gate_layer, gate_idx, coef, tn=1024,
                         name="matmul_pair"):
    mp, k1 = p1.shape
    k2 = p2.shape[1]
    n = w.shape[-1]
    tm = _residual_tile(geom)
    npt = mp // tm
    assert s1.shape[0] == ROW_TILE and x.shape == (mp + ROW_TILE, n)

    pmap = lambda j, i: (jnp.minimum(i, npt - 1), 0)
    smap = lambda j, i: (0, 0)
    tile = pl.BlockSpec((tm, tn), lambda j, i: (i, j))
    body = functools.partial(_mm_pair_body, n_prompt_tiles=npt, coef=coef, gate_idx=gate_idx, ts=geom[3])
    return pl.pallas_call(
        body,
        grid=(n // tn, npt + 1),
        in_specs=[pl.BlockSpec((tm, k1), pmap), pl.BlockSpec((tm, k2), pmap),
                  pl.BlockSpec((ROW_TILE, k1), smap), pl.BlockSpec((ROW_TILE, k2), smap),
                  pl.BlockSpec((k1 + k2, tn), lambda j, i: (0, j)), tile]
        + _mod_specs(gate_layer, geom, 1, lambda j, i: j, tn, tm),
        out_specs=tile,
        out_shape=jax.ShapeDtypeStruct((mp + ROW_TILE, n), F32),
        scratch_shapes=[pltpu.VMEM((k1 + k2, tn), BF16)],
        compiler_params=_cparams("arbitrary", "arbitrary"),
        name=name,
    )(p1, p2, s1, s2, w, x, mod4, mod4)


def _mm_res_body(a_ref, w_ref, x_ref, gp_ref, gs_ref, o_ref, wb_ref, *, n_prompt_tiles, coef, gate_idx, ts):
    i = pl.program_id(1)

    @pl.when(i == 0)
    def _():
        wb_ref[...] = w_ref[...].astype(BF16)

    acc = jnp.dot(a_ref[...], wb_ref[...], preferred_element_type=F32)

    @pl.when(i < n_prompt_tiles)
    def _():
        _gated_residual(x_ref, acc, gp_ref, gs_ref, o_ref, prompt=True, coef=coef, gate_idx=gate_idx, ts=ts)

    @pl.when(i >= n_prompt_tiles)
    def _():
        _gated_residual(x_ref, acc, gp_ref, gs_ref, o_ref, prompt=False, coef=coef, gate_idx=gate_idx, ts=ts)


def matmul_residual(a, w, layer, x, mod4, geom, *, gate_layer, gate_idx, coef, tn=512, name="matmul_residual"):
    bp, tp, _, ts = geom
    mrows, k = a.shape
    n = w.shape[-1]
    tm = _residual_tile(geom)
    npt = bp * tp // tm
    tile = pl.BlockSpec((tm, tn), lambda j, i: (i, j))
    body = functools.partial(_mm_res_body, n_prompt_tiles=npt, coef=coef, gate_idx=gate_idx, ts=ts)
    return pl.pallas_call(
        body,
        grid=(n // tn, npt + 1),
        in_specs=[pl.BlockSpec((tm, k), lambda j, i: (i, 0)),
                  pl.BlockSpec((None, k, tn), lambda j, i: (layer, 0, j)), tile]
        + _mod_specs(gate_layer, geom, 1, lambda j, i: j, tn, tm),
        out_specs=tile,
        out_shape=jax.ShapeDtypeStruct((mrows, n), F32),
        scratch_shapes=[pltpu.VMEM((k, tn), BF16)],
        compiler_params=_cparams("arbitrary", "arbitrary"),
        name=name,
    )(a, w, x, mod4, mod4)


def _conv_chunk(ext_ref, raw, w_ref, cin):
    ext_ref[SUBLANES:SUBLANES + CHUNK, :] = _pad_rows(raw, CHUNK)
    acc = None
    for j in range(CONV_W):
        off = SUBLANES - (CONV_W - 1) + j
        term = ext_ref[off:off + CHUNK, :] * w_ref[j:j + 1, :]
        acc = term if acc is None else acc + term
    return acc


def _conv_init(ext_ref, c0_ref, has_state):
    ext_ref[0:SUBLANES, :] = jnp.zeros((SUBLANES, ext_ref.shape[1]), F32)
    if has_state:
        ext_ref[SUBLANES - (CONV_W - 1):SUBLANES, :] = c0_ref[0]


def _conv_tail(ext_ref, cin):
    return ext_ref[cin + SUBLANES - (CONV_W - 1):cin + SUBLANES, :]


def _conv_advance(ext_ref):
    ext_ref[0:SUBLANES, :] = ext_ref[CHUNK:CHUNK + SUBLANES, :]


def _gla_body(q_ref, k_ref, v_ref, r_ref, g_ref, wg_ref, bg_ref, nrm_ref, s0_ref, o_ref, so_ref, st_ref,
              *, cin, has_state):
    c = pl.program_id(1)
    heads = range(GLA_HEADS)

    @pl.when(c == 0)
    def _():
        for h in heads:
            if has_state:
                st_ref[h] = s0_ref[0, h].T
            else:
                st_ref[h] = jnp.zeros(st_ref.shape[1:], F32)

    q = _pad_rows(q_ref[...], CHUNK) * (GLA_DK ** -0.5)
    k = _pad_rows(k_ref[...], CHUNK)
    v = _pad_rows(v_ref[...], CHUNK)
    glr = _pad_rows(g_ref[...], CHUNK)
    x = _dot(glr, wg_ref[...]) + bg_ref[...]
    logf = jax.nn.log_sigmoid(x) * (1.0 / GLA_TAU)
    if cin < CHUNK:
        rows = lax.broadcasted_iota(jnp.int32, logf.shape, 0)
        logf = jnp.where(rows < cin, logf, 0.0)
    b = _cumsum_rows(logf)
    bmid = b[CHUNK // 2 - 1:CHUNK // 2, :]
    blast = b[CHUNK - 1:CHUNK, :]
    incl, _, _ = _tri_masks(CHUNK)
    qe = q * jnp.exp(b - bmid)
    ke = k * jnp.exp(bmid - b)
    qb = q * jnp.exp(b)
    kw = k * jnp.exp(blast - b)
    sdec = jnp.exp(blast)

    def hk(a, h):
        return a[:, h * GLA_DK:(h + 1) * GLA_DK]

    def hv(a, h):
        return a[:, h * GLA_DV:(h + 1) * GLA_DV]

    sts = [st_ref[h] for h in heads]
    atts = [jnp.where(incl, _dot_nt(hk(qe, h), hk(ke, h)), 0.0) for h in heads]
    os_ = [_dot(atts[h], hv(v, h)) + _dot_nt(hk(qb, h), sts[h]) for h in heads]
    st_news = [sts[h] * hk(sdec, h) + _dot_tn(hv(v, h), hk(kw, h)) for h in heads]
    ys = []
    for h in heads:
        st_ref[h] = st_news[h]
        o = os_[h]
        var = jnp.mean(o * o, axis=-1, keepdims=True)
        ys.append(o * lax.rsqrt(var + EPS) * nrm_ref[...])
    y = jnp.concatenate(ys, axis=1)
    o_ref[0] = (y[:cin] * _silu(r_ref[...])).astype(o_ref.dtype)

    @pl.when(c == pl.num_programs(1) - 1)
    def _():
        for h in heads:
            so_ref[0, h] = st_news[h].T


def gla_mixer(pm, psm, wg_pad, b_gate, nrm, s0, *, row0, nb, t):
    cin = min(t, CHUNK)
    nc = t // cin
    rb0 = row0 // cin
    has_state = s0 is not None
    if s0 is None:
        s0 = jnp.zeros((1, GLA_HEADS, GLA_DK, GLA_DV), F32)
        s0_map = lambda b, c: (0, 0, 0, 0)
    else:
        s0_map = lambda b, c: (b, 0, 0, 0)

    def rmap(col0, width):
        return lambda b, c: (rb0 + b * nc + c, col0 // width)

    const = lambda b, c: (0, 0)
    body = functools.partial(_gla_body, cin=cin, has_state=has_state)
    return pl.pallas_call(
        body,
        grid=(nb, nc),
        in_specs=[pl.BlockSpec((cin, GLA_QK), rmap(EVA_Q, GLA_QK)),
                  pl.BlockSpec((cin, GLA_QK), rmap(EVA_K, GLA_QK)),
                  pl.BlockSpec((cin, GLA_VW), rmap(EVA_V, GLA_VW)),
                  pl.BlockSpec((cin, GLA_VW), rmap(EVA_R, GLA_VW)),
                  pl.BlockSpec((cin, LANES), lambda b, c: (rb0 + b * nc + c, 0)),
                  pl.BlockSpec((LANES, GLA_QK), const),
                  pl.BlockSpec((1, GLA_QK), const),
                  pl.BlockSpec((1, GLA_DV), const),
                  pl.BlockSpec((1, GLA_HEADS, GLA_DK, GLA_DV), s0_map)],
        out_specs=[pl.BlockSpec((1, cin, GLA_VW), lambda b, c: (b, c, 0)),
                   pl.BlockSpec((1, GLA_HEADS, GLA_DK, GLA_DV), lambda b, c: (b, 0, 0, 0))],
        out_shape=[jax.ShapeDtypeStruct((nb, t, GLA_VW), BF16),
                   jax.ShapeDtypeStruct((nb, GLA_HEADS, GLA_DK, GLA_DV), F32)],
        scratch_shapes=[pltpu.VMEM((GLA_HEADS, GLA_DV, GLA_DK), F32)],
        compiler_params=_cparams("arbitrary", "arbitrary"),
        name="gla_mixer",
    )(pm, pm, pm, pm, psm, wg_pad, b_gate.reshape(1, GLA_QK), nrm.reshape(1, GLA_DV), s0)


def _ssd_body(z_ref, x_ref, b_ref, c_ref, sm_ref, cwx_ref, cwb_ref, cwc_ref, cbx_ref, cbb_ref, cbc_ref,
              dtb_ref, alog_ref, dexp_ref, nrm_ref, e_ref, s0_ref, c0x_ref, c0b_ref, c0c_ref,
              y_ref, so_ref, cox_ref, cob_ref, coc_ref,
              st_ref, ex_ref, eb_ref, ec_ref, *, cin, has_state):
    c = pl.program_id(1)

    @pl.when(c == 0)
    def _():
        if has_state:
            st_ref[...] = s0_ref[0].T
        else:
            st_ref[...] = jnp.zeros(st_ref.shape, F32)
        _conv_init(ex_ref, c0x_ref, has_state)
        _conv_init(eb_ref, c0b_ref, has_state)
        _conv_init(ec_ref, c0c_ref, has_state)

    xs = _silu(_conv_chunk(ex_ref, x_ref[...], cwx_ref, cin) + cbx_ref[...])
    bm = _silu(_conv_chunk(eb_ref, b_ref[...], cwb_ref, cin) + cbb_ref[...])
    cm = _silu(_conv_chunk(ec_ref, c_ref[...], cwc_ref, cin) + cbc_ref[...])

    dt = jax.nn.softplus(_pad_rows(sm_ref[...], CHUNK) + dtb_ref[...])
    if cin < CHUNK:
        rows = lax.broadcasted_iota(jnp.int32, dt.shape, 0)
        dt = jnp.where(rows < cin, dt, 0.0)
    a = dt * (-jnp.exp(alog_ref[...]))
    cum = _cumsum_rows(a)
    cum_t = _rows_of(cum, SSD_DT_LANE0 + SSD_HEADS)
    last = cum[CHUNK - 1:CHUNK, :]
    stacked = jnp.concatenate([dt, jnp.exp(cum), jnp.exp(last - cum)], axis=0)
    ex = _dot_exact_rhs(stacked, e_ref[...])
    dt_e, ecum_e, w_e = ex[:CHUNK], ex[CHUNK:2 * CHUNK], ex[2 * CHUNK:]
    xdt = xs * dt_e
    xw = xdt * w_e
    incl, _, _ = _tri_masks(CHUNK)
    lane = lax.broadcasted_iota(jnp.int32, (CHUNK, LANES), 1)
    st = st_ref[...]
    gw = SSD_HPG * SSD_P
    groups = range(SSD_GROUPS)
    pairs = SSD_HPG // 2

    def grp(a, g, w):
        return a[:, g * w:(g + 1) * w]

    cbs = [_dot_nt(grp(cm, g, SSD_STATE), grp(bm, g, SSD_STATE)) for g in groups]
    inters = [_dot(grp(cm, g, SSD_STATE), grp(st, g, gw)) for g in groups]
    st_adds = [_dot_tn(grp(bm, g, SSD_STATE), grp(xw, g, gw)) for g in groups]
    att_pairs = []
    for g in groups:
        for pr in range(pairs):
            atts = []
            for h in (g * SSD_HPG + 2 * pr, g * SSD_HPG + 2 * pr + 1):
                ln = SSD_DT_LANE0 + h
                seg = cum[:, ln:ln + 1] - cum_t[ln:ln + 1, :]
                atts.append(cbs[g] * jnp.exp(jnp.where(incl, seg, NEG_BIG)))
            att_pairs.append(jnp.concatenate(atts, axis=0))
    yys = [_dot(att_pairs[p], xdt[:, p * LANES:(p + 1) * LANES]) for p in range(SSD_GROUPS * pairs)]
    ys = [jnp.where(lane < SSD_P, yy[:CHUNK], yy[CHUNK:]) for yy in yys]
    for g in groups:
        st_ref[:, g * gw:(g + 1) * gw] = grp(st, g, gw) * grp(ecum_e, g, gw)[CHUNK - 1:CHUNK] + st_adds[g]
    y = (jnp.concatenate(ys, axis=1) + jnp.concatenate(inters, axis=1) * ecum_e + xs * dexp_ref[...])
    y = y[:cin] * _silu(z_ref[...])
    var = jnp.mean(y * y, axis=-1, keepdims=True)
    y_ref[0] = (y * lax.rsqrt(var + EPS) * nrm_ref[...]).astype(y_ref.dtype)

    @pl.when(c == pl.num_programs(1) - 1)
    def _():
        so_ref[0] = st_ref[...].T
        cox_ref[0] = _conv_tail(ex_ref, cin)
        cob_ref[0] = _conv_tail(eb_ref, cin)
        coc_ref[0] = _conv_tail(ec_ref, cin)

    if cin == CHUNK:
        _conv_advance(ex_ref)
        _conv_advance(eb_ref)
        _conv_advance(ec_ref)


def ssd_mixer(pm, psm, conv_w, conv_b, dtb_pad, alog_pad, d_exp, nrm, expand, s0, c0, *, row0, nb, t):
    cin = min(t, CHUNK)
    nc = t // cin
    rb0 = row0 // cin
    has_state = s0 is not None
    hp = SSD_INNER
    if s0 is None:
        s0 = jnp.zeros((1, hp, SSD_STATE), F32)
        c0 = jnp.zeros((1, CONV_W - 1, SSD_CONV_DIM), F32)
        bmap = lambda b, c: 0
    else:
        s0 = s0.reshape(nb, hp, SSD_STATE)
        bmap = lambda b, c: b

    def rmap(col0, width):
        return lambda b, c: (rb0 + b * nc + c, col0 // width)

    const = lambda b, c: (0, 0)
    kconv = CONV_W - 1
    body = functools.partial(_ssd_body, cin=cin, has_state=has_state)
    outs = pl.pallas_call(
        body,
        grid=(nb, nc),
        in_specs=[pl.BlockSpec((cin, hp), rmap(EVB_Z, hp)),
                  pl.BlockSpec((cin, hp), rmap(EVB_X, hp)),
                  pl.BlockSpec((cin, SSD_BC), rmap(EVB_B, SSD_BC)),
                  pl.BlockSpec((cin, SSD_BC), rmap(EVB_C, SSD_BC)),
                  pl.BlockSpec((cin, LANES), lambda b, c: (rb0 + b * nc + c, 0)),
                  pl.BlockSpec((CONV_W, hp), lambda b, c: (0, 0)),
                  pl.BlockSpec((CONV_W, SSD_BC), lambda b, c: (0, hp // SSD_BC)),
                  pl.BlockSpec((CONV_W, SSD_BC), lambda b, c: (0, hp // SSD_BC + 1)),
                  pl.BlockSpec((1, hp), lambda b, c: (0, 0)),
                  pl.BlockSpec((1, SSD_BC), lambda b, c: (0, hp // SSD_BC)),
                  pl.BlockSpec((1, SSD_BC), lambda b, c: (0, hp // SSD_BC + 1)),
                  pl.BlockSpec((1, LANES), const),
                  pl.BlockSpec((1, LANES), const),
                  pl.BlockSpec((1, hp), const),
                  pl.BlockSpec((1, hp), const),
                  pl.BlockSpec((LANES, hp), const),
                  pl.BlockSpec((1, hp, SSD_STATE), lambda b, c: (bmap(b, c), 0, 0)),
                  pl.BlockSpec((1, kconv, hp), lambda b, c: (bmap(b, c), 0, 0)),
                  pl.BlockSpec((1, kconv, SSD_BC), lambda b, c: (bmap(b, c), 0, hp // SSD_BC)),
                  pl.BlockSpec((1, kconv, SSD_BC), lambda b, c: (bmap(b, c), 0, hp // SSD_BC + 1))],
        out_specs=[pl.BlockSpec((1, cin, hp), lambda b, c: (b, c, 0)),
                   pl.BlockSpec((1, hp, SSD_STATE), lambda b, c: (b, 0, 0)),
                   pl.BlockSpec((1, kconv, hp), lambda b, c: (b, 0, 0)),
                   pl.BlockSpec((1, kconv, SSD_BC), lambda b, c: (b, 0, 0)),
                   pl.BlockSpec((1, kconv, SSD_BC), lambda b, c: (b, 0, 0))],
        out_shape=[jax.ShapeDtypeStruct((nb, t, hp), BF16),
                   jax.ShapeDtypeStruct((nb, hp, SSD_STATE), F32),
                   jax.ShapeDtypeStruct((nb, kconv, hp), F32),
                   jax.ShapeDtypeStruct((nb, kconv, SSD_BC), F32),
                   jax.ShapeDtypeStruct((nb, kconv, SSD_BC), F32)],
        scratch_shapes=[pltpu.VMEM((SSD_STATE, hp), F32),
                        pltpu.VMEM((CHUNK + SUBLANES, hp), F32),
                        pltpu.VMEM((CHUNK + SUBLANES, SSD_BC), F32),
                        pltpu.VMEM((CHUNK + SUBLANES, SSD_BC), F32)],
        compiler_params=_cparams("arbitrary", "arbitrary"),
        name="ssd_mixer",
    )(pm, pm, pm, pm, psm, conv_w, conv_w, conv_w, conv_b, conv_b, conv_b, dtb_pad, alog_pad, d_exp,
      nrm.reshape(1, hp), expand, s0, c0, c0, c0)
    y, s_new, cx, cb_, cc = outs
    conv_s = jnp.concatenate([cx, cb_, cc], axis=-1)
    return y, s_new.reshape(nb, SSD_HEADS, SSD_P, SSD_STATE), conv_s


def _l2norm_heads(x, n_heads, width, scale):
    outs = []
    for e in range(n_heads):
        xe = x[:, e * width:(e + 1) * width]
        ss = jnp.sum(xe * xe, axis=-1, keepdims=True)
        outs.append(xe * (lax.rsqrt(ss + EPS) * scale))
    return outs


def _unit_lower_inverse(a_mats, eye, order):
    def mm(a, b):
        return jnp.dot(a, b, preferred_element_type=F32)

    ts = [eye - a for a in a_mats]
    pbs = [(-a).astype(BF16) for a in a_mats]
    for _ in range(max(order - 1, 1).bit_length() - 1):
        ps = [mm(pb, pb) for pb in pbs]
        pbs = [p.astype(BF16) for p in ps]
        ts = [t + mm(t.astype(BF16), pb) for t, pb in zip(ts, pbs)]
    outs = []
    splits = [(_split2(a), _split2(t)) for a, t in zip(a_mats, ts)]
    ats = [mm(ah, th) + mm(ah, tl) + mm(al, th) for (ah, al), (th, tl) in splits]
    for t, at, (_, (th, _)) in zip(ts, ats, splits):
        resid = eye - t - at
        outs.append(t + mm(th, resid.astype(BF16)))
    return outs


def _gdn_body(q_ref, k_ref, v_ref, z_ref, sm_ref, cwq_ref, cwk_ref, cwv_ref, alog_ref, dtb_ref, nrm_ref,
              s0_ref, c0q_ref, c0k_ref, c0v_ref,
              o_ref, so_ref, coq_ref, cok_ref, cov_ref,
              s_ref, eq_ref, ek_ref, ev_ref, *, cin, has_state):
    c = pl.program_id(1)
    n = GDN_PACK * CHUNK

    @pl.when(c == 0)
    def _():
        if has_state:
            s_ref[...] = s0_ref[0]
        else:
            s_ref[...] = jnp.zeros(s_ref.shape, F32)
        _conv_init(eq_ref, c0q_ref, has_state)
        _conv_init(ek_ref, c0k_ref, has_state)
        _conv_init(ev_ref, c0v_ref, has_state)

    qc = _silu(_conv_chunk(eq_ref, q_ref[...], cwq_ref, cin))
    kc = _silu(_conv_chunk(ek_ref, k_ref[...], cwk_ref, cin))
    vc = _silu(_conv_chunk(ev_ref, v_ref[...], cwv_ref, cin))
    qs = _l2norm_heads(qc, GDN_QK_HEADS, GDN_DK, GDN_DK ** -0.5)
    ks = _l2norm_heads(kc, GDN_QK_HEADS, GDN_DK, 1.0)

    sm = _pad_rows(sm_ref[...], CHUNK)
    beta = jax.nn.sigmoid(sm)
    gl = -jnp.exp(alog_ref[...]) * jax.nn.softplus(sm + dtb_ref[...])
    if cin < CHUNK:
        rows = lax.broadcasted_iota(jnp.int32, sm.shape, 0)
        beta = jnp.where(rows < cin, beta, 0.0)
        gl = jnp.where(rows < cin, gl, 0.0)
    cum = _cumsum_rows(gl)
    last_row = cum[CHUNK - 1:CHUNK, :]

    rr = lax.broadcasted_iota(jnp.int32, (n, n), 0)
    cc = lax.broadcasted_iota(jnp.int32, (n, n), 1)
    same = (rr // CHUNK) == (cc // CHUNK)
    incl = same & (rr >= cc)
    strict = same & (rr > cc)
    eye = jnp.where(rr == cc, 1.0, 0.0)
    head_of_row = lax.broadcasted_iota(jnp.int32, (n, LANES), 0) // CHUNK
    lane = lax.broadcasted_iota(jnp.int32, (n, LANES), 1)
    bd_mask = (lax.broadcasted_iota(jnp.int32, (n, GDN_PACK * GDN_DK), 0) // CHUNK
               == lax.broadcasted_iota(jnp.int32, (n, GDN_PACK * GDN_DK), 1) // GDN_DK)
    srow = lax.broadcasted_iota(jnp.int32, (GDN_PACK * GDN_DK, LANES), 0) // GDN_DK
    slane = lax.broadcasted_iota(jnp.int32, (GDN_PACK * GDN_DK, LANES), 1)

    def block_diag(x):
        return jnp.where(bd_mask, jnp.concatenate([x] * GDN_PACK, axis=1), 0.0)

    def stack(parts):
        return jnp.concatenate(parts, axis=0)

    cum_rep = stack([cum] * GDN_PACK)
    beta_rep = stack([beta] * GDN_PACK)
    last_rep = jnp.broadcast_to(last_row, (n, LANES))
    cum_t = _rows_of(cum, 2 * GDN_V_HEADS)
    packs = range(GDN_V_HEADS // GDN_PACK)

    pre = []
    for j in packs:
        h0 = j * GDN_PACK
        heads = range(h0, h0 + GDN_PACK)
        k_st = stack([ks[i // GDN_REP] for i in heads])
        q_st = stack([qs[i // GDN_REP] for i in heads])
        v_st = stack([vc[:, i * GDN_DV:(i + 1) * GDN_DV] for i in heads])
        g_sel = lane == GDN_V_HEADS + h0 + head_of_row
        col = jnp.sum(jnp.where(g_sel, cum_rep, 0.0), axis=1, keepdims=True)
        rowv = jnp.concatenate([cum_t[GDN_V_HEADS + i:GDN_V_HEADS + i + 1, :] for i in heads], axis=1)
        bcol = jnp.sum(jnp.where(lane == h0 + head_of_row, beta_rep, 0.0), axis=1, keepdims=True)
        lastc = jnp.sum(jnp.where(g_sel, last_rep, 0.0), axis=1, keepdims=True)
        dec = jnp.exp(jnp.where(incl, col - rowv, NEG_BIG))
        ecol = jnp.exp(col)
        s_st = s_ref[h0:h0 + GDN_PACK].reshape(GDN_PACK * GDN_DK, GDN_DV)
        pre.append(dict(k=k_st, q=q_st, v=v_st, col=col, bcol=bcol, lastc=lastc, dec=dec, ecol=ecol, s=s_st))
    kks = [_dot_nt(p["k"], p["k"]) for p in pre]
    qks = [_dot_nt(p["q"], p["k"]) for p in pre]
    a_mats = [jnp.where(strict, p["bcol"] * kk * p["dec"], 0.0) for p, kk in zip(pre, kks)]
    t_invs = _unit_lower_inverse(a_mats, eye, cin)
    rhss = [jnp.concatenate([p["v"] * p["bcol"], p["k"] * (p["bcol"] * p["ecol"])], axis=1) for p in pre]
    sols = [_dot(t, r) for t, r in zip(t_invs, rhss)]
    deltas = [sol[:, :GDN_DV] - _dot(block_diag(sol[:, GDN_DV:]), p["s"]) for sol, p in zip(sols, pre)]
    o_sts = [_dot(block_diag(p["q"] * p["ecol"]), p["s"]) + _dot(jnp.where(incl, qk * p["dec"], 0.0), dl)
             for p, qk, dl in zip(pre, qks, deltas)]
    outs = []
    for j, (p, dl, o_st) in enumerate(zip(pre, deltas, o_sts)):
        h0 = j * GDN_PACK
        kw_bd = block_diag(p["k"] * jnp.exp(p["lastc"] - p["col"]))
        s_dec = jnp.exp(jnp.sum(jnp.where(slane == GDN_V_HEADS + h0 + srow,
                                          jnp.broadcast_to(last_row, srow.shape), 0.0), axis=1, keepdims=True))
        s_new = p["s"] * s_dec + _dot_tn(kw_bd, dl)
        s_ref[h0:h0 + GDN_PACK] = s_new.reshape(GDN_PACK, GDN_DK, GDN_DV)
        var = jnp.mean(o_st * o_st, axis=-1, keepdims=True)
        y_st = o_st * lax.rsqrt(var + EPS) * nrm_ref[...]
        outs += [y_st[i * CHUNK:(i + 1) * CHUNK] for i in range(GDN_PACK)]
    o = jnp.concatenate(outs, axis=1)
    o_ref[0] = (o[:cin] * _silu(z_ref[...])).astype(o_ref.dtype)

    @pl.when(c == pl.num_programs(1) - 1)
    def _():
        so_ref[0] = s_ref[...]
        coq_ref[0] = _conv_tail(eq_ref, cin)
        cok_ref[0] = _conv_tail(ek_ref, cin)
        cov_ref[0] = _conv_tail(ev_ref, cin)

    if cin == CHUNK:
        _conv_advance(eq_ref)
        _conv_advance(ek_ref)
        _conv_advance(ev_ref)


def gdn_mixer(pm, psm, conv_w, alog_pad, dtb_pad, nrm, s0, c0, *, row0, nb, t):
    cin = min(t, CHUNK)
    nc = t // cin
    rb0 = row0 // cin
    has_state = s0 is not None
    kconv = CONV_W - 1
    if s0 is None:
        s0 = jnp.zeros((1, GDN_V_HEADS, GDN_DK, GDN_DV), F32)
        c0 = jnp.zeros((1, kconv, GDN_CONV_DIM), F32)
        bmap = lambda b: 0
    else:
        bmap = lambda b: b

    def rmap(col0, width):
        return lambda b, c: (rb0 + b * nc + c, col0 // width)

    const = lambda b, c: (0, 0)
    body = functools.partial(_gdn_body, cin=cin, has_state=has_state)
    outs = pl.pallas_call(
        body,
        grid=(nb, nc),
        in_specs=[pl.BlockSpec((cin, GDN_QK), rmap(ODA_Q, GDN_QK)),
                  pl.BlockSpec((cin, GDN_QK), rmap(ODA_K, GDN_QK)),
                  pl.BlockSpec((cin, GDN_VW), rmap(ODA_V, GDN_VW)),
                  pl.BlockSpec((cin, GDN_VW), rmap(ODA_Z, GDN_VW)),
                  pl.BlockSpec((cin, LANES), lambda b, c: (rb0 + b * nc + c, 0)),
                  pl.BlockSpec((CONV_W, GDN_QK), lambda b, c: (0, 0)),
                  pl.BlockSpec((CONV_W, GDN_QK), lambda b, c: (0, 1)),
                  pl.BlockSpec((CONV_W, GDN_VW), lambda b, c: (0, 1)),
                  pl.BlockSpec((1, LANES), const),
                  pl.BlockSpec((1, LANES), const),
                  pl.BlockSpec((1, GDN_DV), const),
                  pl.BlockSpec((1, GDN_V_HEADS, GDN_DK, GDN_DV), lambda b, c: (bmap(b), 0, 0, 0)),
                  pl.BlockSpec((1, kconv, GDN_QK), lambda b, c: (bmap(b), 0, 0)),
                  pl.BlockSpec((1, kconv, GDN_QK), lambda b, c: (bmap(b), 0, 1)),
                  pl.BlockSpec((1, kconv, GDN_VW), lambda b, c: (bmap(b), 0, 1))],
        out_specs=[pl.BlockSpec((1, cin, GDN_VW), lambda b, c: (b, c, 0)),
                   pl.BlockSpec((1, GDN_V_HEADS, GDN_DK, GDN_DV), lambda b, c: (b, 0, 0, 0)),
                   pl.BlockSpec((1, kconv, GDN_QK), lambda b, c: (b, 0, 0)),
                   pl.BlockSpec((1, kconv, GDN_QK), lambda b, c: (b, 0, 0)),
                   pl.BlockSpec((1, kconv, GDN_VW), lambda b, c: (b, 0, 0))],
        out_shape=[jax.ShapeDtypeStruct((nb, t, GDN_VW), BF16),
                   jax.ShapeDtypeStruct((nb, GDN_V_HEADS, GDN_DK, GDN_DV), F32),
                   jax.ShapeDtypeStruct((nb, kconv, GDN_QK), F32),
                   jax.ShapeDtypeStruct((nb, kconv, GDN_QK), F32),
                   jax.ShapeDtypeStruct((nb, kconv, GDN_VW), F32)],
        scratch_shapes=[pltpu.VMEM((GDN_V_HEADS, GDN_DK, GDN_DV), F32),
                        pltpu.VMEM((CHUNK + SUBLANES, GDN_QK), F32),
                        pltpu.VMEM((CHUNK + SUBLANES, GDN_QK), F32),
                        pltpu.VMEM((CHUNK + SUBLANES, GDN_VW), F32)],
        compiler_params=_cparams("arbitrary", "arbitrary"),
        name="gdn_mixer",
    )(pm, pm, pm, pm, psm, conv_w, conv_w, conv_w, alog_pad, dtb_pad, nrm.reshape(1, GDN_DV),
      s0, c0, c0, c0)
    o, s_new, cq, ck, cv = outs
    return o, s_new, jnp.concatenate([cq, ck, cv], axis=-1)


def _residue_perm(dil, inverse=False):
    n = ROW_TILE // dil
    i = lax.broadcasted_iota(jnp.int32, (ROW_TILE, ROW_TILE), 0)
    j = lax.broadcasted_iota(jnp.int32, (ROW_TILE, ROW_TILE), 1)
    if inverse:
        i, j = j, i
    return jnp.where(j == (i % n) * dil + i // n, 1.0, 0.0).astype(BF16)


def _rope_body(q_ref, k_ref, v_ref, inv_ref, qo_ref, ko_ref, *rest, n_prompt_tiles, tiles_per_seq, ts):
    i = pl.program_id(0)
    rt = q_ref.shape[0]
    r = lax.broadcasted_iota(jnp.int32, (rt, LANES), 0)
    pos_prompt = (i % tiles_per_seq) * rt + r
    pos_sample = PAST_LEN + r % ts
    pos = jnp.where(i < n_prompt_tiles, pos_prompt, pos_sample).astype(F32)
    ang = pos * inv_ref[...]
    cos = jnp.cos(ang)
    sin = jnp.sin(ang)
    lane = lax.broadcasted_iota(jnp.int32, (rt, LANES), 1)
    sin = jnp.where(lane < LANES // 2, -sin, sin)
    for src, dst in ((q_ref, qo_ref), (k_ref, ko_ref)):
        for h in range(DSA_W // DSA_HD):
            x = src[:, h * DSA_HD:(h + 1) * DSA_HD]
            dst[:, h * DSA_HD:(h + 1) * DSA_HD] = x * cos + pltpu.roll(x, DSA_HD // 2, 1) * sin
    for g, (_, dil) in enumerate(DSA_GROUPS):
        perm = None if dil == 1 else _residue_perm(dil)
        for src, dst in zip((qo_ref, ko_ref, v_ref), rest[3 * g:3 * g + 3]):
            xb = src[:, g * DSA_GW:(g + 1) * DSA_GW].astype(BF16)
            if perm is not None:
                xb = jnp.dot(perm, xb, preferred_element_type=F32).astype(BF16)
            dst[...] = xb.reshape(dst.shape)


def rope_qk(pm, geom):
    bp, tp, bs, ts = geom
    mrows = pm.shape[0]
    half = DSA_HD // 2
    inv = ROPE_THETA ** (-jnp.arange(half, dtype=F32) / half)
    inv2 = jnp.concatenate([inv, inv]).reshape(1, DSA_HD)
    body = functools.partial(_rope_body, n_prompt_tiles=bp * tp // ROW_TILE, tiles_per_seq=tp // ROW_TILE, ts=ts)
    spec_o = pl.BlockSpec((ROW_TILE, DSA_W), lambda i: (i, 0))
    out_specs = [spec_o, spec_o]
    out_shape = [jax.ShapeDtypeStruct((mrows, DSA_W), F32)] * 2
    for _, dil in DSA_GROUPS:
        n = ROW_TILE // dil
        out_specs += [pl.BlockSpec((dil, n, DSA_GW), lambda i: (0, i, 0))] * 3
        out_shape += [jax.ShapeDtypeStruct((dil, mrows // dil, DSA_GW), BF16)] * 3
    outs = pl.pallas_call(
        body,
        grid=(mrows // ROW_TILE,),
        in_specs=[pl.BlockSpec((ROW_TILE, DSA_W), lambda i: (i, ODB_DQ // DSA_W)),
                  pl.BlockSpec((ROW_TILE, DSA_W), lambda i: (i, ODB_DK // DSA_W)),
                  pl.BlockSpec((ROW_TILE, DSA_W), lambda i: (i, ODB_DV // DSA_W)),
                  pl.BlockSpec((1, DSA_HD), lambda i: (0, 0))],
        out_specs=out_specs,
        out_shape=out_shape,
        compiler_params=_cparams("arbitrary"),
        name="rope_qk",
    )(pm, pm, pm, inv2)
    return outs[0], outs[1], [tuple(outs[2 + 3 * g:5 + 3 * g]) for g in range(len(DSA_GROUPS))]


def _lse_lanes(lses):
    rows = lses[0].shape[0]
    lane = lax.broadcasted_iota(jnp.int32, (rows, LANES), 1)
    out = jnp.zeros((rows, LANES), F32)
    for h, v in enumerate(lses):
        out = jnp.where(lane == h, v, out)
    return out


def _dsa_prompt_body(q_ref, kp_ref, kc_ref, vp_ref, vc_ref, o_ref, l_ref, *, jmax):
    n = pl.program_id(2)
    blk = DSA_BLOCK
    i_idx = lax.broadcasted_iota(jnp.int32, (blk, 2 * blk), 0)
    m_idx = lax.broadcasted_iota(jnp.int32, (blk, 2 * blk), 1)
    dist = i_idx + blk - m_idx
    valid = (dist >= 0) & (dist <= jmax) & ((m_idx >= blk) | (n > 0))
    scale = DSA_HD ** -0.5
    lses = []
    for h in range(DSA_HEADS):
        sl = slice(h * DSA_HD, (h + 1) * DSA_HD)
        kk = jnp.concatenate([kp_ref[:, sl], kc_ref[:, sl]], axis=0)
        vv = jnp.concatenate([vp_ref[:, sl], vc_ref[:, sl]], axis=0)
        s = _dot_nt(q_ref[:, sl], kk) * scale
        s = jnp.where(valid, s, NEG_BIG)
        mx = jnp.max(s, axis=-1, keepdims=True)
        p = jnp.exp(s - mx)
        den = jnp.sum(p, axis=-1, keepdims=True)
        o_ref[:, sl] = (_dot(p, vv) / den).astype(o_ref.dtype)
        lses.append(mx + jnp.log(den))
    l_ref[...] = _lse_lanes(lses)


def dsa_prompt(qkv, g, geom):
    bp, tp, _, _ = geom
    window, dil = DSA_GROUPS[g]
    ls = tp // dil
    nblk = ls // DSA_BLOCK
    q, k, v = qkv
    mu = bp * ls

    cur = lambda b, r, n: (r, b * nblk + n, 0)
    prev = lambda b, r, n: (r, b * nblk + jnp.maximum(n - 1, 0), 0)
    blk = (None, DSA_BLOCK, DSA_GW)
    return pl.pallas_call(
        functools.partial(_dsa_prompt_body, jmax=window // dil),
        grid=(bp, dil, nblk),
        in_specs=[pl.BlockSpec(blk, cur), pl.BlockSpec(blk, prev), pl.BlockSpec(blk, cur),
                  pl.BlockSpec(blk, prev), pl.BlockSpec(blk, cur)],
        out_specs=[pl.BlockSpec(blk, cur), pl.BlockSpec((None, DSA_BLOCK, LANES), cur)],
        out_shape=[jax.ShapeDtypeStruct((dil, mu, DSA_GW), BF16),
                   jax.ShapeDtypeStruct((dil, mu, LANES), F32)],
        compiler_params=_cparams("arbitrary", "arbitrary", "arbitrary"),
        name=f"dsa_prompt_w{window}",
    )(q, k, k, v, v)


def _dsa_sample_body(q_ref, kn_ref, vn_ref, cache_ref, o_ref, l_ref, m_scr, d_scr, acc_scr, *, window, dil):
    j = pl.program_id(1)
    nj = pl.num_programs(1)
    ts = q_ref.shape[0]
    nh = DSA_HEADS
    scale = DSA_HD ** -0.5

    def heads_to_rows(ref):
        return jnp.concatenate([ref[:, h * DSA_HD:(h + 1) * DSA_HD] for h in range(nh)], axis=0)

    @pl.when(j == 0)
    def _():
        m_scr[...] = jnp.full(m_scr.shape, NEG_BIG, F32)
        d_scr[...] = jnp.zeros(d_scr.shape, F32)
        acc_scr[...] = jnp.zeros(acc_scr.shape, F32)

    def update(s, vals, ok):
        s = jnp.where(ok, s, NEG_BIG)
        m_old = m_scr[...]
        m_new = jnp.maximum(m_old, jnp.max(s, axis=-1, keepdims=True))
        alpha = jnp.exp(m_old - m_new)
        p = jnp.where(ok, jnp.exp(s - m_new[:, 0:1]), 0.0)
        d_scr[...] = alpha * d_scr[...] + jnp.sum(p, axis=-1, keepdims=True)
        acc_scr[...] = alpha * acc_scr[...] + _dot(p, vals)
        m_scr[...] = m_new

    q_rows = heads_to_rows(q_ref) * scale
    ub, n_res = cache_ref.shape[1], cache_ref.shape[2] // (2 * nh)
    lb = ub * n_res
    x4 = cache_ref[0].reshape(lb, 2, nh, DSA_HD)
    k_all = x4[:, 0].reshape(lb * nh, DSA_HD)
    v_all = x4[:, 1].reshape(lb * nh, DSA_HD)
    r = lax.broadcasted_iota(jnp.int32, (nh * ts, lb * nh), 0)
    c = lax.broadcasted_iota(jnp.int32, (nh * ts, lb * nh), 1)
    pos = c // nh
    dist = window + r % ts - ((j * ub + pos // n_res) * dil + pos % n_res)
    ok = (r // ts == c % nh) & ((dist & (dil - 1)) == 0) & (dist <= window)
    update(_dot_nt(q_rows, k_all), v_all, ok)

    @pl.when(j == nj - 1)
    def _():
        r2 = lax.broadcasted_iota(jnp.int32, (nh * ts, nh * ts), 0)
        c2 = lax.broadcasted_iota(jnp.int32, (nh * ts, nh * ts), 1)
        d2 = r2 % ts - c2 % ts
        ok2 = (r2 // ts == c2 // ts) & (d2 >= 0) & ((d2 & (dil - 1)) == 0)
        update(_dot_nt(q_rows, heads_to_rows(kn_ref)), heads_to_rows(vn_ref), ok2)
        o_rows = acc_scr[...] / d_scr[...]
        lse = m_scr[...] + jnp.log(d_scr[...])
        for h in range(nh):
            o_ref[:, h * DSA_HD:(h + 1) * DSA_HD] = o_rows[h * ts:(h + 1) * ts]
        l_ref[...] = _lse_lanes([lse[h * ts:(h + 1) * ts, 0:1] for h in range(nh)])


def dsa_sample(rq, rk, pm, cache, g, geom):
    bp, tp, bs, ts = geom
    window, dil = DSA_GROUPS[g]
    lcache = cache.shape[1]
    assert lcache == window and window % dil == 0, "cache must hold exactly the window"
    n_res = min(dil, ts)
    strides = lcache // dil
    ub = min(strides, 512 // n_res)
    rb0 = bp * tp // ts
    kv_rows = 2 * DSA_HEADS
    cache4 = cache.reshape(bs, strides, dil * kv_rows, DSA_HD)
    blk = (ts, DSA_GW)
    body = functools.partial(_dsa_sample_body, window=window, dil=dil)
    return pl.pallas_call(
        body,
        grid=(bs, strides // ub),
        in_specs=[pl.BlockSpec(blk, lambda b, j: (rb0 + b, g)),
                  pl.BlockSpec(blk, lambda b, j: (rb0 + b, g)),
                  pl.BlockSpec(blk, lambda b, j: (rb0 + b, ODB_DV // DSA_GW + g)),
                  pl.BlockSpec((1, ub, n_res * kv_rows, DSA_HD), lambda b, j: (b, j, 0, 0))],
        out_specs=[pl.BlockSpec(blk, lambda b, j: (b, 0)), pl.BlockSpec((ts, LANES), lambda b, j: (b, 0))],
        out_shape=[jax.ShapeDtypeStruct((bs * ts, DSA_GW), F32),
                   jax.ShapeDtypeStruct((bs * ts, LANES), F32)],
        scratch_shapes=[pltpu.VMEM((DSA_HEADS * ts, DSA_HD), F32)] * 3,
        compiler_params=_cparams("arbitrary", "arbitrary"),
        name=f"dsa_sample_w{window}",
    )(rq, rk, pm, cache4)


def _dsa_merge_body(o0, o1, o2, l0, l1, l2, e_ref, out_ref, *, dils):
    os_, ls_ = [], []
    for o_ref, l_ref, dil in zip((o0, o1, o2), (l0, l1, l2), dils):
        o = o_ref[...].reshape(ROW_TILE, DSA_GW)
        l = l_ref[...].reshape(ROW_TILE, LANES)
        if dil > 1:
            pinv = _residue_perm(dil, inverse=True)
            o = jnp.dot(pinv, o.astype(BF16), preferred_element_type=F32)
            l = _dot_exact_lhs(pinv, l)
        os_.append(o.astype(F32))
        ls_.append(l)
    mx = jnp.maximum(jnp.maximum(ls_[0], ls_[1]), ls_[2])
    ws = [jnp.exp(l - mx) for l in ls_]
    tot = ws[0] + ws[1] + ws[2]
    acc = None
    for w, o in zip(ws, os_):
        term = _dot_exact_rhs(w / tot, e_ref[...]) * o
        acc = term if acc is None else acc + term
    out_ref[...] = acc.astype(out_ref.dtype)


def dsa_merge(os_, ls_, dils, n_tiles):
    in_specs = []
    for width in (DSA_GW, LANES):
        for dil in dils:
            if dil:
                in_specs.append(pl.BlockSpec((dil, ROW_TILE // dil, width), lambda i: (0, i, 0)))
            else:
                in_specs.append(pl.BlockSpec((ROW_TILE, width), lambda i: (i, 0)))
    in_specs.append(pl.BlockSpec((LANES, DSA_GW), lambda i: (0, 0)))
    expand = jnp.asarray((np.arange(LANES)[:, None] == (np.arange(DSA_GW)[None, :] // DSA_HD)).astype(np.float32))
    return pl.pallas_call(
        functools.partial(_dsa_merge_body, dils=tuple(max(d, 1) for d in dils)),
        grid=(n_tiles,),
        in_specs=in_specs,
        out_specs=pl.BlockSpec((ROW_TILE, DSA_GW), lambda i: (i, 0)),
        out_shape=jax.ShapeDtypeStruct((n_tiles * ROW_TILE, DSA_GW), BF16),
        compiler_params=_cparams("arbitrary"),
        name="dsa_merge",
    )(*os_, *ls_, expand)


def _pad_lanes(vec, offset=0):
    return jnp.zeros((1, LANES), F32).at[0, offset:offset + vec.shape[0]].set(vec)


def kernel(x_prompt, x_sample, c_prompt, c_sample, state_gla, state_ssd, state_ssd_conv, state_gdn, state_gdn_conv, cache_dsa_w128, cache_dsa_w512, cache_dsa_w2048, ln_ffn1, ln_mix, ln_ffn2, w_ada, b_ada, ffn1_w_in, ffn1_w_out, ffn2_w_in, ffn2_w_out, even_w_in, gla_w_gate2, gla_b_gate, gla_norm, ssd_conv_w, ssd_conv_b, ssd_dt_bias, ssd_A_log, ssd_D, ssd_norm, even_w_out, odd_w_in, gdn_conv_w, gdn_A_log, gdn_dt_bias, gdn_norm, odd_w_out, final_norm):
    bp, tp, d = x_prompt.shape
    bs, ts, _ = x_sample.shape
    depth = w_ada.shape[0]
    geom = (bp, tp, bs, ts)
    assert tp % ROW_TILE == 0 and bs * ts == ROW_TILE and ts == SUBLANES
    mp = bp * tp
    caches = (cache_dsa_w128, cache_dsa_w512, cache_dsa_w2048)

    x = jnp.concatenate([x_prompt.reshape(mp, d), x_sample.reshape(bs * ts, d)], axis=0)
    n_pad = (-(bs + bp)) % SUBLANES
    c_all = jnp.concatenate([c_sample, c_prompt, jnp.zeros((n_pad, d), F32)], axis=0)
    mod4 = ada_mod(c_all, w_ada, b_ada).reshape(depth, c_all.shape[0], N_MOD, d)
    expand = jnp.asarray((np.arange(LANES)[:, None] - SSD_DT_LANE0
                          == (np.arange(SSD_INNER)[None, :] // SSD_P)).astype(np.float32))

    new ={k: ([], []) for k in ("gla", "ssd", "ssd_conv", "gdn", "gdn_conv", "dsa0", "dsa1", "dsa2")}

    h = rows_norm(x, mod4, ln_ffn1[0], geom, mod_layer=0, shift_idx=0, scale_idx=1)
    for l in range(depth):
        act = matmul_swiglu(h, ffn1_w_in, l)
        x = matmul_residual(act, ffn1_w_out, l, x, mod4, geom, gate_layer=l, gate_idx=2, coef=0.5, name="ffn_out")
        h = rows_norm(x, mod4, ln_mix[l], geom, mod_layer=l, shift_idx=3, scale_idx=4)
        i = l // 2
        if l % 2 == 0:
            assert even_w_in.shape[-1] == EVEN_IN
            pma = matmul(h, even_w_in, layer=i, n_cols=EVA_W, tn=1024, name="even_in_a")
            pmb = matmul_shift(h, even_w_in, i, EVB_COL, EVB_W, name="even_in_b")
            p_glr = matmul_cols128(h, even_w_in, i, EV_GLR_COL, name="even_in_glr")
            p_dt = matmul_cols128(h, even_w_in, i, EV_DT_COL, name="even_in_dt")
            wg_pad = jnp.zeros((LANES, GLA_QK), F32).at[:GLA_RANK].set(gla_w_gate2[i])
            dtb = _pad_lanes(ssd_dt_bias[i], SSD_DT_LANE0)
            alog = _pad_lanes(ssd_A_log[i], SSD_DT_LANE0)
            d_exp = jnp.repeat(ssd_D[i], SSD_P).reshape(1, SSD_INNER)
            mix = []
            for grp, (row0, nb, t, s_gla, s_ssd, s_conv) in enumerate((
                    (0, bp, tp, None, None, None),
                    (mp, bs, ts, state_gla[i], state_ssd[i], state_ssd_conv[i]))):
                o_gla, gla_new = gla_mixer(pma, p_glr, wg_pad, gla_b_gate[i], gla_norm[i], s_gla,
                                           row0=row0, nb=nb, t=t)
                y_ssd, ssd_new, conv_new = ssd_mixer(pmb, p_dt, ssd_conv_w[i], ssd_conv_b[i].reshape(1, -1), dtb,
                                                     alog, d_exp, ssd_norm[i], expand, s_ssd, s_conv,
                                                     row0=row0, nb=nb, t=t)
                new["gla"][grp].append(gla_new)
                new["ssd"][grp].append(ssd_new)
                new["ssd_conv"][grp].append(conv_new)
                mix += [o_gla.reshape(nb * t, GLA_VW), y_ssd.reshape(nb * t, SSD_INNER)]
            x = matmul_pair_residual(*mix, even_w_out[i], x, mod4, geom, gate_layer=l, gate_idx=5, coef=1.0,
                                     name="even_out")
        else:
            assert odd_w_in.shape[-1] == ODD_IN
            pma = matmul(h, odd_w_in, layer=i, n_cols=ODA_W, tn=1024, name="odd_in_a")
            pmb = matmul_shift(h, odd_w_in, i, ODB_COL, ODB_W, name="odd_in_b")
            p_ba = matmul_cols128(h, odd_w_in, i, OD_BA_COL, name="odd_in_ba")
            rq, rk, qkv = rope_qk(pmb, geom)
            alog = _pad_lanes(gdn_A_log[i], GDN_V_HEADS)
            dtb = _pad_lanes(gdn_dt_bias[i], GDN_V_HEADS)
            mix = []
            for grp, (row0, nb, t, s_gdn, s_conv) in enumerate((
                    (0, bp, tp, None, None), (mp, bs, ts, state_gdn[i], state_gdn_conv[i]))):
                o_gdn, gdn_new, conv_new = gdn_mixer(pma, p_ba, gdn_conv_w[i], alog, dtb, gdn_norm[i],
                                                     s_gdn, s_conv, row0=row0, nb=nb, t=t)
                new["gdn"][grp].append(gdn_new)
                new["gdn_conv"][grp].append(conv_new)
                os_, ls_ = [], []
                for g, (window, dil) in enumerate(DSA_GROUPS):
                    vcol = ODB_DV + g * DSA_GW
                    if grp == 0:
                        o_g, l_g = dsa_prompt(qkv[g], g, geom)
                        keep = min(window, tp)
                        kg = jnp.stack([rk[(b + 1) * tp - keep:(b + 1) * tp, g * DSA_GW:(g + 1) * DSA_GW]
                                        for b in range(bp)])
                        vg = jnp.stack([pmb[(b + 1) * tp - keep:(b + 1) * tp, vcol:vcol + DSA_GW]
                                        for b in range(bp)])
                    else:
                        o_g, l_g = dsa_sample(rq, rk, pmb, caches[g][i], g, geom)
                        keep = ts
                        kg = rk[mp:, g * DSA_GW:(g + 1) * DSA_GW]
                        vg = pmb[mp:, vcol:vcol + DSA_GW]
                    rows = jnp.stack([kg.reshape(nb, keep, DSA_HEADS, DSA_HD),
                                      vg.reshape(nb, keep, DSA_HEADS, DSA_HD)], axis=2)
                    new["dsa%d" % g][grp].append(rows)
                    os_.append(o_g)
                    ls_.append(l_g)
                if grp == 0:
                    o_dsa = dsa_merge(os_, ls_, [dil for _, dil in DSA_GROUPS], mp // ROW_TILE)
                else:
                    o_dsa = dsa_merge(os_, ls_, [0] * len(DSA_GROUPS), 1)
                mix += [o_gdn.reshape(nb * t, GDN_VW), o_dsa]
            x = matmul_pair_residual(*mix, odd_w_out[i], x, mod4, geom, gate_layer=l, gate_idx=5, coef=1.0,
                                     name="odd_out")
        h = rows_norm(x, mod4, ln_ffn2[l], geom, mod_layer=l, shift_idx=6, scale_idx=7)
        act = matmul_swiglu(h, ffn2_w_in, l)
        x = matmul_residual(act, ffn2_w_out, l, x, mod4, geom, gate_layer=l, gate_idx=8, coef=0.5, name="ffn_out")
        if l + 1 < depth:
            h = rows_norm(x, mod4, ln_ffn1[l + 1], geom, mod_layer=l + 1, shift_idx=0, scale_idx=1)
        else:
            y_p, y_s = rows_norm(x, mod4, final_norm, geom, modulate=False, split=True)

    y_prompt = y_p.reshape(bp, tp, d)
    y_sample = y_s.reshape(bs, ts, d)
    outs = [y_prompt, y_sample]
    for name in ("gla", "ssd", "ssd_conv", "gdn", "gdn_conv", "dsa0", "dsa1", "dsa2"):
        outs.append(jnp.stack(new[name][0]))
        outs.append(jnp.stack(new[name][1]))
    return tuple(outs)
```
